```python
import math
import jax, jax.numpy as jnp
from jax import lax
import numpy as np


D_MODEL = 2048
BATCH = 2
SEQ = 4096
DEPTH = 2
DEC_BATCH = 2
DEC_SEQ = 16384
PAST_LEN = 128

D_GROUP = D_MODEL // 2
LRU_BLOCKS = 8
LRU_BLOCK = D_GROUP // LRU_BLOCKS
LRU_CONV = 4
LRU_C = 8.0
DIFF_HEADS = 8
DIFF_DH = D_GROUP // (2 * DIFF_HEADS)
ATTN_BLOCK = 128
ROPE_THETA = 10000.0
HY_ORDER = 2
HY_CONV = 3
HY_EMB = 33
HY_FFN = 64
HY_FFN_DEPTH = 2
HY_FAST_DECAY = 0.3
HY_SLOW_DECAY = 1.5
HY_TARGET = 1e-2
RET_HEADS = 8
RET_DH = D_GROUP // RET_HEADS
RET_CHUNK = 128
N_EXPERTS = 16
D_EXPERT = D_MODEL
EC_CAPACITY = 2
EPS = 1e-6
N_EVEN = (DEPTH + 1) // 2
N_ODD = DEPTH // 2
F32 = jnp.float32

kernel_name = 'hybrid_bidir_lru_diffattn_hyena_retnet_ec'


def _rmsnorm(x, g):
    xf = x.astype(F32)
    y = xf * lax.rsqrt(jnp.mean(xf * xf, axis=-1, keepdims=True) + EPS)
    return (y * g.astype(F32)).astype(x.dtype)


def _dwconv(x, w, b, left):
    k = w.shape[0]
    L = x.shape[1]
    xp = jnp.pad(x, ((0, 0), (left, k - 1 - left), (0, 0)))
    y = b
    for j in range(k):
        y = y + xp[:, j:j + L] * w[j]
    return y


def _rope(x, inv_freq):
    L = x.shape[1]
    ang = jnp.arange(L, dtype=F32)[:, None] * inv_freq[None, :]
    cos = jnp.cos(ang)[None, :, None, :]
    sin = jnp.sin(ang)[None, :, None, :]
    x1, x2 = jnp.split(x, 2, axis=-1)
    return jnp.concatenate([x1 * cos - x2 * sin, x2 * cos + x1 * sin], axis=-1)


def _linear_scan(a, b):
    def combine(left, right):
        a1, b1 = left
        a2, b2 = right
        return a1 * a2, a2 * b1 + b2
    _, h = lax.associative_scan(combine, (a, b), axis=1)
    return h


def _rglru_direction(x, wa, ba, wx, bx, lam):
    B, L, C = x.shape
    xb = x.reshape(B, L, LRU_BLOCKS, LRU_BLOCK)
    r = jax.nn.sigmoid(jnp.einsum('blnc,ncd->blnd', xb, wa.astype(F32)).reshape(B, L, C) + ba.astype(F32))
    i = jax.nn.sigmoid(jnp.einsum('blnc,ncd->blnd', xb, wx.astype(F32)).reshape(B, L, C) + bx.astype(F32))
    log_a = -LRU_C * r * jax.nn.softplus(-lam.astype(F32))
    b = jnp.sqrt(-jnp.expm1(2.0 * log_a)) * (i * x)
    return _linear_scan(jnp.exp(log_a), b)


def _rglru_mixer(xr, gate, conv_w, conv_b, wa, ba, wx, bx, lam):
    xc = _dwconv(xr, conv_w, conv_b, LRU_CONV // 2).astype(F32)
    h_f = _rglru_direction(xc, wa[0], ba[0], wx[0], bx[0], lam[0])
    h_b = jnp.flip(_rglru_direction(jnp.flip(xc, 1), wa[1], ba[1], wx[1], bx[1], lam[1]), 1)
    return (h_f + h_b) * jax.nn.gelu(gate.astype(F32))


def _diff_attention(q, k, v, lam_vec, subln_g, lam_init):
    B, L = q.shape[:2]
    nb = L // ATTN_BLOCK
    inv_freq = ROPE_THETA ** (-jnp.arange(0, DIFF_DH, 2, dtype=F32) / DIFF_DH)
    q = _rope(q.astype(F32).reshape(B, L, 2 * DIFF_HEADS, DIFF_DH), inv_freq) * (DIFF_DH ** -0.5)
    k = _rope(k.astype(F32).reshape(B, L, 2 * DIFF_HEADS, DIFF_DH), inv_freq)
    qb = q.reshape(B, nb, ATTN_BLOCK, DIFF_HEADS, 2, DIFF_DH).transpose(1, 0, 3, 4, 2, 5)
    kt = k.reshape(B, L, DIFF_HEADS, 2, DIFF_DH).transpose(0, 2, 3, 1, 4)
    vt = v.astype(F32).reshape(B, L, DIFF_HEADS, 2 * DIFF_DH).transpose(0, 2, 1, 3)
    lv = lam_vec.astype(F32)
    lam = jnp.exp(jnp.sum(lv[0] * lv[1])) - jnp.exp(jnp.sum(lv[2] * lv[3])) + lam_init

    def block(qblk):
        s = jnp.einsum('bhiqd,bhikd->bhiqk', qblk, kt)
        p = jax.nn.softmax(s, axis=-1)
        return jnp.einsum('bhqk,bhkd->bhqd', p[:, :, 0] - lam * p[:, :, 1], vt)

    o = lax.map(block, qb)
    o = o.transpose(1, 0, 3, 2, 4).reshape(B, L, DIFF_HEADS, 2 * DIFF_DH)
    o = o * lax.rsqrt(jnp.mean(o * o, axis=-1, keepdims=True) + EPS) * subln_g.astype(F32) * (1.0 - lam_init)
    return o.reshape(B, L, D_GROUP)


def _hyena_filters(L, w1, b1, w2, b2, w3, freq):
    t = jnp.linspace(0.0, 1.0, L, dtype=F32)[:, None]
    bands = (HY_EMB - 1) // 2
    wpos = 2.0 * math.pi * jnp.arange(L, dtype=F32)[:, None] / L
    f = jnp.linspace(1e-4, bands - 1, bands, dtype=F32)[None, :]
    z = jnp.concatenate([t, jnp.cos(f * wpos), jnp.sin(f * wpos)], axis=-1)
    fr = freq.astype(F32)
    h = jnp.sin(fr * (z @ w1.astype(F32) + b1.astype(F32)))
    for j in range(HY_FFN_DEPTH):
        h = jnp.sin(fr * (h @ w2[j].astype(F32) + b2[j].astype(F32)))
    hf = (h @ w3.astype(F32)).reshape(L, HY_ORDER, 2, D_GROUP)
    max_decay = math.log(HY_TARGET) / HY_FAST_DECAY
    min_decay = math.log(HY_TARGET) / HY_SLOW_DECAY
    deltas = jnp.abs(jnp.linspace(min_decay, max_decay, D_GROUP, dtype=F32))
    hf = hf * jnp.exp(-t * deltas[None, :])[:, None, None, :]
    kfull = jnp.concatenate([hf[:, :, 0], jnp.zeros((1, HY_ORDER, D_GROUP), F32), jnp.flip(hf[:L - 1, :, 1], 0)], axis=0)
    kfull = kfull / jnp.sum(jnp.abs(kfull), axis=0, keepdims=True)
    return jnp.fft.rfft(kfull, axis=0)


def _hyena_mixer(u, conv_w, conv_b, w1, b1, w2, b2, w3, freq, bias):
    B, L, _ = u.shape
    uc = _dwconv(u, conv_w, conv_b, HY_CONV // 2).astype(F32)
    v, x1, x2 = jnp.split(uc, 3, axis=-1)
    kf = _hyena_filters(L, w1, b1, w2, b2, w3, freq)
    bias = bias.astype(F32)
    z = v
    for o, g in enumerate((x1, x2)):
        zf = jnp.fft.rfft(z, n=2 * L, axis=1)
        y = jnp.fft.irfft(zf * kf[None, :, o, :], n=2 * L, axis=1)[:, :L]
        z = g * (y + z * bias[o])
    return z


def _ret_chunk(q, k, v, lg, strict):
    B, H, L, d = q.shape
    n = L // RET_CHUNK
    qc = q.reshape(B, H, n, RET_CHUNK, d)
    kc = k.reshape(B, H, n, RET_CHUNK, d)
    vc = v.reshape(B, H, n, RET_CHUNK, v.shape[-1])
    idx = jnp.arange(RET_CHUNK, dtype=F32)
    diff = idx[:, None] - idx[None, :]
    mask = (diff > 0) if strict else (diff >= 0)
    dmat = jnp.where(mask[None], jnp.exp(jnp.where(mask, diff, 0.0)[None] * lg[:, None, None]), 0.0)
    s = jnp.einsum('bhncd,bhnmd->bhncm', qc, kc) * dmat[None, :, None]
    inner = jnp.einsum('bhncm,bhnme->bhnce', s, vc)
    kdec = jnp.exp((RET_CHUNK - 1 - idx)[None, :] * lg[:, None])
    qdec = jnp.exp((idx + 1.0)[None, :] * lg[:, None])
    kv = jnp.einsum('bhncd,hc,bhnce->nbhde', kc, kdec, vc)
    cdec = jnp.exp(RET_CHUNK * lg)[None, :, None, None]

    def step(state, kv_n):
        return state * cdec + kv_n, state

    _, prev = lax.scan(step, jnp.zeros(kv.shape[1:], F32), kv)
    cross = jnp.einsum('bhncd,hc,nbhde->bhnce', qc, qdec, prev)
    return (inner + cross).reshape(B, H, L, v.shape[-1])


def _retention_mixer(q, k, v, g, rho):
    B, L, _ = q.shape
    inv_freq = 1.0 / (10000.0 ** jnp.linspace(0.0, 1.0, RET_DH // 2, dtype=F32))
    q = _rope(q.astype(F32).reshape(B, L, RET_HEADS, RET_DH), inv_freq).transpose(0, 2, 1, 3)
    k = (_rope(k.astype(F32).reshape(B, L, RET_HEADS, RET_DH), inv_freq) * (RET_DH ** -0.5)).transpose(0, 2, 1, 3)
    v = v.astype(F32).reshape(B, L, RET_HEADS, RET_DH).transpose(0, 2, 1, 3)
    lg = -jax.nn.softplus(-rho.astype(F32))
    y_f = _ret_chunk(q, k, v, lg[0], False)
    y_b = jnp.flip(_ret_chunk(jnp.flip(q, 2), jnp.flip(k, 2), jnp.flip(v, 2), lg[1], True), 2)
    y = (y_f + y_b).transpose(0, 2, 1, 3)
    y = y * lax.rsqrt(jnp.mean(y * y, axis=-1, keepdims=True) + EPS)
    y = y * jax.nn.silu(g.astype(F32).reshape(B, L, RET_HEADS, RET_DH))
    return y.reshape(B, L, D_GROUP)


def _even_mixer(h, w_in, conv_w, conv_b, wa, ba, wx, bx, lam, diff_lam, subln, w_out, layer):
    proj = h @ w_in
    xr, gate, q, k, v = jnp.split(proj, 5, axis=-1)
    a_out = _rglru_mixer(xr, gate, conv_w, conv_b, wa, ba, wx, bx, lam)
    lam_init = 0.8 - 0.6 * math.exp(-0.3 * layer)
    b_out = _diff_attention(q, k, v, diff_lam, subln, lam_init)
    return jnp.concatenate([a_out, b_out], axis=-1).astype(h.dtype) @ w_out


def _odd_mixer(h, w_in, conv_w, conv_b, w1, b1, w2, b2, w3, freq, bias, rho, w_out):
    proj = h @ w_in
    u = proj[..., :3 * D_GROUP]
    q, k, v, g = jnp.split(proj[..., 3 * D_GROUP:], 4, axis=-1)
    c_out = _hyena_mixer(u, conv_w, conv_b, w1, b1, w2, b2, w3, freq, bias)
    d_out = _retention_mixer(q, k, v, g, rho)
    return jnp.concatenate([c_out, d_out], axis=-1).astype(h.dtype) @ w_out


def _ec_moe(x, router, w_gate, w_up, w_down):
    B, L, D = x.shape
    n = B * L
    xf = x.reshape(n, D)
    cap = EC_CAPACITY * n // N_EXPERTS
    aff = jax.nn.softmax((xf @ router).astype(F32), axis=-1)
    gates, idx = lax.top_k(aff.T, cap)
    xe = xf[idx]
    hid = jax.nn.silu(jnp.einsum('ecd,edf->ecf', xe, w_gate)) * jnp.einsum('ecd,edf->ecf', xe, w_up)
    ye = jnp.einsum('ecf,efd->ecd', hid, w_down) * gates[..., None].astype(x.dtype)
    out = jnp.zeros_like(xf).at[idx.reshape(-1)].add(ye.reshape(-1, D))
    return out.reshape(B, L, D)


def _trunk(x, w):
    (ev_norm, ev_w_in, ev_conv_w, ev_conv_b, ev_rg_wa, ev_rg_ba, ev_rg_wx, ev_rg_bx, ev_rg_lam,
     ev_diff_lam, ev_subln, ev_w_out, od_norm, od_w_in, od_conv_w, od_conv_b, od_flt_w1, od_flt_b1,
     od_flt_w2, od_flt_b2, od_flt_w3, od_flt_freq, od_flt_bias, od_ret_rho, od_w_out,
     moe_norm, moe_router, moe_w_gate, moe_w_up, moe_w_down, final_norm) = w
    for layer in range(DEPTH):
        j = layer // 2
        if layer % 2 == 0:
            h = _rmsnorm(x, ev_norm[j])
            x = x + _even_mixer(h, ev_w_in[j], ev_conv_w[j], ev_conv_b[j], ev_rg_wa[j], ev_rg_ba[j],
                                ev_rg_wx[j], ev_rg_bx[j], ev_rg_lam[j], ev_diff_lam[j], ev_subln[j],
                                ev_w_out[j], layer)
        else:
            h = _rmsnorm(x, od_norm[j])
            x = x + _odd_mixer(h, od_w_in[j], od_conv_w[j], od_conv_b[j], od_flt_w1[j], od_flt_b1[j],
                               od_flt_w2[j], od_flt_b2[j], od_flt_w3[j], od_flt_freq[j], od_flt_bias[j],
                               od_ret_rho[j], od_w_out[j])
        x = x + _ec_moe(_rmsnorm(x, moe_norm[layer]), moe_router[layer], moe_w_gate[layer],
                        moe_w_up[layer], moe_w_down[layer])
    return _rmsnorm(x, final_norm)


def _normal(k, shape, scale):
    return jax.random.normal(k, shape, F32) * scale


def setup_inputs(seed: int = 0) -> dict:
    key = jax.random.key(seed)
    ks = iter(jax.random.split(key, 40))
    D = D_MODEL
    u = jax.random.uniform(next(ks), (N_EVEN, 2, D_GROUP), F32, minval=0.9, maxval=0.999)
    sp = -jnp.log(u) / LRU_C
    rg_lam = -jnp.log(jnp.expm1(sp))
    gam = 1.0 - 2.0 ** (-5.0 - np.arange(RET_HEADS, dtype=np.float32))
    rho0 = jnp.asarray(np.log(gam / (1.0 - gam)).astype(np.float32))
    return {
        'x_prompt': _normal(next(ks), (BATCH, SEQ, D), 1.0),
        'x_sample': _normal(next(ks), (DEC_BATCH, DEC_SEQ, D), 1.0),
        'ev_norm': 1.0 + _normal(next(ks), (N_EVEN, D), 0.02),
        'ev_w_in': _normal(next(ks), (N_EVEN, D, 5 * D_GROUP), D ** -0.5),
        'ev_conv_w': _normal(next(ks), (N_EVEN, LRU_CONV, D_GROUP), LRU_CONV ** -0.5),
        'ev_conv_b': _normal(next(ks), (N_EVEN, D_GROUP), 0.02),
        'ev_rg_wa': _normal(next(ks), (N_EVEN, 2, LRU_BLOCKS, LRU_BLOCK, LRU_BLOCK), LRU_BLOCK ** -0.5),
        'ev_rg_ba': _normal(next(ks), (N_EVEN, 2, D_GROUP), 0.02),
        'ev_rg_wx': _normal(next(ks), (N_EVEN, 2, LRU_BLOCKS, LRU_BLOCK, LRU_BLOCK), LRU_BLOCK ** -0.5),
        'ev_rg_bx': _normal(next(ks), (N_EVEN, 2, D_GROUP), 0.02),
        'ev_rg_lam': rg_lam,
        'ev_diff_lam': _normal(next(ks), (N_EVEN, 4, DIFF_DH), 0.1),
        'ev_subln': 1.0 + _normal(next(ks), (N_EVEN, 2 * DIFF_DH), 0.02),
        'ev_w_out': _normal(next(ks), (N_EVEN, 2 * D_GROUP, D), (2 * D_GROUP) ** -0.5),
        'od_norm': 1.0 + _normal(next(ks), (N_ODD, D), 0.02),
        'od_w_in': _normal(next(ks), (N_ODD, D, 7 * D_GROUP), D ** -0.5),
        'od_conv_w': _normal(next(ks), (N_ODD, HY_CONV, 3 * D_GROUP), HY_CONV ** -0.5),
        'od_conv_b': _normal(next(ks), (N_ODD, 3 * D_GROUP), 0.02),
        'od_flt_w1': _normal(next(ks), (N_ODD, HY_EMB, HY_FFN), HY_EMB ** -0.5),
        'od_flt_b1': _normal(next(ks), (N_ODD, HY_FFN), 0.1),
        'od_flt_w2': _normal(next(ks), (N_ODD, HY_FFN_DEPTH, HY_FFN, HY_FFN), HY_FFN ** -0.5),
        'od_flt_b2': _normal(next(ks), (N_ODD, HY_FFN_DEPTH, HY_FFN), 0.1),
        'od_flt_w3': _normal(next(ks), (N_ODD, HY_FFN, HY_ORDER * 2 * D_GROUP), HY_FFN ** -0.5),
        'od_flt_freq': 1.0 + _normal(next(ks), (N_ODD, HY_FFN), 0.02),
        'od_flt_bias': _normal(next(ks), (N_ODD, HY_ORDER, D_GROUP), 1.0),
        'od_ret_rho': rho0 + _normal(next(ks), (N_ODD, 2, RET_HEADS), 0.01),
        'od_w_out': _normal(next(ks), (N_ODD, 2 * D_GROUP, D), (2 * D_GROUP) ** -0.5),
        'moe_norm': 1.0 + _normal(next(ks), (DEPTH, D), 0.02),
        'moe_router': _normal(next(ks), (DEPTH, D, N_EXPERTS), D ** -0.5),
        'moe_w_gate': _normal(next(ks), (DEPTH, N_EXPERTS, D, D_EXPERT), D ** -0.5),
        'moe_w_up': _normal(next(ks), (DEPTH, N_EXPERTS, D, D_EXPERT), D ** -0.5),
        'moe_w_down': _normal(next(ks), (DEPTH, N_EXPERTS, D_EXPERT, D), D_EXPERT ** -0.5),
        'final_norm': 1.0 + _normal(next(ks), (D,), 0.02),
    }


def reference(x_prompt, x_sample, ev_norm, ev_w_in, ev_conv_w, ev_conv_b, ev_rg_wa, ev_rg_ba, ev_rg_wx,
              ev_rg_bx, ev_rg_lam, ev_diff_lam, ev_subln, ev_w_out, od_norm, od_w_in, od_conv_w, od_conv_b,
              od_flt_w1, od_flt_b1, od_flt_w2, od_flt_b2, od_flt_w3, od_flt_freq, od_flt_bias, od_ret_rho,
              od_w_out, moe_norm, moe_router, moe_w_gate, moe_w_up, moe_w_down, final_norm):
    weights = (ev_norm, ev_w_in, ev_conv_w, ev_conv_b, ev_rg_wa, ev_rg_ba, ev_rg_wx, ev_rg_bx, ev_rg_lam,
               ev_diff_lam, ev_subln, ev_w_out, od_norm, od_w_in, od_conv_w, od_conv_b, od_flt_w1, od_flt_b1,
               od_flt_w2, od_flt_b2, od_flt_w3, od_flt_freq, od_flt_bias, od_ret_rho, od_w_out,
               moe_norm, moe_router, moe_w_gate, moe_w_up, moe_w_down, final_norm)
    y_prompt = _trunk(x_prompt, weights)
    y_sample = _trunk(x_sample, weights)
    return (y_prompt, y_sample)
```

```python
import functools
import math

import jax
import jax.numpy as jnp
import numpy as np
from jax import lax
from jax.experimental import pallas as pl
from jax.experimental.pallas import tpu as pltpu

F32 = jnp.float32
BF16 = jnp.bfloat16
I32 = jnp.int32

D_MODEL = 2048
D_GROUP = 1024
LANES = 128
N_HEAD_BLOCKS = D_GROUP // LANES
LRU_C = 8.0
DIFF_DH = 64
ROPE_THETA = 10000.0
RET_DH = 128
HY_EMB = 33
HY_FFN = 64
HY_FAST_DECAY = 0.3
HY_SLOW_DECAY = 1.5
HY_TARGET = 1e-2
N_EXPERTS = 16
EC_CAPACITY = 2
EPS = 1e-6
FFT_N2 = 128
HALO = 8
VMEM_LIMIT_MB = 56


def _cp(sem, vmem_mb=VMEM_LIMIT_MB):
    return pltpu.CompilerParams(dimension_semantics=sem, vmem_limit_bytes=vmem_mb * 1024 * 1024)


def _sigmoid(x):
    return 1.0 / (1.0 + jnp.exp(-x))


def _softplus(x):
    return jnp.maximum(x, 0.0) + jnp.log(1.0 + jnp.exp(-jnp.abs(x)))


def _dot(a, b):
    return jnp.dot(a, b, preferred_element_type=F32)


def _dot_nt(a, b):
    return lax.dot_general(a, b, (((1,), (1,)), ((), ())), preferred_element_type=F32)


def _rms_matmul_kernel(x_ref, g_ref, w_ref, o_ref, h_ref):
    @pl.when(pl.program_id(1) == 0)
    def _():
        xv = x_ref[...]
        ms = jnp.mean(xv * xv, axis=-1, keepdims=True)
        h_ref[...] = (xv * lax.rsqrt(ms + EPS) * g_ref[...]).astype(BF16)

    o_ref[...] = _dot(h_ref[...], w_ref[...])


def _rms_matmul(x, g, w_bf, tm=512, tn=1024):
    n, d = x.shape
    nout = w_bf.shape[1]
    return pl.pallas_call(
        _rms_matmul_kernel,
        grid=(n // tm, nout // tn),
        in_specs=[
            pl.BlockSpec((tm, d), lambda i, j: (i, 0)),
            pl.BlockSpec((1, d), lambda i, j: (0, 0)),
            pl.BlockSpec((d, tn), lambda i, j: (0, j)),
        ],
        out_specs=pl.BlockSpec((tm, tn), lambda i, j: (i, j)),
        out_shape=jax.ShapeDtypeStruct((n, nout), F32),
        scratch_shapes=[pltpu.VMEM((tm, d), BF16)],
        compiler_params=_cp(("parallel", "arbitrary")),
    )(x, g.reshape(1, d), w_bf)


def _out_matmul_kernel(a_ref, b_ref, wa_ref, wb_ref, x_ref, o_ref):
    o_ref[...] = x_ref[...] + _dot(a_ref[...], wa_ref[...]) + _dot(b_ref[...], wb_ref[...])


def _out_matmul(a, b, w_bf, x, tm=512, tn=1024):
    n, d = x.shape
    dg = a.shape[1]
    return pl.pallas_call(
        _out_matmul_kernel,
        grid=(n // tm, d // tn),
        in_specs=[
            pl.BlockSpec((tm, dg), lambda i, j: (i, 0)),
            pl.BlockSpec((tm, dg), lambda i, j: (i, 0)),
            pl.BlockSpec((dg, tn), lambda i, j: (0, j)),
            pl.BlockSpec((dg, tn), lambda i, j: (1, j)),
            pl.BlockSpec((tm, tn), lambda i, j: (i, j)),
        ],
        out_specs=pl.BlockSpec((tm, tn), lambda i, j: (i, j)),
        out_shape=jax.ShapeDtypeStruct((n, d), F32),
        compiler_params=_cp(("parallel", "parallel")),
    )(a, b, w_bf, w_bf, x)


def _lru_kernel(*refs, reverse, T, nt):
    if reverse:
        (x_ref, prev_ref, next_ref, cw_ref, cb_ref, w_ref, ba_ref, bx_ref, lam_ref,
         gate_ref, hf_ref, o_ref, a_s, b_s, carry, h_s) = refs
    else:
        (x_ref, prev_ref, next_ref, cw_ref, cb_ref, w_ref, ba_ref, bx_ref, lam_ref,
         o_ref, a_s, b_s, carry) = refs
    i = pl.program_id(1)
    ti = (nt - 1 - i) if reverse else i

    @pl.when(i == 0)
    def _():
        carry[...] = jnp.zeros_like(carry)

    x = x_ref[0]
    prev = jnp.where(ti == 0, 0.0, prev_ref[0])
    nxt = jnp.where(ti == nt - 1, 0.0, next_ref[0])
    xe = jnp.concatenate([prev, x, nxt], axis=0)
    cw = cw_ref[...]
    xc = (cb_ref[...] + cw[0:1] * xe[HALO - 2:HALO - 2 + T] + cw[1:2] * xe[HALO - 1:HALO - 1 + T]
          + cw[2:3] * xe[HALO:HALO + T] + cw[3:4] * xe[HALO + 1:HALO + 1 + T])
    sp = _softplus(-lam_ref[...])
    for blk in range(N_HEAD_BLOCKS):
        sl = slice(blk * LANES, (blk + 1) * LANES)
        xb = xc[:, sl]
        ri = _dot(xb.astype(BF16), w_ref[blk])
        r = _sigmoid(ri[:, :LANES] + ba_ref[:, sl])
        ig = _sigmoid(ri[:, LANES:] + bx_ref[:, sl])
        a = jnp.exp(-LRU_C * r * sp[:, sl])
        a_s[:, sl] = a
        b_s[:, sl] = jnp.sqrt(1.0 - a * a) * (ig * xb)

    dst = h_s if reverse else o_ref.at[0]

    def body(k, h):
        t = (T - 1 - k) if reverse else k
        h = a_s[pl.ds(t, 1), :] * h + b_s[pl.ds(t, 1), :]
        dst[pl.ds(t, 1), :] = h
        return h

    carry[...] = lax.fori_loop(0, T, body, carry[...], unroll=8)

    if reverse:
        g = gate_ref[0]
        gelu = 0.5 * g * (1.0 + jnp.tanh(math.sqrt(2.0 / math.pi) * (g + 0.044715 * g * g * g)))
        o_ref[0] = ((hf_ref[0] + h_s[...]) * gelu).astype(BF16)


def _lru(proj, hf, conv_w, conv_b, wcat, ba, bx, lam, *, reverse, T=512):
    B, L, _ = proj.shape
    T = min(T, L)
    nt = L // T
    hb = T // HALO
    nh = L // HALO

    def tile(i):
        return (nt - 1 - i) if reverse else i

    x_spec = pl.BlockSpec((1, T, D_GROUP), lambda b, i: (b, tile(i), 0))
    prev_spec = pl.BlockSpec((1, HALO, D_GROUP), lambda b, i: (b, jnp.maximum(tile(i) * hb - 1, 0), 0))
    next_spec = pl.BlockSpec((1, HALO, D_GROUP), lambda b, i: (b, jnp.minimum((tile(i) + 1) * hb, nh - 1), 0))
    vec = pl.BlockSpec((1, D_GROUP), lambda b, i: (0, 0))
    in_specs = [x_spec, prev_spec, next_spec,
                pl.BlockSpec((4, D_GROUP), lambda b, i: (0, 0)), vec,
                pl.BlockSpec((N_HEAD_BLOCKS, LANES, 2 * LANES), lambda b, i: (0, 0, 0)), vec, vec, vec]
    args = [proj, proj, proj, conv_w, conv_b.reshape(1, -1), wcat, ba.reshape(1, -1), bx.reshape(1, -1),
            lam.reshape(1, -1)]
    scratch = [pltpu.VMEM((T, D_GROUP), F32), pltpu.VMEM((T, D_GROUP), F32), pltpu.VMEM((1, D_GROUP), F32)]
    if reverse:
        in_specs += [pl.BlockSpec((1, T, D_GROUP), lambda b, i: (b, tile(i), 1)),
                     pl.BlockSpec((1, T, D_GROUP), lambda b, i: (b, tile(i), 0))]
        args += [proj, hf]
        scratch += [pltpu.VMEM((T, D_GROUP), F32)]
        out_dtype = BF16
    else:
        out_dtype = F32
    return pl.pallas_call(
        functools.partial(_lru_kernel, reverse=reverse, T=T, nt=nt),
        grid=(B, nt),
        in_specs=in_specs,
        out_specs=pl.BlockSpec((1, T, D_GROUP), lambda b, i: (b, tile(i), 0)),
        out_shape=jax.ShapeDtypeStruct((B, L, D_GROUP), out_dtype),
        scratch_shapes=scratch,
        compiler_params=_cp(("parallel", "arbitrary")),
    )(*args)


def _rope_kernel(x_ref, c_ref, sa_ref, sb_ref, o_ref, *, sh_a, sh_b, scale):
    c = c_ref[...]
    sa = sa_ref[...]
    sb = sb_ref[...]
    for h in range(N_HEAD_BLOCKS):
        sl = slice(h * LANES, (h + 1) * LANES)
        x = x_ref[0, :, sl]
        y = x * c + pltpu.roll(x, sh_a, 1) * sa + pltpu.roll(x, sh_b, 1) * sb
        o_ref[0, :, sl] = (y * scale).astype(o_ref.dtype)


def _rope(proj, col_block, tabs, *, sh_a, sh_b, scale, T=512):
    B, L, _ = proj.shape
    T = min(T, L)
    tab_spec = pl.BlockSpec((T, LANES), lambda b, i: (i, 0))
    return pl.pallas_call(
        functools.partial(_rope_kernel, sh_a=sh_a, sh_b=sh_b, scale=scale),
        grid=(B, L // T),
        in_specs=[pl.BlockSpec((1, T, D_GROUP), lambda b, i: (b, i, col_block)), tab_spec, tab_spec, tab_spec],
        out_specs=pl.BlockSpec((1, T, D_GROUP), lambda b, i: (b, i, 0)),
        out_shape=jax.ShapeDtypeStruct((B, L, D_GROUP), BF16),
        compiler_params=_cp(("parallel", "parallel")),
    )(proj, *tabs)


def _rope_tables(L, dh, inv_freq):
    lane = np.arange(LANES)
    d = lane % dh
    half = dh // 2
    ang = jnp.arange(L, dtype=F32)[:, None] * inv_freq[d % half][None, :]
    cos = jnp.cos(ang)
    sin = jnp.sin(ang)
    lo = jnp.asarray(d < half)[None, :]
    return cos, jnp.where(lo, -sin, 0.0), jnp.where(lo, 0.0, sin)


def _vt_kernel(v_ref, o_ref):
    for h in range(N_HEAD_BLOCKS):
        sl = slice(h * LANES, (h + 1) * LANES)
        o_ref[0, sl, :] = v_ref[0, :, sl].T.astype(BF16)


def _v_transpose(proj, col_block, T=512):
    B, L, _ = proj.shape
    T = min(T, L)
    return pl.pallas_call(
        _vt_kernel,
        grid=(B, L // T),
        in_specs=[pl.BlockSpec((1, T, D_GROUP), lambda b, i: (b, i, col_block))],
        out_specs=pl.BlockSpec((1, D_GROUP, T), lambda b, i: (b, 0, i)),
        out_shape=jax.ShapeDtypeStruct((B, D_GROUP, L), BF16),
        compiler_params=_cp(("parallel", "parallel")),
    )(proj)


def _attn_kernel(q_ref, k_ref, vt_ref, lv_ref, sg_ref, o_ref, q2_s, m_s, l_s, acc_s, *, tq, nk, lam_init):
    ki = pl.program_id(3)

    @pl.when(ki == 0)
    def _():
        q = q_ref[0]
        lane = lax.broadcasted_iota(I32, q.shape, 1)
        zero = jnp.zeros_like(q)
        q2_s[0:tq, :] = jnp.where(lane < DIFF_DH, q, zero)
        q2_s[tq:2 * tq, :] = jnp.where(lane >= DIFF_DH, q, zero)
        m_s[...] = jnp.full_like(m_s, -jnp.inf)
        l_s[...] = jnp.zeros_like(l_s)
        acc_s[...] = jnp.zeros_like(acc_s)

    s = _dot_nt(k_ref[0], q2_s[...])
    m_prev = m_s[...]
    m_new = jnp.maximum(m_prev, jnp.max(s, axis=0, keepdims=True))
    alpha = jnp.exp(m_prev - m_new)
    p = jnp.exp(s - m_new)
    l_s[...] = alpha * l_s[...] + jnp.sum(p, axis=0, keepdims=True)
    acc_s[...] = alpha * acc_s[...] + _dot(vt_ref[0], p.astype(BF16))
    m_s[...] = m_new

    @pl.when(ki == nk - 1)
    def _():
        o = acc_s[...] / l_s[...]
        lv = lv_ref[...]
        lam = (jnp.exp(jnp.sum(lv[0:1] * lv[1:2], axis=1, keepdims=True))
               - jnp.exp(jnp.sum(lv[2:3] * lv[3:4], axis=1, keepdims=True)) + lam_init)
        od = o[:, 0:tq] - lam * o[:, tq:2 * tq]
        ms = jnp.mean(od * od, axis=0, keepdims=True)
        y = od * lax.rsqrt(ms + EPS) * sg_ref[...] * (1.0 - lam_init)
        o_ref[0] = y.T.astype(BF16)


def _diff_attention(qr, kr, vt, diff_lam, subln, lam_init, tq=512, tk=512):
    B, L, _ = qr.shape
    tq = min(tq, L)
    tk = min(tk, L)
    nq, nk = L // tq, L // tk
    return pl.pallas_call(
        functools.partial(_attn_kernel, tq=tq, nk=nk, lam_init=lam_init),
        grid=(B, N_HEAD_BLOCKS, nq, nk),
        in_specs=[
            pl.BlockSpec((1, tq, LANES), lambda b, h, qi, ki: (b, qi, h)),
            pl.BlockSpec((1, tk, LANES), lambda b, h, qi, ki: (b, ki, h)),
            pl.BlockSpec((1, LANES, tk), lambda b, h, qi, ki: (b, h, ki)),
            pl.BlockSpec((4, DIFF_DH), lambda b, h, qi, ki: (0, 0)),
            pl.BlockSpec((LANES, 1), lambda b, h, qi, ki: (0, 0)),
        ],
        out_specs=pl.BlockSpec((1, tq, LANES), lambda b, h, qi, ki: (b, qi, h)),
        out_shape=jax.ShapeDtypeStruct((B, L, D_GROUP), BF16),
        scratch_shapes=[pltpu.VMEM((2 * tq, LANES), BF16), pltpu.VMEM((1, 2 * tq), F32),
                        pltpu.VMEM((1, 2 * tq), F32), pltpu.VMEM((LANES, 2 * tq), F32)],
        compiler_params=_cp(("parallel", "parallel", "parallel", "arbitrary")),
    )(qr, kr, vt, diff_lam, subln.reshape(LANES, 1))


def _log_gamma(rho_ref):
    return -_softplus(-rho_ref[0, 0])


def _ret_state_kernel(kf_ref, vf_ref, kb_ref, vb_ref, rf_ref, rb_ref, sf_ref, sb_ref, sf_s, sb_s, *, C):
    n = pl.program_id(2)

    @pl.when(n == 0)
    def _():
        sf_s[...] = jnp.zeros_like(sf_s)
        sb_s[...] = jnp.zeros_like(sb_s)

    sf_ref[0, 0, 0] = sf_s[...].astype(BF16)
    sb_ref[0, 0, 0] = sb_s[...].astype(BF16)
    lgf = _log_gamma(rf_ref)
    lgb = _log_gamma(rb_ref)
    row = lax.broadcasted_iota(I32, (C, LANES), 0).astype(F32)
    kdf = kf_ref[0].astype(F32) * jnp.exp(lgf * (C - 1.0 - row))
    kdb = kb_ref[0].astype(F32) * jnp.exp(lgb * row)
    sf_s[...] = sf_s[...] * jnp.exp(lgf * C) + _dot(kdf.T.astype(BF16), vf_ref[0].astype(BF16))
    sb_s[...] = sb_s[...] * jnp.exp(lgb * C) + _dot(kdb.T.astype(BF16), vb_ref[0].astype(BF16))


def _ret_main_kernel(q_ref, k_ref, v_ref, g_ref, sf_ref, sb_ref, rf_ref, rb_ref, o_ref, *, C):
    lgf = _log_gamma(rf_ref)
    lgb = _log_gamma(rb_ref)
    q = q_ref[0]
    s = _dot_nt(q, k_ref[0])
    r = lax.broadcasted_iota(I32, (C, C), 0)
    c = lax.broadcasted_iota(I32, (C, C), 1)
    dist = (r - c).astype(F32)
    dmat = jnp.exp(jnp.where(r >= c, lgf * dist, -lgb * dist))
    inner = _dot((s * dmat).astype(BF16), v_ref[0].astype(BF16))
    row = lax.broadcasted_iota(I32, (C, LANES), 0).astype(F32)
    qf = q.astype(F32)
    qcat = jnp.concatenate([qf * jnp.exp(lgf * (row + 1.0)), qf * jnp.exp(lgb * (C - row))], axis=1)
    scat = jnp.concatenate([sf_ref[0, 0, 0], sb_ref[0, 0, 0]], axis=0)
    y = inner + _dot(qcat.astype(BF16), scat)
    y = y * lax.rsqrt(jnp.mean(y * y, axis=-1, keepdims=True) + EPS)
    g = g_ref[0]
    o_ref[0] = (y * (g * _sigmoid(g))).astype(BF16)


def _retention(proj, qr, kr, rho, v_blk0, g_blk0, C=256):
    B, L, _ = proj.shape
    H = N_HEAD_BLOCKS
    C = min(C, L)
    N = L // C
    rho4 = rho.reshape(2, H, 1, 1)
    rf_spec = pl.BlockSpec((1, 1, 1, 1), lambda b, h, n: (0, h, 0, 0))
    rb_spec = pl.BlockSpec((1, 1, 1, 1), lambda b, h, n: (1, h, 0, 0))
    st_shape = jax.ShapeDtypeStruct((B, H, N, LANES, LANES), BF16)
    sf, sb = pl.pallas_call(
        functools.partial(_ret_state_kernel, C=C),
        grid=(B, H, N),
        in_specs=[
            pl.BlockSpec((1, C, LANES), lambda b, h, n: (b, n, h)),
            pl.BlockSpec((1, C, LANES), lambda b, h, n: (b, n, v_blk0 + h)),
            pl.BlockSpec((1, C, LANES), lambda b, h, n: (b, N - 1 - n, h)),
            pl.BlockSpec((1, C, LANES), lambda b, h, n: (b, N - 1 - n, v_blk0 + h)),
            rf_spec, rb_spec,
        ],
        out_specs=[pl.BlockSpec((1, 1, 1, LANES, LANES), lambda b, h, n: (b, h, n, 0, 0)),
                   pl.BlockSpec((1, 1, 1, LANES, LANES), lambda b, h, n: (b, h, N - 1 - n, 0, 0))],
        out_shape=[st_shape, st_shape],
        scratch_shapes=[pltpu.VMEM((LANES, LANES), F32), pltpu.VMEM((LANES, LANES), F32)],
        compiler_params=_cp(("parallel", "parallel", "arbitrary")),
    )(kr, proj, kr, proj, rho4, rho4)
    st_spec = pl.BlockSpec((1, 1, 1, LANES, LANES), lambda b, h, n: (b, h, n, 0, 0))
    return pl.pallas_call(
        functools.partial(_ret_main_kernel, C=C),
        grid=(B, H, N),
        in_specs=[
            pl.BlockSpec((1, C, LANES), lambda b, h, n: (b, n, h)),
            pl.BlockSpec((1, C, LANES), lambda b, h, n: (b, n, h)),
            pl.BlockSpec((1, C, LANES), lambda b, h, n: (b, n, v_blk0 + h)),
            pl.BlockSpec((1, C, LANES), lambda b, h, n: (b, n, g_blk0 + h)),
            st_spec, st_spec, rf_spec, rb_spec,
        ],
        out_specs=pl.BlockSpec((1, C, LANES), lambda b, h, n: (b, n, h)),
        out_shape=jax.ShapeDtypeStruct((B, L, D_GROUP), BF16),
        compiler_params=_cp(("parallel", "parallel", "parallel")),
    )(qr, kr, proj, proj, sf, sb, rho4, rho4)


def _dwconv3_kernel(x_ref, prev_ref, next_ref, w_ref, b_ref, o_ref, *, T, nt):
    i = pl.program_id(1)
    x = x_ref[0]
    prev = jnp.where(i == 0, 0.0, prev_ref[0])
    nxt = jnp.where(i == nt - 1, 0.0, next_ref[0])
    xe = jnp.concatenate([prev, x, nxt], axis=0)
    w = w_ref[...]
    o_ref[0] = (b_ref[...] + w[0:1] * xe[HALO - 1:HALO - 1 + T] + w[1:2] * xe[HALO:HALO + T]
                + w[2:3] * xe[HALO + 1:HALO + 1 + T])


def _dwconv3(proj, conv_w, conv_b, T=512):
    B, L, _ = proj.shape
    T = min(T, L)
    nt = L // T
    hb = T // HALO
    nh = L // HALO
    W = 3 * D_GROUP
    return pl.pallas_call(
        functools.partial(_dwconv3_kernel, T=T, nt=nt),
        grid=(B, nt),
        in_specs=[
            pl.BlockSpec((1, T, W), lambda b, i: (b, i, 0)),
            pl.BlockSpec((1, HALO, W), lambda b, i: (b, jnp.maximum(i * hb - 1, 0), 0)),
            pl.BlockSpec((1, HALO, W), lambda b, i: (b, jnp.minimum((i + 1) * hb, nh - 1), 0)),
            pl.BlockSpec((3, W), lambda b, i: (0, 0)),
            pl.BlockSpec((1, W), lambda b, i: (0, 0)),
        ],
        out_specs=pl.BlockSpec((1, T, W), lambda b, i: (b, i, 0)),
        out_shape=jax.ShapeDtypeStruct((B, L, W), F32),
        compiler_params=_cp(("parallel", "parallel")),
    )(proj, proj, proj, conv_w, conv_b.reshape(1, W))


def _filter_mlp_kernel(z_ref, w1_ref, b1_ref, w2_ref, b2_ref, w3_ref, fr_ref, hf_ref, sum_ref, *, TL, L):
    i = pl.program_id(0)
    fr = fr_ref[...]
    h = jnp.sin(fr * (_dot(z_ref[...].astype(BF16), w1_ref[...]) + b1_ref[...]))
    for j in range(2):
        h = jnp.sin(fr * (_dot(h.astype(BF16), w2_ref[j]) + b2_ref[j]))
    hf = _dot(h.astype(BF16), w3_ref[...])
    row = i * TL + lax.broadcasted_iota(I32, (TL, D_GROUP), 0)
    t = row.astype(F32) / (L - 1.0)
    ch = lax.broadcasted_iota(I32, (TL, D_GROUP), 1).astype(F32)
    max_decay = math.log(HY_TARGET) / HY_FAST_DECAY
    min_decay = math.log(HY_TARGET) / HY_SLOW_DECAY
    delta = jnp.abs(min_decay + (max_decay - min_decay) * ch / (D_GROUP - 1.0))
    dec = jnp.exp(-t * delta)
    keep = row < L - 1

    @pl.when(i == 0)
    def _():
        sum_ref[...] = jnp.zeros_like(sum_ref)

    for q in range(4):
        sl = slice(q * D_GROUP, (q + 1) * D_GROUP)
        v = hf[:, sl] * dec
        hf_ref[:, sl] = v
        a = jnp.abs(v)
        if q % 2 == 1:
            a = jnp.where(keep, a, 0.0)
        sum_ref[:, sl] += jnp.sum(a, axis=0, keepdims=True)


def _filter_mlp(L, w1, b1, w2, b2, w3, freq, TL=512):
    TL = min(TL, L)
    t = jnp.linspace(0.0, 1.0, L, dtype=F32)[:, None]
    bands = (HY_EMB - 1) // 2
    wpos = 2.0 * math.pi * jnp.arange(L, dtype=F32)[:, None] / L
    f = jnp.linspace(1e-4, bands - 1, bands, dtype=F32)[None, :]
    z = jnp.concatenate([t, jnp.cos(f * wpos), jnp.sin(f * wpos)], axis=-1)
    z = jnp.pad(z, ((0, 0), (0, LANES - HY_EMB)))
    pf = LANES - HY_FFN
    w1p = jnp.pad(w1, ((0, LANES - HY_EMB), (0, pf))).astype(BF16)
    b1p = jnp.pad(b1, (0, pf)).reshape(1, LANES)
    w2p = jnp.pad(w2, ((0, 0), (0, pf), (0, pf))).astype(BF16)
    b2p = jnp.pad(b2, ((0, 0), (0, pf))).reshape(2, 1, LANES)
    w3p = jnp.pad(w3, ((0, pf), (0, 0))).astype(BF16)
    frp = jnp.pad(freq, (0, pf)).reshape(1, LANES)
    W = 4 * D_GROUP
    full = lambda *shape: pl.BlockSpec(shape, lambda i: (0,) * len(shape))
    return pl.pallas_call(
        functools.partial(_filter_mlp_kernel, TL=TL, L=L),
        grid=(L // TL,),
        in_specs=[pl.BlockSpec((TL, LANES), lambda i: (i, 0)), full(LANES, LANES), full(1, LANES),
                  full(2, LANES, LANES), full(2, 1, LANES), full(LANES, W), full(1, LANES)],
        out_specs=[pl.BlockSpec((TL, W), lambda i: (i, 0)), full(1, W)],
        out_shape=[jax.ShapeDtypeStruct((L, W), F32), jax.ShapeDtypeStruct((1, W), F32)],
        compiler_params=_cp(("arbitrary",)),
    )(z, w1p, b1p, w2p, b2p, w3p, frp)


def _dft_tables(N):
    N2 = FFT_N2
    N1 = N // N2
    k1 = np.arange(N1)
    a1 = 2.0 * math.pi * ((k1[:, None] * k1[None, :]) % N1) / N1
    f1 = jnp.asarray(np.concatenate([np.cos(a1), -np.sin(a1)], axis=0), F32)
    f1inv = jnp.asarray(np.concatenate([np.cos(a1), -np.sin(a1)], axis=1)[:N1 // 2] / N, F32)
    k2 = np.arange(N2)
    a2 = 2.0 * math.pi * ((k2[:, None] * k2[None, :]) % N2) / N2
    fr, fi = np.cos(a2), -np.sin(a2)
    f2 = jnp.asarray(np.block([[fr, -fi], [fi, fr]]), F32).astype(BF16)
    f2inv = jnp.asarray(np.block([[fr, fi], [-fi, fr]]), F32).astype(BF16)
    at = 2.0 * math.pi * (k2[:, None] * k1[None, :]) / N
    tw = lambda a: jnp.broadcast_to(jnp.asarray(a, F32)[:, :, None], a.shape + (LANES,))
    return dict(N1=N1, N2=N2, f1=f1.astype(BF16), f1half=f1[:, :N1 // 2].astype(BF16), f1inv=f1inv.astype(BF16),
                f2=f2, f2inv=f2inv, twa_c=tw(np.cos(at)), twa_s=tw(-np.sin(at)),
                twb_c=tw(np.cos(at.T)), twb_s=tw(np.sin(at.T)))


def _dft1_kernel(x_ref, f_ref, c_ref, s_ref, o_ref, *, N1, C):
    a = _dot(f_ref[...], x_ref[0].astype(BF16))
    c = c_ref[0]
    s = s_ref[0]
    for j in range(C // LANES):
        sl = slice(j * LANES, (j + 1) * LANES)
        ar = a[0:N1, sl]
        ai = a[N1:2 * N1, sl]
        o_ref[0, 0:N1, sl] = (ar * c - ai * s).astype(BF16)
        o_ref[0, N1:2 * N1, sl] = (ar * s + ai * c).astype(BF16)


def _dft1(x2d, fmat, tab, C):
    Bt, R, _ = x2d.shape
    N1, N2 = tab["N1"], tab["N2"]
    return pl.pallas_call(
        functools.partial(_dft1_kernel, N1=N1, C=C),
        grid=(Bt, N2),
        in_specs=[pl.BlockSpec((1, R, C), lambda b, t: (b, 0, t)),
                  pl.BlockSpec((2 * N1, R), lambda b, t: (0, 0)),
                  pl.BlockSpec((1, N1, LANES), lambda b, t: (t, 0, 0)),
                  pl.BlockSpec((1, N1, LANES), lambda b, t: (t, 0, 0))],
        out_specs=pl.BlockSpec((1, 2 * N1, C), lambda b, t: (b, 0, t)),
        out_shape=jax.ShapeDtypeStruct((Bt, 2 * N1, N2 * C), BF16),
        compiler_params=_cp(("parallel", "parallel")),
    )(x2d, fmat, tab["twa_c"], tab["twa_s"])


def _filter_dft2_kernel(a_ref, f_ref, inv_ref, o_ref, *, N2):
    x = a_ref[:, 0].reshape(2 * N2, a_ref.shape[-1])
    z = _dot(f_ref[...], x) * inv_ref[...]
    o_ref[:, 0] = z.reshape(2, N2, z.shape[-1]).astype(BF16)


def _filter_dft2(a4, tab, inv_sum, Ct=1024):
    _, N1, N2, Ck = a4.shape
    blk = pl.BlockSpec((2, 1, N2, Ct), lambda k, c: (0, k, 0, c))
    return pl.pallas_call(
        functools.partial(_filter_dft2_kernel, N2=N2),
        grid=(N1, Ck // Ct),
        in_specs=[blk, pl.BlockSpec((2 * N2, 2 * N2), lambda k, c: (0, 0)),
                  pl.BlockSpec((1, Ct), lambda k, c: (0, c))],
        out_specs=blk,
        out_shape=jax.ShapeDtypeStruct(a4.shape, BF16),
        compiler_params=_cp(("parallel", "parallel")),
    )(a4, tab["f2"], inv_sum)


def _conv_mid_kernel(a_ref, kf_ref, f_ref, fi_ref, c_ref, s_ref, o_ref, *, N2):
    Ct = a_ref.shape[-1]
    x = a_ref[0, :, 0].reshape(2 * N2, Ct)
    z = _dot(f_ref[...], x)
    zr, zi = z[0:N2], z[N2:2 * N2]
    kr = kf_ref[0, 0].astype(F32)
    ki = kf_ref[1, 0].astype(F32)
    y = jnp.concatenate([zr * kr - zi * ki, zr * ki + zi * kr], axis=0).astype(BF16)
    b = _dot(fi_ref[...], y)
    c = c_ref[0]
    s = s_ref[0]
    for j in range(Ct // LANES):
        sl = slice(j * LANES, (j + 1) * LANES)
        br = b[0:N2, sl]
        bi = b[N2:2 * N2, sl]
        o_ref[0, 0, 0, :, sl] = (br * c - bi * s).astype(BF16)
        o_ref[0, 1, 0, :, sl] = (br * s + bi * c).astype(BF16)


def _conv_mid(a5, kf4, order, tab, Ct=1024):
    B, _, N1, N2, C = a5.shape
    nc = C // Ct
    sq = pl.BlockSpec((2 * N2, 2 * N2), lambda b, k, c: (0, 0))
    twb = pl.BlockSpec((1, N2, LANES), lambda b, k, c: (k, 0, 0))
    blk = pl.BlockSpec((1, 2, 1, N2, Ct), lambda b, k, c: (b, 0, k, 0, c))
    return pl.pallas_call(
        functools.partial(_conv_mid_kernel, N2=N2),
        grid=(B, N1, nc),
        in_specs=[blk, pl.BlockSpec((2, 1, N2, Ct), lambda b, k, c: (0, k, 0, order * nc + c)), sq, sq, twb, twb],
        out_specs=blk,
        out_shape=jax.ShapeDtypeStruct(a5.shape, BF16),
        compiler_params=_cp(("parallel", "parallel", "parallel")),
    )(a5, kf4, tab["f2"], tab["f2inv"], tab["twb_c"], tab["twb_s"])


def _conv_out_kernel(b_ref, f_ref, g_ref, z_ref, bias_ref, o_ref):
    y = _dot(f_ref[...], b_ref[0])
    o_ref[0] = (g_ref[0] * (y + z_ref[0] * bias_ref[...])).astype(o_ref.dtype)


def _conv_out(b3, tab, uc2d, g_blk, z2d, z_blk, bias, out_dtype):
    B, _, M = b3.shape
    N1, N2 = tab["N1"], tab["N2"]
    C = M // N2
    R = N1 // 2
    nb_g = uc2d.shape[-1] // (N2 * C)
    nb_z = z2d.shape[-1] // (N2 * C)
    return pl.pallas_call(
        _conv_out_kernel,
        grid=(B, N2),
        in_specs=[pl.BlockSpec((1, 2 * N1, C), lambda b, t: (b, 0, t)),
                  pl.BlockSpec((R, 2 * N1), lambda b, t: (0, 0)),
                  pl.BlockSpec((1, R, C), lambda b, t: (b, 0, t * nb_g + g_blk)),
                  pl.BlockSpec((1, R, C), lambda b, t: (b, 0, t * nb_z + z_blk)),
                  pl.BlockSpec((1, C), lambda b, t: (0, 0))],
        out_specs=pl.BlockSpec((1, R, C), lambda b, t: (b, 0, t)),
        out_shape=jax.ShapeDtypeStruct((B, R, M), out_dtype),
        compiler_params=_cp(("parallel", "parallel")),
    )(b3, tab["f1inv"], uc2d, z2d, bias.reshape(1, C))


def _hyena(proj, conv_w, conv_b, w1, b1, w2, b2, w3, freq, bias):
    B, L, _ = proj.shape
    C = D_GROUP
    N = 2 * L
    tab = _dft_tables(N)
    N1, N2 = tab["N1"], tab["N2"]
    R = N1 // 2
    uc = _dwconv3(proj, conv_w, conv_b)
    hf, sums = _filter_mlp(L, w1, b1, w2, b2, w3, freq)
    hf4 = hf.reshape(L, 2, 2, C)
    kfull = jnp.concatenate([hf4[:, :, 0], jnp.zeros((1, 2, C), F32), jnp.flip(hf4[:L - 1, :, 1], 0)], axis=0)
    s4 = sums.reshape(2, 2, C)
    inv_sum = (1.0 / (s4[:, 0] + s4[:, 1])).reshape(1, 2 * C)
    fa = _dft1(kfull.reshape(1, N1, N2 * 2 * C), tab["f1"], tab, 2 * C)
    kf4 = _filter_dft2(fa.reshape(2, N1, N2, 2 * C), tab, inv_sum)
    uc2d = uc.reshape(B, R, N2 * 3 * C)
    z2d, z_blk = uc2d, 0
    for o in range(2):
        if o == 0:
            a3 = _dft1_cols(uc2d, tab, C)
        else:
            a3 = _dft1(z2d, tab["f1half"], tab, C)
        b5 = _conv_mid(a3.reshape(B, 2, N1, N2, C), kf4, o, tab)
        z2d = _conv_out(b5.reshape(B, 2 * N1, N2 * C), tab, uc2d, 1 + o, z2d, z_blk, bias[o],
                        F32 if o == 0 else BF16)
        z_blk = 0
    return z2d.reshape(B, L, C)


def _dft1_cols(uc2d, tab, C):
    B, R, _ = uc2d.shape
    N1, N2 = tab["N1"], tab["N2"]
    return pl.pallas_call(
        functools.partial(_dft1_kernel, N1=N1, C=C),
        grid=(B, N2),
        in_specs=[pl.BlockSpec((1, R, C), lambda b, t: (b, 0, 3 * t)),
                  pl.BlockSpec((2 * N1, R), lambda b, t: (0, 0)),
                  pl.BlockSpec((1, N1, LANES), lambda b, t: (t, 0, 0)),
                  pl.BlockSpec((1, N1, LANES), lambda b, t: (t, 0, 0))],
        out_specs=pl.BlockSpec((1, 2 * N1, C), lambda b, t: (b, 0, t)),
        out_shape=jax.ShapeDtypeStruct((B, 2 * N1, N2 * C), BF16),
        compiler_params=_cp(("parallel", "parallel")),
    )(uc2d, tab["f1half"], tab["twa_c"], tab["twa_s"])


def _router_kernel(x_ref, g_ref, rt_ref, h_ref, aff_ref):
    xv = x_ref[...]
    ms = jnp.mean(xv * xv, axis=-1, keepdims=True)
    h = xv * lax.rsqrt(ms + EPS) * g_ref[...]
    h_ref[...] = h
    logits = lax.dot_general(rt_ref[...], h, (((1,), (1,)), ((), ())), precision=lax.Precision.HIGHEST,
                             preferred_element_type=F32)
    e = jnp.exp(logits - jnp.max(logits, axis=0, keepdims=True))
    aff_ref[...] = e / jnp.sum(e, axis=0, keepdims=True)


def _router(x, g, router_t, tm=512):
    n, d = x.shape
    return pl.pallas_call(
        _router_kernel,
        grid=(n // tm,),
        in_specs=[pl.BlockSpec((tm, d), lambda i: (i, 0)), pl.BlockSpec((1, d), lambda i: (0, 0)),
                  pl.BlockSpec((N_EXPERTS, d), lambda i: (0, 0))],
        out_specs=[pl.BlockSpec((tm, d), lambda i: (i, 0)), pl.BlockSpec((N_EXPERTS, tm), lambda i: (0, i))],
        out_shape=[jax.ShapeDtypeStruct((n, d), F32), jax.ShapeDtypeStruct((N_EXPERTS, n), F32)],
        compiler_params=_cp(("parallel",)),
    )(x, g.reshape(1, d), router_t)


def _select_kernel(aff_ref, pos_ref, cs_ref, gate_ref, u_s, m_s, *, n, cap, CH):
    E = N_EXPERTS
    bits = pltpu.bitcast(aff_ref[...], I32)

    def search(it, thr):
        cand = thr | jnp.left_shift(jnp.int32(1), 30 - it)
        cnt = jnp.sum((bits >= cand).astype(F32), axis=1, keepdims=True)
        return jnp.where(cnt >= cap, cand, thr)

    thr = lax.fori_loop(0, 31, search, jnp.zeros((E, 1), I32))
    need = cap - jnp.sum((bits > thr).astype(F32), axis=1, keepdims=True)
    r = lax.broadcasted_iota(I32, (CH, CH), 0)
    c = lax.broadcasted_iota(I32, (CH, CH), 1)
    u_s[...] = (r <= c).astype(BF16)

    def chunk_bits(j):
        off = pl.multiple_of(j * CH, CH)
        return off, pltpu.bitcast(aff_ref[:, pl.ds(off, CH)], I32)

    def ties(j, carry):
        off, b = chunk_bits(j)
        eq = (b == thr).astype(F32)
        incl = _dot(eq.astype(BF16), u_s[...]) + carry
        sel = (b > thr) | ((b == thr) & (incl - eq < need))
        m_s[:, pl.ds(off, CH)] = sel.astype(F32)
        return incl[:, CH - 1:CH]

    lax.fori_loop(0, n // CH, ties, jnp.zeros((E, 1), F32))

    def slots(j, carry):
        off, _ = chunk_bits(j)
        m = m_s[:, pl.ds(off, CH)]
        incl = _dot(m.astype(BF16), u_s[...]) + carry
        cs_ref[:, pl.ds(off, CH)] = incl.astype(I32)
        pos_ref[:, pl.ds(off, CH)] = jnp.where(m > 0.0, incl - 1.0, -1.0).astype(I32)
        gate_ref[:, pl.ds(off, CH)] = jnp.where(m > 0.0, aff_ref[:, pl.ds(off, CH)], 0.0)
        return incl[:, CH - 1:CH]

    lax.fori_loop(0, n // CH, slots, jnp.zeros((E, 1), F32))


def _select(aff, cap, CH=512):
    E, n = aff.shape
    CH = min(CH, n)
    full = pl.BlockSpec((E, n), lambda: (0, 0))
    return pl.pallas_call(
        functools.partial(_select_kernel, n=n, cap=cap, CH=CH),
        in_specs=[full],
        out_specs=[full, full, full],
        out_shape=[jax.ShapeDtypeStruct((E, n), I32), jax.ShapeDtypeStruct((E, n), I32),
                   jax.ShapeDtypeStruct((E, n), F32)],
        scratch_shapes=[pltpu.VMEM((CH, CH), BF16), pltpu.VMEM((E, n), F32)],
        compiler_params=_cp(None),
    )(aff)


def _compact_kernel(pos_ref, idx_ref, *, n, cap):
    def body(t, carry):
        p = pos_ref[t]
        idx_ref[jnp.where(p < 0, cap, p)] = t
        return carry

    lax.fori_loop(0, n, body, 0, unroll=8)

    def clear(j, carry):
        idx_ref[j] = 0
        return carry

    lax.fori_loop(cap + 1, 2 * cap, clear, 0, unroll=8)


def _compact(pos_flat, n, cap):
    return pl.pallas_call(
        functools.partial(_compact_kernel, n=n, cap=cap),
        grid=(N_EXPERTS,),
        in_specs=[pl.BlockSpec((n,), lambda e: (e,), memory_space=pltpu.SMEM)],
        out_specs=pl.BlockSpec((2 * cap,), lambda e: (e,), memory_space=pltpu.SMEM),
        out_shape=jax.ShapeDtypeStruct((N_EXPERTS * 2 * cap,), I32),
        compiler_params=_cp(("arbitrary",)),
    )(pos_flat)


def _expert_kernel(idx_ref, h_ref, wg_ref, wu_ref, wd_ref, o_ref, xbuf, xb, acc, sem, *, tm, nf):
    r = pl.program_id(1)
    f = pl.program_id(2)

    @pl.when(f == 0)
    def _():
        def issue(j, carry):
            t = idx_ref[r * tm + j]
            pltpu.make_async_copy(h_ref.at[pl.ds(t, 1)], xbuf.at[pl.ds(j, 1)], sem).start()
            return carry

        lax.fori_loop(0, tm, issue, 0, unroll=8)
        pltpu.make_async_copy(h_ref.at[pl.ds(0, tm)], xbuf, sem).wait()
        xb[...] = xbuf[...].astype(BF16)
        acc[...] = jnp.zeros_like(acc)

    x = xb[...]
    g = _dot(x, wg_ref[0])
    u = _dot(x, wu_ref[0])
    hid = (g * _sigmoid(g) * u).astype(BF16)
    acc[...] += _dot(hid, wd_ref[0])

    @pl.when(f == nf - 1)
    def _():
        o_ref[0] = acc[...].astype(BF16)


def _experts(idx_flat, h, wg, wu, wd, cap, tm=1024, tf=512):
    n, d = h.shape
    E, _, dff = wg.shape
    tm = min(tm, cap)
    nf = dff // tf
    return pl.pallas_call(
        functools.partial(_expert_kernel, tm=tm, nf=nf),
        grid=(E, cap // tm, nf),
        in_specs=[pl.BlockSpec((2 * cap,), lambda e, r, f: (e,), memory_space=pltpu.SMEM),
                  pl.BlockSpec(memory_space=pl.ANY),
                  pl.BlockSpec((1, d, tf), lambda e, r, f: (e, 0, f)),
                  pl.BlockSpec((1, d, tf), lambda e, r, f: (e, 0, f)),
                  pl.BlockSpec((1, tf, d), lambda e, r, f: (e, f, 0))],
        out_specs=pl.BlockSpec((1, tm, d), lambda e, r, f: (e, r, 0)),
        out_shape=jax.ShapeDtypeStruct((E, cap, d), BF16),
        scratch_shapes=[pltpu.VMEM((tm, d), F32), pltpu.VMEM((tm, d), BF16), pltpu.VMEM((tm, d), F32),
                        pltpu.SemaphoreType.DMA(())],
        compiler_params=_cp(("parallel", "arbitrary", "arbitrary")),
    )(idx_flat, h, wg, wu, wd)


COMBINE_WIN = 128
ROW_ALIGN = 16


def _combine_kernel(st_ref, x_ref, pos_ref, gate_ref, ye_ref, o_ref, buf, sems, buf2, sem2, c_s, *, TT, cap):
    i = pl.program_id(0)
    E = N_EXPERTS
    W = COMBINE_WIN

    def window(e, w):
        s0 = st_ref[i * E + e]
        lo = lax.shift_left(lax.shift_right_logical(s0, 4), 4) + W * w
        return jnp.minimum(lo, cap - W), lo

    def first_copy(e):
        a, _ = window(e, 0)
        return pltpu.make_async_copy(ye_ref.at[e, pl.ds(pl.multiple_of(a, ROW_ALIGN), W)], buf.at[e], sems.at[e])

    for e in range(E):
        first_copy(e).start()
    o_ref[...] = x_ref[...]
    lane = lax.broadcasted_iota(I32, (TT, W), 1)

    def onehot(pc, a, lo):
        hit = (pc - a == lane) & (pc >= lo) & (pc < lo + W)
        return jnp.where(hit, 1.0, 0.0).astype(BF16)

    for e in range(E):
        a, lo = window(e, 0)
        first_copy(e).wait()
        pc = pos_ref[:, e:e + 1]
        c_s[...] = _dot(onehot(pc, a, lo), buf[e])
        s1 = st_ref[(i + 1) * E + e]
        nwin = lax.shift_right_logical(s1 - lo + (W - 1), 7)

        def extra(w, carry):
            a2, lo2 = window(e, w)
            cp = pltpu.make_async_copy(ye_ref.at[e, pl.ds(pl.multiple_of(a2, ROW_ALIGN), W)], buf2, sem2)
            cp.start()
            cp.wait()
            c_s[...] += _dot(onehot(pc, a2, lo2), buf2[...])
            return carry

        lax.fori_loop(1, nwin, extra, 0)
        o_ref[...] += gate_ref[:, e:e + 1] * c_s[...]


def _combine(starts, x, pos_t, gate_t, ye, cap, TT=256):
    n, d = x.shape
    E = N_EXPERTS
    TT = min(TT, n)
    grid_spec = pltpu.PrefetchScalarGridSpec(
        num_scalar_prefetch=1,
        grid=(n // TT,),
        in_specs=[pl.BlockSpec((TT, d), lambda i, st: (i, 0)),
                  pl.BlockSpec((TT, E), lambda i, st: (i, 0)),
                  pl.BlockSpec((TT, E), lambda i, st: (i, 0)),
                  pl.BlockSpec(memory_space=pl.ANY)],
        out_specs=pl.BlockSpec((TT, d), lambda i, st: (i, 0)),
        scratch_shapes=[pltpu.VMEM((E, COMBINE_WIN, d), BF16), pltpu.SemaphoreType.DMA((E,)),
                        pltpu.VMEM((COMBINE_WIN, d), BF16), pltpu.SemaphoreType.DMA(()),
                        pltpu.VMEM((TT, d), F32)],
    )
    return pl.pallas_call(
        functools.partial(_combine_kernel, TT=TT, cap=cap),
        grid_spec=grid_spec,
        out_shape=jax.ShapeDtypeStruct((n, d), F32),
        compiler_params=_cp(("arbitrary",)),
    )(starts, x, pos_t, gate_t, ye)


def _moe(x, g, router, wg, wu, wd):
    n, d = x.shape
    E = N_EXPERTS
    cap = EC_CAPACITY * n // E
    TT = min(256, n)
    h, aff = _router(x, g, router.T)
    pos, cs, gate = _select(aff, cap)
    idx = _compact(pos.reshape(-1), n, cap)
    ye = _experts(idx, h, wg, wu, wd, cap)
    starts = jnp.concatenate([jnp.zeros((E, 1), I32), cs[:, TT - 1::TT]], axis=1).T.reshape(-1)
    return _combine(starts, x, pos.T, gate.T, ye, cap, TT=TT)


def _rmsnorm_kernel(x_ref, g_ref, o_ref):
    xv = x_ref[...]
    o_ref[...] = xv * lax.rsqrt(jnp.mean(xv * xv, axis=-1, keepdims=True) + EPS) * g_ref[...]


def _rmsnorm(x, g, tm=512):
    n, d = x.shape
    return pl.pallas_call(
        _rmsnorm_kernel,
        grid=(n // tm,),
        in_specs=[pl.BlockSpec((tm, d), lambda i: (i, 0)), pl.BlockSpec((1, d), lambda i: (0, 0))],
        out_specs=pl.BlockSpec((tm, d), lambda i: (i, 0)),
        out_shape=jax.ShapeDtypeStruct((n, d), F32),
        compiler_params=_cp(("parallel",)),
    )(x, g.reshape(1, d))


def _even_mixer(x, B, L, p):
    proj = _rms_matmul(x, p["norm"], p["w_in"]).reshape(B, L, 5 * D_GROUP)
    hf = _lru(proj, None, p["conv_w"], p["conv_b"], p["wcat"][0], p["ba"][0], p["bx"][0], p["lam"][0],
              reverse=False)
    a_out = _lru(proj, hf, p["conv_w"], p["conv_b"], p["wcat"][1], p["ba"][1], p["bx"][1], p["lam"][1],
                 reverse=True)
    inv_freq = ROPE_THETA ** (-jnp.arange(0, DIFF_DH, 2, dtype=F32) / DIFF_DH)
    tabs = _rope_tables(L, DIFF_DH, inv_freq)
    half = DIFF_DH // 2
    qr = _rope(proj, 2, tabs, sh_a=LANES - half, sh_b=half, scale=DIFF_DH ** -0.5)
    kr = _rope(proj, 3, tabs, sh_a=LANES - half, sh_b=half, scale=1.0)
    vt = _v_transpose(proj, 4)
    b_out = _diff_attention(qr, kr, vt, p["diff_lam"], p["subln"], p["lam_init"])
    return _out_matmul(a_out.reshape(B * L, D_GROUP), b_out.reshape(B * L, D_GROUP), p["w_out"], x)


def _odd_mixer(x, B, L, p):
    proj = _rms_matmul(x, p["norm"], p["w_in"]).reshape(B, L, 7 * D_GROUP)
    c_out = _hyena(proj, p["conv_w"], p["conv_b"], p["w1"], p["b1"], p["w2"], p["b2"], p["w3"], p["freq"],
                   p["bias"])
    inv_freq = 1.0 / (10000.0 ** jnp.linspace(0.0, 1.0, RET_DH // 2, dtype=F32))
    tabs = _rope_tables(L, RET_DH, inv_freq)
    half = RET_DH // 2
    qr = _rope(proj, 3, tabs, sh_a=LANES - half, sh_b=half, scale=1.0)
    kr = _rope(proj, 4, tabs, sh_a=LANES - half, sh_b=half, scale=RET_DH ** -0.5)
    d_out = _retention(proj, qr, kr, p["rho"], 5 * N_HEAD_BLOCKS, 6 * N_HEAD_BLOCKS)
    return _out_matmul(c_out.reshape(B * L, D_GROUP), d_out.reshape(B * L, D_GROUP), p["w_out"], x)


def _trunk(x3, layers, final_norm):
    B, L, D = x3.shape
    x = x3.reshape(B * L, D)
    for kind, mp, ep in layers:
        x = (_even_mixer if kind == "even" else _odd_mixer)(x, B, L, mp)
        x = _moe(x, ep["norm"], ep["router"], ep["wg"], ep["wu"], ep["wd"])
    return _rmsnorm(x, final_norm).reshape(B, L, D)


def kernel(x_prompt, x_sample, ev_norm, ev_w_in, ev_conv_w, ev_conv_b, ev_rg_wa, ev_rg_ba, ev_rg_wx, ev_rg_bx, ev_rg_lam, ev_diff_lam, ev_subln, ev_w_out, od_norm, od_w_in, od_conv_w, od_conv_b, od_flt_w1, od_flt_b1, od_flt_w2, od_flt_b2, od_flt_w3, od_flt_freq, od_flt_bias, od_ret_rho, od_w_out, moe_norm, moe_router, moe_w_gate, moe_w_up, moe_w_down, final_norm):
    depth = moe_norm.shape[0]
    layers = []
    for layer in range(depth):
        j = layer // 2
        if layer % 2 == 0:
            mp = dict(norm=ev_norm[j], w_in=ev_w_in[j].astype(BF16), conv_w=ev_conv_w[j], conv_b=ev_conv_b[j],
                      wcat=jnp.concatenate([ev_rg_wa[j], ev_rg_wx[j]], axis=-1).astype(BF16),
                      ba=ev_rg_ba[j], bx=ev_rg_bx[j], lam=ev_rg_lam[j], diff_lam=ev_diff_lam[j],
                      subln=ev_subln[j], w_out=ev_w_out[j].astype(BF16),
                      lam_init=0.8 - 0.6 * math.exp(-0.3 * layer))
            kind = "even"
        else:
            mp = dict(norm=od_norm[j], w_in=od_w_in[j].astype(BF16), conv_w=od_conv_w[j], conv_b=od_conv_b[j],
                      w1=od_flt_w1[j], b1=od_flt_b1[j], w2=od_flt_w2[j], b2=od_flt_b2[j], w3=od_flt_w3[j],
                      freq=od_flt_freq[j], bias=od_flt_bias[j], rho=od_ret_rho[j],
                      w_out=od_w_out[j].astype(BF16))
            kind = "odd"
        ep = dict(norm=moe_norm[layer], router=moe_router[layer], wg=moe_w_gate[layer].astype(BF16),
                  wu=moe_w_up[layer].astype(BF16), wd=moe_w_down[layer].astype(BF16))
        layers.append((kind, mp, ep))
    return (_trunk(x_prompt, layers, final_norm), _trunk(x_sample, layers, final_norm))
```

```python
import functools
import math

import jax
import jax.numpy as jnp
import numpy as np
from jax import lax
from jax.experimental import pallas as pl
from jax.experimental.pallas import tpu as pltpu

F32 = jnp.float32
BF16 = jnp.bfloat16
I32 = jnp.int32

D_MODEL = 2048
D_GROUP = 1024
LANES = 128
N_HEAD_BLOCKS = D_GROUP // LANES
LRU_C = 8.0
DIFF_DH = 64
ROPE_THETA = 10000.0
RET_DH = 128
HY_EMB = 33
HY_FFN = 64
HY_FAST_DECAY = 0.3
HY_SLOW_DECAY = 1.5
HY_TARGET = 1e-2
N_EXPERTS = 16
EC_CAPACITY = 2
EPS = 1e-6
FFT_N2 = 128
HALO = 8
VMEM_LIMIT_MB = 56


def _cp(sem, vmem_mb=VMEM_LIMIT_MB):
    return pltpu.CompilerParams(dimension_semantics=sem, vmem_limit_bytes=vmem_mb * 1024 * 1024)


def _sigmoid(x):
    return 1.0 / (1.0 + jnp.exp(-x))


def _softplus(x):
    return jnp.maximum(x, 0.0) + jnp.log(1.0 + jnp.exp(-jnp.abs(x)))


def _dot(a, b):
    return jnp.dot(a, b, preferred_element_type=F32)


def _dot_nt(a, b):
    return lax.dot_general(a, b, (((1,), (1,)), ((), ())), preferred_element_type=F32)


def _rms_matmul_kernel(x_ref, g_ref, w_ref, o_ref, h_ref):
    @pl.when(pl.program_id(1) == 0)
    def _():
        xv = x_ref[...]
        ms = jnp.mean(xv * xv, axis=-1, keepdims=True)
        h_ref[...] = (xv * lax.rsqrt(ms + EPS) * g_ref[...]).astype(BF16)

    o_ref[...] = _dot(h_ref[...], w_ref[...])


def _rms_matmul(x, g, w_bf, tm=512, tn=1024):
    n, d = x.shape
    nout = w_bf.shape[1]
    return pl.pallas_call(
        _rms_matmul_kernel,
        grid=(n // tm, nout // tn),
        in_specs=[
            pl.BlockSpec((tm, d), lambda i, j: (i, 0)),
            pl.BlockSpec((1, d), lambda i, j: (0, 0)),
            pl.BlockSpec((d, tn), lambda i, j: (0, j)),
        ],
        out_specs=pl.BlockSpec((tm, tn), lambda i, j: (i, j)),
        out_shape=jax.ShapeDtypeStruct((n, nout), F32),
        scratch_shapes=[pltpu.VMEM((tm, d), BF16)],
        compiler_params=_cp(("parallel", "arbitrary")),
    )(x, g.reshape(1, d), w_bf)


def _out_matmul_kernel(a_ref, b_ref, wa_ref, wb_ref, x_ref, o_ref):
    o_ref[...] = x_ref[...] + _dot(a_ref[...], wa_ref[...]) + _dot(b_ref[...], wb_ref[...])


def _out_matmul(a, b, w_bf, x, tm=512, tn=1024):
    n, d = x.shape
    dg = a.shape[1]
    return pl.pallas_call(
        _out_matmul_kernel,
        grid=(n // tm, d // tn),
        in_specs=[
            pl.BlockSpec((tm, dg), lambda i, j: (i, 0)),
            pl.BlockSpec((tm, dg), lambda i, j: (i, 0)),
            pl.BlockSpec((dg, tn), lambda i, j: (0, j)),
            pl.BlockSpec((dg, tn), lambda i, j: (1, j)),
            pl.BlockSpec((tm, tn), lambda i, j: (i, j)),
        ],
        out_specs=pl.BlockSpec((tm, tn), lambda i, j: (i, j)),
        out_shape=jax.ShapeDtypeStruct((n, d), F32),
        compiler_params=_cp(("parallel", "parallel")),
    )(a, b, w_bf, w_bf, x)


def _lru_kernel(*refs, reverse, T, nt):
    if reverse:
        (x_ref, prev_ref, next_ref, cw_ref, cb_ref, w_ref, ba_ref, bx_ref, lam_ref,
         gate_ref, hf_ref, o_ref, a_s, b_s, carry, h_s) = refs
    else:
        (x_ref, prev_ref, next_ref, cw_ref, cb_ref, w_ref, ba_ref, bx_ref, lam_ref,
         o_ref, a_s, b_s, carry) = refs
    i = pl.program_id(1)
    ti = (nt - 1 - i) if reverse else i

    @pl.when(i == 0)
    def _():
        carry[...] = jnp.zeros_like(carry)

    x = x_ref[0]
    prev = jnp.where(ti == 0, 0.0, prev_ref[0])
    nxt = jnp.where(ti == nt - 1, 0.0, next_ref[0])
    xe = jnp.concatenate([prev, x, nxt], axis=0)
    cw = cw_ref[...]
    xc = (cb_ref[...] + cw[0:1] * xe[HALO - 2:HALO - 2 + T] + cw[1:2] * xe[HALO - 1:HALO - 1 + T]
          + cw[2:3] * xe[HALO:HALO + T] + cw[3:4] * xe[HALO + 1:HALO + 1 + T])
    sp = _softplus(-lam_ref[...])
    for blk in range(N_HEAD_BLOCKS):
        sl = slice(blk * LANES, (blk + 1) * LANES)
        xb = xc[:, sl]
        ri = _dot(xb.astype(BF16), w_ref[blk])
        r = _sigmoid(ri[:, :LANES] + ba_ref[:, sl])
        ig = _sigmoid(ri[:, LANES:] + bx_ref[:, sl])
        a = jnp.exp(-LRU_C * r * sp[:, sl])
        a_s[:, sl] = a
        b_s[:, sl] = jnp.sqrt(1.0 - a * a) * (ig * xb)

    dst = h_s if reverse else o_ref.at[0]

    def body(k, h):
        t = (T - 1 - k) if reverse else k
        h = a_s[pl.ds(t, 1), :] * h + b_s[pl.ds(t, 1), :]
        dst[pl.ds(t, 1), :] = h
        return h

    carry[...] = lax.fori_loop(0, T, body, carry[...], unroll=8)

    if reverse:
        g = gate_ref[0]
        gelu = 0.5 * g * (1.0 + jnp.tanh(math.sqrt(2.0 / math.pi) * (g + 0.044715 * g * g * g)))
        o_ref[0] = ((hf_ref[0] + h_s[...]) * gelu).astype(BF16)


def _lru(proj, hf, conv_w, conv_b, wcat, ba, bx, lam, *, reverse, T=512):
    B, L, _ = proj.shape
    T = min(T, L)
    nt = L // T
    hb = T // HALO
    nh = L // HALO

    def tile(i):
        return (nt - 1 - i) if reverse else i

    x_spec = pl.BlockSpec((1, T, D_GROUP), lambda b, i: (b, tile(i), 0))
    prev_spec = pl.BlockSpec((1, HALO, D_GROUP), lambda b, i: (b, jnp.maximum(tile(i) * hb - 1, 0), 0))
    next_spec = pl.BlockSpec((1, HALO, D_GROUP), lambda b, i: (b, jnp.minimum((tile(i) + 1) * hb, nh - 1), 0))
    vec = pl.BlockSpec((1, D_GROUP), lambda b, i: (0, 0))
    in_specs = [x_spec, prev_spec, next_spec,
                pl.BlockSpec((4, D_GROUP), lambda b, i: (0, 0)), vec,
                pl.BlockSpec((N_HEAD_BLOCKS, LANES, 2 * LANES), lambda b, i: (0, 0, 0)), vec, vec, vec]
    args = [proj, proj, proj, conv_w, conv_b.reshape(1, -1), wcat, ba.reshape(1, -1), bx.reshape(1, -1),
            lam.reshape(1, -1)]
    scratch = [pltpu.VMEM((T, D_GROUP), F32), pltpu.VMEM((T, D_GROUP), F32), pltpu.VMEM((1, D_GROUP), F32)]
    if reverse:
        in_specs += [pl.BlockSpec((1, T, D_GROUP), lambda b, i: (b, tile(i), 1)),
                     pl.BlockSpec((1, T, D_GROUP), lambda b, i: (b, tile(i), 0))]
        args += [proj, hf]
        scratch += [pltpu.VMEM((T, D_GROUP), F32)]
        out_dtype = BF16
    else:
        out_dtype = F32
    return pl.pallas_call(
        functools.partial(_lru_kernel, reverse=reverse, T=T, nt=nt),
        grid=(B, nt),
        in_specs=in_specs,
        out_specs=pl.BlockSpec((1, T, D_GROUP), lambda b, i: (b, tile(i), 0)),
        out_shape=jax.ShapeDtypeStruct((B, L, D_GROUP), out_dtype),
        scratch_shapes=scratch,
        compiler_params=_cp(("parallel", "arbitrary")),
    )(*args)


def _rope_kernel(x_ref, c_ref, sa_ref, sb_ref, o_ref, *, sh_a, sh_b, scale):
    c = c_ref[...]
    sa = sa_ref[...]
    sb = sb_ref[...]
    for h in range(N_HEAD_BLOCKS):
        sl = slice(h * LANES, (h + 1) * LANES)
        x = x_ref[0, :, sl]
        y = x * c + pltpu.roll(x, sh_a, 1) * sa + pltpu.roll(x, sh_b, 1) * sb
        o_ref[0, :, sl] = (y * scale).astype(o_ref.dtype)


def _rope(proj, col_block, tabs, *, sh_a, sh_b, scale, T=512):
    B, L, _ = proj.shape
    T = min(T, L)
    tab_spec = pl.BlockSpec((T, LANES), lambda b, i: (i, 0))
    return pl.pallas_call(
        functools.partial(_rope_kernel, sh_a=sh_a, sh_b=sh_b, scale=scale),
        grid=(B, L // T),
        in_specs=[pl.BlockSpec((1, T, D_GROUP), lambda b, i: (b, i, col_block)), tab_spec, tab_spec, tab_spec],
        out_specs=pl.BlockSpec((1, T, D_GROUP), lambda b, i: (b, i, 0)),
        out_shape=jax.ShapeDtypeStruct((B, L, D_GROUP), BF16),
        compiler_params=_cp(("parallel", "parallel")),
    )(proj, *tabs)


def _rope_tables(L, dh, inv_freq):
    lane = np.arange(LANES)
    d = lane % dh
    half = dh // 2
    ang = jnp.arange(L, dtype=F32)[:, None] * inv_freq[d % half][None, :]
    cos = jnp.cos(ang)
    sin = jnp.sin(ang)
    lo = jnp.asarray(d < half)[None, :]
    return cos, jnp.where(lo, -sin, 0.0), jnp.where(lo, 0.0, sin)


VT_ROWS = LANES + 16


def _vt_kernel(v_ref, o_ref):
    T = v_ref.shape[1]
    for h in range(N_HEAD_BLOCKS):
        sl = slice(h * LANES, (h + 1) * LANES)
        o_ref[0, h, 0:LANES, :] = v_ref[0, :, sl].T.astype(BF16)
        o_ref[0, h, LANES:VT_ROWS, :] = jnp.ones((VT_ROWS - LANES, T), BF16)


def _v_transpose(proj, col_block, T=512):
    B, L, _ = proj.shape
    T = min(T, L)
    return pl.pallas_call(
        _vt_kernel,
        grid=(B, L // T),
        in_specs=[pl.BlockSpec((1, T, D_GROUP), lambda b, i: (b, i, col_block))],
        out_specs=pl.BlockSpec((1, N_HEAD_BLOCKS, VT_ROWS, T), lambda b, i: (b, 0, 0, i)),
        out_shape=jax.ShapeDtypeStruct((B, N_HEAD_BLOCKS, VT_ROWS, L), BF16),
        compiler_params=_cp(("parallel", "parallel")),
    )(proj)


def _attn_kernel(q_ref, k_ref, vt_ref, lv_ref, sg_ref, o_ref, q2_s, m_s, acc_s, *, tq, tks, nsub, nk, lam_init):
    ki = pl.program_id(3)

    @pl.when(ki == 0)
    def _():
        q = q_ref[0]
        lane = lax.broadcasted_iota(I32, q.shape, 1)
        zero = jnp.zeros_like(q)
        q2_s[0:tq, :] = jnp.where(lane < DIFF_DH, q, zero)
        q2_s[tq:2 * tq, :] = jnp.where(lane >= DIFF_DH, q, zero)
        m_s[...] = jnp.full_like(m_s, -jnp.inf)
        acc_s[...] = jnp.zeros_like(acc_s)

    def scores(j):
        return _dot_nt(k_ref[0, j * tks:(j + 1) * tks, :], q2_s[...])

    s_next = scores(0)
    for j in range(nsub):
        ks = slice(j * tks, (j + 1) * tks)
        s = s_next
        if j + 1 < nsub:
            s_next = scores(j + 1)
        m_prev = m_s[...]
        m_new = jnp.maximum(m_prev, jnp.max(s, axis=0, keepdims=True))
        p = jnp.exp2(s - m_new).astype(BF16)
        acc_s[...] = jnp.exp2(m_prev - m_new) * acc_s[...] + _dot(vt_ref[0, 0, :, ks], p)
        m_s[...] = m_new

    @pl.when(ki == nk - 1)
    def _():
        o = acc_s[0:LANES, :] / acc_s[LANES:LANES + 1, :]
        lv = lv_ref[...]
        lam = (jnp.exp(jnp.sum(lv[0:1] * lv[1:2], axis=1, keepdims=True))
               - jnp.exp(jnp.sum(lv[2:3] * lv[3:4], axis=1, keepdims=True)) + lam_init)
        od = o[:, 0:tq] - lam * o[:, tq:2 * tq]
        ms = jnp.mean(od * od, axis=0, keepdims=True)
        y = od * lax.rsqrt(ms + EPS) * sg_ref[...] * (1.0 - lam_init)
        o_ref[0] = y.T.astype(BF16)


def _diff_attention(qr, kr, vt, diff_lam, subln, lam_init, tq=512, tk=8192, tks=1024):
    B, L, _ = qr.shape
    tq = min(tq, L)
    tk = min(tk, L)
    tks = min(tks, tk)
    nq, nk = L // tq, L // tk
    return pl.pallas_call(
        functools.partial(_attn_kernel, tq=tq, tks=tks, nsub=tk // tks, nk=nk, lam_init=lam_init),
        grid=(B, N_HEAD_BLOCKS, nq, nk),
        in_specs=[
            pl.BlockSpec((1, tq, LANES), lambda b, h, qi, ki: (b, qi, h)),
            pl.BlockSpec((1, tk, LANES), lambda b, h, qi, ki: (b, ki, h)),
            pl.BlockSpec((1, 1, VT_ROWS, tk), lambda b, h, qi, ki: (b, h, 0, ki)),
            pl.BlockSpec((4, DIFF_DH), lambda b, h, qi, ki: (0, 0)),
            pl.BlockSpec((LANES, 1), lambda b, h, qi, ki: (0, 0)),
        ],
        out_specs=pl.BlockSpec((1, tq, LANES), lambda b, h, qi, ki: (b, qi, h)),
        out_shape=jax.ShapeDtypeStruct((B, L, D_GROUP), BF16),
        scratch_shapes=[pltpu.VMEM((2 * tq, LANES), BF16), pltpu.VMEM((1, 2 * tq), F32),
                        pltpu.VMEM((VT_ROWS, 2 * tq), F32)],
        compiler_params=_cp(("parallel", "parallel", "parallel", "arbitrary")),
    )(qr, kr, vt, diff_lam, subln.reshape(LANES, 1))


def _log_gamma(rho_ref):
    return -_softplus(-rho_ref[0, 0])


def _ret_state_kernel(kf_ref, vf_ref, kb_ref, vb_ref, rf_ref, rb_ref, sf_ref, sb_ref, sf_s, sb_s, *, C):
    n = pl.program_id(2)

    @pl.when(n == 0)
    def _():
        sf_s[...] = jnp.zeros_like(sf_s)
        sb_s[...] = jnp.zeros_like(sb_s)

    sf_ref[0, 0, 0] = sf_s[...].astype(BF16)
    sb_ref[0, 0, 0] = sb_s[...].astype(BF16)
    lgf = _log_gamma(rf_ref)
    lgb = _log_gamma(rb_ref)
    row = lax.broadcasted_iota(I32, (C, LANES), 0).astype(F32)
    kdf = kf_ref[0].astype(F32) * jnp.exp(lgf * (C - 1.0 - row))
    kdb = kb_ref[0].astype(F32) * jnp.exp(lgb * row)
    sf_s[...] = sf_s[...] * jnp.exp(lgf * C) + _dot(kdf.T.astype(BF16), vf_ref[0].astype(BF16))
    sb_s[...] = sb_s[...] * jnp.exp(lgb * C) + _dot(kdb.T.astype(BF16), vb_ref[0].astype(BF16))


def _ret_main_kernel(q_ref, k_ref, v_ref, g_ref, sf_ref, sb_ref, rf_ref, rb_ref, o_ref, *, C):
    lgf = _log_gamma(rf_ref)
    lgb = _log_gamma(rb_ref)
    q = q_ref[0]
    s = _dot_nt(q, k_ref[0])
    r = lax.broadcasted_iota(I32, (C, C), 0)
    c = lax.broadcasted_iota(I32, (C, C), 1)
    dist = (r - c).astype(F32)
    dmat = jnp.exp(jnp.where(r >= c, lgf * dist, -lgb * dist))
    inner = _dot((s * dmat).astype(BF16), v_ref[0].astype(BF16))
    row = lax.broadcasted_iota(I32, (C, LANES), 0).astype(F32)
    qf = q.astype(F32)
    qcat = jnp.concatenate([qf * jnp.exp(lgf * (row + 1.0)), qf * jnp.exp(lgb * (C - row))], axis=1)
    scat = jnp.concatenate([sf_ref[0, 0, 0], sb_ref[0, 0, 0]], axis=0)
    y = inner + _dot(qcat.astype(BF16), scat)
    y = y * lax.rsqrt(jnp.mean(y * y, axis=-1, keepdims=True) + EPS)
    g = g_ref[0]
    o_ref[0] = (y * (g * _sigmoid(g))).astype(BF16)


def _retention(proj, qr, kr, rho, v_blk0, g_blk0, C=256):
    B, L, _ = proj.shape
    H = N_HEAD_BLOCKS
    C = min(C, L)
    N = L // C
    rho4 = rho.reshape(2, H, 1, 1)
    rf_spec = pl.BlockSpec((1, 1, 1, 1), lambda b, h, n: (0, h, 0, 0))
    rb_spec = pl.BlockSpec((1, 1, 1, 1), lambda b, h, n: (1, h, 0, 0))
    st_shape = jax.ShapeDtypeStruct((B, H, N, LANES, LANES), BF16)
    sf, sb = pl.pallas_call(
        functools.partial(_ret_state_kernel, C=C),
        grid=(B, H, N),
        in_specs=[
            pl.BlockSpec((1, C, LANES), lambda b, h, n: (b, n, h)),
            pl.BlockSpec((1, C, LANES), lambda b, h, n: (b, n, v_blk0 + h)),
            pl.BlockSpec((1, C, LANES), lambda b, h, n: (b, N - 1 - n, h)),
            pl.BlockSpec((1, C, LANES), lambda b, h, n: (b, N - 1 - n, v_blk0 + h)),
            rf_spec, rb_spec,
        ],
        out_specs=[pl.BlockSpec((1, 1, 1, LANES, LANES), lambda b, h, n: (b, h, n, 0, 0)),
                   pl.BlockSpec((1, 1, 1, LANES, LANES), lambda b, h, n: (b, h, N - 1 - n, 0, 0))],
        out_shape=[st_shape, st_shape],
        scratch_shapes=[pltpu.VMEM((LANES, LANES), F32), pltpu.VMEM((LANES, LANES), F32)],
        compiler_params=_cp(("parallel", "parallel", "arbitrary")),
    )(kr, proj, kr, proj, rho4, rho4)
    st_spec = pl.BlockSpec((1, 1, 1, LANES, LANES), lambda b, h, n: (b, h, n, 0, 0))
    return pl.pallas_call(
        functools.partial(_ret_main_kernel, C=C),
        grid=(B, H, N),
        in_specs=[
            pl.BlockSpec((1, C, LANES), lambda b, h, n: (b, n, h)),
            pl.BlockSpec((1, C, LANES), lambda b, h, n: (b, n, h)),
            pl.BlockSpec((1, C, LANES), lambda b, h, n: (b, n, v_blk0 + h)),
            pl.BlockSpec((1, C, LANES), lambda b, h, n: (b, n, g_blk0 + h)),
            st_spec, st_spec, rf_spec, rb_spec,
        ],
        out_specs=pl.BlockSpec((1, C, LANES), lambda b, h, n: (b, n, h)),
        out_shape=jax.ShapeDtypeStruct((B, L, D_GROUP), BF16),
        compiler_params=_cp(("parallel", "parallel", "parallel")),
    )(qr, kr, proj, proj, sf, sb, rho4, rho4)


def _dwconv3_kernel(x_ref, prev_ref, next_ref, w_ref, b_ref, o_ref, *, T, nt):
    i = pl.program_id(1)
    x = x_ref[0]
    prev = jnp.where(i == 0, 0.0, prev_ref[0])
    nxt = jnp.where(i == nt - 1, 0.0, next_ref[0])
    xe = jnp.concatenate([prev, x, nxt], axis=0)
    w = w_ref[...]
    o_ref[0] = (b_ref[...] + w[0:1] * xe[HALO - 1:HALO - 1 + T] + w[1:2] * xe[HALO:HALO + T]
                + w[2:3] * xe[HALO + 1:HALO + 1 + T])


def _dwconv3(proj, conv_w, conv_b, T=512):
    B, L, _ = proj.shape
    T = min(T, L)
    nt = L // T
    hb = T // HALO
    nh = L // HALO
    W = 3 * D_GROUP
    return pl.pallas_call(
        functools.partial(_dwconv3_kernel, T=T, nt=nt),
        grid=(B, nt),
        in_specs=[
            pl.BlockSpec((1, T, W), lambda b, i: (b, i, 0)),
            pl.BlockSpec((1, HALO, W), lambda b, i: (b, jnp.maximum(i * hb - 1, 0), 0)),
            pl.BlockSpec((1, HALO, W), lambda b, i: (b, jnp.minimum((i + 1) * hb, nh - 1), 0)),
            pl.BlockSpec((3, W), lambda b, i: (0, 0)),
            pl.BlockSpec((1, W), lambda b, i: (0, 0)),
        ],
        out_specs=pl.BlockSpec((1, T, W), lambda b, i: (b, i, 0)),
        out_shape=jax.ShapeDtypeStruct((B, L, W), F32),
        compiler_params=_cp(("parallel", "parallel")),
    )(proj, proj, proj, conv_w, conv_b.reshape(1, W))


def _filter_mlp_kernel(z_ref, w1_ref, b1_ref, w2_ref, b2_ref, w3_ref, fr_ref, hf_ref, sum_ref, *, TL, L):
    i = pl.program_id(0)
    fr = fr_ref[...]
    h = jnp.sin(fr * (_dot(z_ref[...].astype(BF16), w1_ref[...]) + b1_ref[...]))
    for j in range(2):
        h = jnp.sin(fr * (_dot(h.astype(BF16), w2_ref[j]) + b2_ref[j]))
    hf = _dot(h.astype(BF16), w3_ref[...])
    row = i * TL + lax.broadcasted_iota(I32, (TL, D_GROUP), 0)
    t = row.astype(F32) / (L - 1.0)
    ch = lax.broadcasted_iota(I32, (TL, D_GROUP), 1).astype(F32)
    max_decay = math.log(HY_TARGET) / HY_FAST_DECAY
    min_decay = math.log(HY_TARGET) / HY_SLOW_DECAY
    delta = jnp.abs(min_decay + (max_decay - min_decay) * ch / (D_GROUP - 1.0))
    dec = jnp.exp(-t * delta)
    keep = row < L - 1

    @pl.when(i == 0)
    def _():
        sum_ref[...] = jnp.zeros_like(sum_ref)

    for q in range(4):
        sl = slice(q * D_GROUP, (q + 1) * D_GROUP)
        v = hf[:, sl] * dec
        hf_ref[:, sl] = v
        a = jnp.abs(v)
        if q % 2 == 1:
            a = jnp.where(keep, a, 0.0)
        sum_ref[:, sl] += jnp.sum(a, axis=0, keepdims=True)


def _filter_mlp(L, w1, b1, w2, b2, w3, freq, TL=512):
    TL = min(TL, L)
    t = jnp.linspace(0.0, 1.0, L, dtype=F32)[:, None]
    bands = (HY_EMB - 1) // 2
    wpos = 2.0 * math.pi * jnp.arange(L, dtype=F32)[:, None] / L
    f = jnp.linspace(1e-4, bands - 1, bands, dtype=F32)[None, :]
    z = jnp.concatenate([t, jnp.cos(f * wpos), jnp.sin(f * wpos)], axis=-1)
    z = jnp.pad(z, ((0, 0), (0, LANES - HY_EMB)))
    pf = LANES - HY_FFN
    w1p = jnp.pad(w1, ((0, LANES - HY_EMB), (0, pf))).astype(BF16)
    b1p = jnp.pad(b1, (0, pf)).reshape(1, LANES)
    w2p = jnp.pad(w2, ((0, 0), (0, pf), (0, pf))).astype(BF16)
    b2p = jnp.pad(b2, ((0, 0), (0, pf))).reshape(2, 1, LANES)
    w3p = jnp.pad(w3, ((0, pf), (0, 0))).astype(BF16)
    frp = jnp.pad(freq, (0, pf)).reshape(1, LANES)
    W = 4 * D_GROUP
    full = lambda *shape: pl.BlockSpec(shape, lambda i: (0,) * len(shape))
    return pl.pallas_call(
        functools.partial(_filter_mlp_kernel, TL=TL, L=L),
        grid=(L // TL,),
        in_specs=[pl.BlockSpec((TL, LANES), lambda i: (i, 0)), full(LANES, LANES), full(1, LANES),
                  full(2, LANES, LANES), full(2, 1, LANES), full(LANES, W), full(1, LANES)],
        out_specs=[pl.BlockSpec((TL, W), lambda i: (i, 0)), full(1, W)],
        out_shape=[jax.ShapeDtypeStruct((L, W), F32), jax.ShapeDtypeStruct((1, W), F32)],
        compiler_params=_cp(("arbitrary",)),
    )(z, w1p, b1p, w2p, b2p, w3p, frp)


def _dft_tables(N):
    N2 = FFT_N2
    N1 = N // N2
    k1 = np.arange(N1)
    a1 = 2.0 * math.pi * ((k1[:, None] * k1[None, :]) % N1) / N1
    f1 = jnp.asarray(np.concatenate([np.cos(a1), -np.sin(a1)], axis=0), F32)
    f1inv = jnp.asarray(np.concatenate([np.cos(a1), -np.sin(a1)], axis=1)[:N1 // 2] / N, F32)
    k2 = np.arange(N2)
    a2 = 2.0 * math.pi * ((k2[:, None] * k2[None, :]) % N2) / N2
    fr, fi = np.cos(a2), -np.sin(a2)
    f2 = jnp.asarray(np.block([[fr, -fi], [fi, fr]]), F32).astype(BF16)
    f2inv = jnp.asarray(np.block([[fr, fi], [-fi, fr]]), F32).astype(BF16)
    at = 2.0 * math.pi * (k2[:, None] * k1[None, :]) / N
    tw = lambda a: jnp.broadcast_to(jnp.asarray(a, F32)[:, :, None], a.shape + (LANES,))
    return dict(N1=N1, N2=N2, f1=f1.astype(BF16), f1half=f1[:, :N1 // 2].astype(BF16), f1inv=f1inv.astype(BF16),
                f2=f2, f2inv=f2inv, twa_c=tw(np.cos(at)), twa_s=tw(-np.sin(at)),
                twb_c=tw(np.cos(at.T)), twb_s=tw(np.sin(at.T)))


def _dft1_kernel(x_ref, f_ref, c_ref, s_ref, o_ref, *, N1, C):
    a = _dot(f_ref[...], x_ref[0].astype(BF16))
    c = c_ref[0]
    s = s_ref[0]
    for j in range(C // LANES):
        sl = slice(j * LANES, (j + 1) * LANES)
        ar = a[0:N1, sl]
        ai = a[N1:2 * N1, sl]
        o_ref[0, 0:N1, sl] = (ar * c - ai * s).astype(BF16)
        o_ref[0, N1:2 * N1, sl] = (ar * s + ai * c).astype(BF16)


def _dft1(x2d, fmat, tab, C):
    Bt, R, _ = x2d.shape
    N1, N2 = tab["N1"], tab["N2"]
    return pl.pallas_call(
        functools.partial(_dft1_kernel, N1=N1, C=C),
        grid=(Bt, N2),
        in_specs=[pl.BlockSpec((1, R, C), lambda b, t: (b, 0, t)),
                  pl.BlockSpec((2 * N1, R), lambda b, t: (0, 0)),
                  pl.BlockSpec((1, N1, LANES), lambda b, t: (t, 0, 0)),
                  pl.BlockSpec((1, N1, LANES), lambda b, t: (t, 0, 0))],
        out_specs=pl.BlockSpec((1, 2 * N1, C), lambda b, t: (b, 0, t)),
        out_shape=jax.ShapeDtypeStruct((Bt, 2 * N1, N2 * C), BF16),
        compiler_params=_cp(("parallel", "parallel")),
    )(x2d, fmat, tab["twa_c"], tab["twa_s"])


def _filter_dft2_kernel(a_ref, f_ref, inv_ref, o_ref, *, N2):
    x = a_ref[:, 0].reshape(2 * N2, a_ref.shape[-1])
    z = _dot(f_ref[...], x) * inv_ref[...]
    o_ref[:, 0] = z.reshape(2, N2, z.shape[-1]).astype(BF16)


def _filter_dft2(a4, tab, inv_sum, Ct=1024):
    _, N1, N2, Ck = a4.shape
    blk = pl.BlockSpec((2, 1, N2, Ct), lambda k, c: (0, k, 0, c))
    return pl.pallas_call(
        functools.partial(_filter_dft2_kernel, N2=N2),
        grid=(N1, Ck // Ct),
        in_specs=[blk, pl.BlockSpec((2 * N2, 2 * N2), lambda k, c: (0, 0)),
                  pl.BlockSpec((1, Ct), lambda k, c: (0, c))],
        out_specs=blk,
        out_shape=jax.ShapeDtypeStruct(a4.shape, BF16),
        compiler_params=_cp(("parallel", "parallel")),
    )(a4, tab["f2"], inv_sum)


def _conv_mid_kernel(a_ref, kf_ref, f_ref, fi_ref, c_ref, s_ref, o_ref, *, N2):
    Ct = a_ref.shape[-1]
    x = a_ref[0, :, 0].reshape(2 * N2, Ct)
    z = _dot(f_ref[...], x)
    zr, zi = z[0:N2], z[N2:2 * N2]
    kr = kf_ref[0, 0].astype(F32)
    ki = kf_ref[1, 0].astype(F32)
    y = jnp.concatenate([zr * kr - zi * ki, zr * ki + zi * kr], axis=0).astype(BF16)
    b = _dot(fi_ref[...], y)
    c = c_ref[0]
    s = s_ref[0]
    for j in range(Ct // LANES):
        sl = slice(j * LANES, (j + 1) * LANES)
        br = b[0:N2, sl]
        bi = b[N2:2 * N2, sl]
        o_ref[0, 0, 0, :, sl] = (br * c - bi * s).astype(BF16)
        o_ref[0, 1, 0, :, sl] = (br * s + bi * c).astype(BF16)


def _conv_mid(a5, kf4, order, tab, Ct=1024):
    B, _, N1, N2, C = a5.shape
    nc = C // Ct
    sq = pl.BlockSpec((2 * N2, 2 * N2), lambda b, k, c: (0, 0))
    twb = pl.BlockSpec((1, N2, LANES), lambda b, k, c: (k, 0, 0))
    blk = pl.BlockSpec((1, 2, 1, N2, Ct), lambda b, k, c: (b, 0, k, 0, c))
    return pl.pallas_call(
        functools.partial(_conv_mid_kernel, N2=N2),
        grid=(B, N1, nc),
        in_specs=[blk, pl.BlockSpec((2, 1, N2, Ct), lambda b, k, c: (0, k, 0, order * nc + c)), sq, sq, twb, twb],
        out_specs=blk,
        out_shape=jax.ShapeDtypeStruct(a5.shape, BF16),
        compiler_params=_cp(("parallel", "parallel", "parallel")),
    )(a5, kf4, tab["f2"], tab["f2inv"], tab["twb_c"], tab["twb_s"])


def _conv_out_kernel(b_ref, f_ref, g_ref, z_ref, bias_ref, o_ref):
    y = _dot(f_ref[...], b_ref[0])
    o_ref[0] = (g_ref[0] * (y + z_ref[0] * bias_ref[...])).astype(o_ref.dtype)


def _conv_out(b3, tab, uc2d, g_blk, z2d, z_blk, bias, out_dtype):
    B, _, M = b3.shape
    N1, N2 = tab["N1"], tab["N2"]
    C = M // N2
    R = N1 // 2
    nb_g = uc2d.shape[-1] // (N2 * C)
    nb_z = z2d.shape[-1] // (N2 * C)
    return pl.pallas_call(
        _conv_out_kernel,
        grid=(B, N2),
        in_specs=[pl.BlockSpec((1, 2 * N1, C), lambda b, t: (b, 0, t)),
                  pl.BlockSpec((R, 2 * N1), lambda b, t: (0, 0)),
                  pl.BlockSpec((1, R, C), lambda b, t: (b, 0, t * nb_g + g_blk)),
                  pl.BlockSpec((1, R, C), lambda b, t: (b, 0, t * nb_z + z_blk)),
                  pl.BlockSpec((1, C), lambda b, t: (0, 0))],
        out_specs=pl.BlockSpec((1, R, C), lambda b, t: (b, 0, t)),
        out_shape=jax.ShapeDtypeStruct((B, R, M), out_dtype),
        compiler_params=_cp(("parallel", "parallel")),
    )(b3, tab["f1inv"], uc2d, z2d, bias.reshape(1, C))


def _hyena(proj, conv_w, conv_b, w1, b1, w2, b2, w3, freq, bias):
    B, L, _ = proj.shape
    C = D_GROUP
    N = 2 * L
    tab = _dft_tables(N)
    N1, N2 = tab["N1"], tab["N2"]
    R = N1 // 2
    uc = _dwconv3(proj, conv_w, conv_b)
    hf, sums = _filter_mlp(L, w1, b1, w2, b2, w3, freq)
    hf4 = hf.reshape(L, 2, 2, C)
    kfull = jnp.concatenate([hf4[:, :, 0], jnp.zeros((1, 2, C), F32), jnp.flip(hf4[:L - 1, :, 1], 0)], axis=0)
    s4 = sums.reshape(2, 2, C)
    inv_sum = (1.0 / (s4[:, 0] + s4[:, 1])).reshape(1, 2 * C)
    fa = _dft1(kfull.reshape(1, N1, N2 * 2 * C), tab["f1"], tab, 2 * C)
    kf4 = _filter_dft2(fa.reshape(2, N1, N2, 2 * C), tab, inv_sum)
    uc2d = uc.reshape(B, R, N2 * 3 * C)
    z2d, z_blk = uc2d, 0
    for o in range(2):
        if o == 0:
            a3 = _dft1_cols(uc2d, tab, C)
        else:
            a3 = _dft1(z2d, tab["f1half"], tab, C)
        b5 = _conv_mid(a3.reshape(B, 2, N1, N2, C), kf4, o, tab)
        z2d = _conv_out(b5.reshape(B, 2 * N1, N2 * C), tab, uc2d, 1 + o, z2d, z_blk, bias[o],
                        F32 if o == 0 else BF16)
        z_blk = 0
    return z2d.reshape(B, L, C)


def _dft1_cols(uc2d, tab, C):
    B, R, _ = uc2d.shape
    N1, N2 = tab["N1"], tab["N2"]
    return pl.pallas_call(
        functools.partial(_dft1_kernel, N1=N1, C=C),
        grid=(B, N2),
        in_specs=[pl.BlockSpec((1, R, C), lambda b, t: (b, 0, 3 * t)),
                  pl.BlockSpec((2 * N1, R), lambda b, t: (0, 0)),
                  pl.BlockSpec((1, N1, LANES), lambda b, t: (t, 0, 0)),
                  pl.BlockSpec((1, N1, LANES), lambda b, t: (t, 0, 0))],
        out_specs=pl.BlockSpec((1, 2 * N1, C), lambda b, t: (b, 0, t)),
        out_shape=jax.ShapeDtypeStruct((B, 2 * N1, N2 * C), BF16),
        compiler_params=_cp(("parallel", "parallel")),
    )(uc2d, tab["f1half"], tab["twa_c"], tab["twa_s"])


def _router_kernel(x_ref, g_ref, rt_ref, h_ref, aff_ref):
    d = x_ref.shape[1]
    xv = x_ref[...]
    ms = jnp.mean(xv * xv, axis=-1, keepdims=True)
    h = xv * lax.rsqrt(ms + EPS) * g_ref[...]
    h_ref[:, 0:d] = h
    logits = lax.dot_general(rt_ref[...], h, (((1,), (1,)), ((), ())), precision=lax.Precision.HIGHEST,
                             preferred_element_type=F32)
    row = lax.broadcasted_iota(I32, logits.shape, 0)
    logits = jnp.where(row < N_EXPERTS, logits, -jnp.inf)
    e = jnp.exp(logits - jnp.max(logits, axis=0, keepdims=True))
    aff = e / jnp.sum(e, axis=0, keepdims=True)
    aff_ref[...] = aff[0:N_EXPERTS]
    h_ref[:, d:d + LANES] = aff.T


def _router(x, g, router_t, tm=512):
    n, d = x.shape
    rt = jnp.pad(router_t, ((0, LANES - N_EXPERTS), (0, 0)))
    return pl.pallas_call(
        _router_kernel,
        grid=(n // tm,),
        in_specs=[pl.BlockSpec((tm, d), lambda i: (i, 0)), pl.BlockSpec((1, d), lambda i: (0, 0)),
                  pl.BlockSpec((LANES, d), lambda i: (0, 0))],
        out_specs=[pl.BlockSpec((tm, d + LANES), lambda i: (i, 0)), pl.BlockSpec((N_EXPERTS, tm), lambda i: (0, i))],
        out_shape=[jax.ShapeDtypeStruct((n, d + LANES), F32), jax.ShapeDtypeStruct((N_EXPERTS, n), F32)],
        compiler_params=_cp(("parallel",)),
    )(x, g.reshape(1, d), rt)


def _select_kernel(aff_ref, pos_ref, cs_ref, m_s, u_s, *, n, cap, CH):
    E = N_EXPERTS
    bits = pltpu.bitcast(aff_ref[...], I32)

    def search(it, thr):
        cand = thr | jnp.left_shift(jnp.int32(1), 30 - it)
        cnt = jnp.sum((bits >= cand).astype(F32), axis=1, keepdims=True)
        return jnp.where(cnt >= cap, cand, thr)

    thr = lax.fori_loop(0, 31, search, jnp.zeros((E, 1), I32))
    need = cap - jnp.sum((bits > thr).astype(F32), axis=1, keepdims=True)
    r = lax.broadcasted_iota(I32, (CH, CH), 0)
    c = lax.broadcasted_iota(I32, (CH, CH), 1)
    u_s[...] = (r <= c).astype(BF16)

    def chunk_bits(j):
        off = pl.multiple_of(j * CH, CH)
        return off, pltpu.bitcast(aff_ref[:, pl.ds(off, CH)], I32)

    def ties(j, carry):
        off, b = chunk_bits(j)
        eq = (b == thr).astype(F32)
        incl = _dot(eq.astype(BF16), u_s[...]) + carry
        sel = (b > thr) | ((b == thr) & (incl - eq < need))
        m_s[:, pl.ds(off, CH)] = sel.astype(F32)
        return incl[:, CH - 1:CH]

    lax.fori_loop(0, n // CH, ties, jnp.zeros((E, 1), F32))

    def slots(j, carry):
        off, _ = chunk_bits(j)
        m = m_s[:, pl.ds(off, CH)]
        incl = _dot(m.astype(BF16), u_s[...]) + carry
        cs_ref[:, pl.ds(off, CH)] = incl.astype(I32)
        pos_ref[:, pl.ds(off, CH)] = jnp.where(m > 0.0, incl - 1.0, -1.0).astype(I32)
        return incl[:, CH - 1:CH]

    lax.fori_loop(0, n // CH, slots, jnp.zeros((E, 1), F32))


def _select(aff, cap, CH=512):
    E, n = aff.shape
    CH = min(CH, n)
    full = pl.BlockSpec((E, n), lambda: (0, 0))
    return pl.pallas_call(
        functools.partial(_select_kernel, n=n, cap=cap, CH=CH),
        in_specs=[full],
        out_specs=[full, full, full],
        out_shape=[jax.ShapeDtypeStruct((E, n), I32), jax.ShapeDtypeStruct((E, n), I32),
                   jax.ShapeDtypeStruct((E, n), F32)],
        scratch_shapes=[pltpu.VMEM((CH, CH), BF16)],
        compiler_params=_cp(None),
    )(aff)


def _compact_kernel(m_ref, idx_ref, *, cap, RB):
    m = m_ref[0]
    nch = m.shape[0]
    r = lax.broadcasted_iota(I32, (LANES, LANES), 0)
    c = lax.broadcasted_iota(I32, (LANES, LANES), 1)
    local = _dot(m, (r <= c).astype(F32))
    tot = _dot(m, (r >= 0).astype(F32))
    rr = lax.broadcasted_iota(I32, (nch, nch), 0)
    cc = lax.broadcasted_iota(I32, (nch, nch), 1)
    cend = _dot((cc <= rr).astype(F32), tot)
    cend_row = cend.T[0:1, :]
    cstart_row = cend_row - tot.T[0:1, :]
    chunk_id = lax.broadcasted_iota(I32, (RB, nch), 1).astype(F32)

    def block(b, carry):
        j0 = pl.multiple_of(b * RB, RB)
        slot = (j0 + lax.broadcasted_iota(I32, (RB, 1), 0)).astype(F32)
        cstar = jnp.sum((slot >= cend_row).astype(F32), axis=1, keepdims=True)
        oh = chunk_id == cstar
        counts = _dot(jnp.where(oh, 1.0, 0.0), local)
        first = jnp.sum(jnp.where(oh, cstart_row, 0.0), axis=1, keepdims=True)
        within = jnp.sum((counts <= slot - first).astype(F32), axis=1, keepdims=True)
        tok = jnp.broadcast_to(LANES * cstar + within, (RB, LANES)).T
        idx_ref[0, :, pl.ds(j0, RB)] = tok[0:8].astype(I32)
        return carry

    lax.fori_loop(0, cap // RB, block, 0)


def _compact(mask, n, cap, RB=512):
    E = N_EXPERTS
    nch = n // LANES
    m3 = mask.reshape(E, nch, LANES)
    if nch < LANES:
        m3 = jnp.pad(m3, ((0, 0), (0, LANES - nch), (0, 0)))
        nch = LANES
    RB = min(RB, cap)
    idx8 = pl.pallas_call(
        functools.partial(_compact_kernel, cap=cap, RB=RB),
        grid=(E,),
        in_specs=[pl.BlockSpec((1, nch, LANES), lambda e: (e, 0, 0))],
        out_specs=pl.BlockSpec((1, 8, cap), lambda e: (e, 0, 0)),
        out_shape=jax.ShapeDtypeStruct((E, 8, cap), I32),
        compiler_params=_cp(("parallel",)),
    )(m3)
    return idx8[:, 0, :].reshape(-1)


def _expert_kernel(idx_ref, h_ref, wg_ref, wu_ref, wd_ref, o_ref, xbuf, xb, acc, sem, *, tm, nf, d):
    e = pl.program_id(0)
    r = pl.program_id(1)
    f = pl.program_id(2)

    @pl.when(f == 0)
    def _():
        def issue(j, carry):
            t = idx_ref[r * tm + j]
            pltpu.make_async_copy(h_ref.at[pl.ds(t, 1)], xbuf.at[pl.ds(j, 1)], sem).start()
            return carry

        lax.fori_loop(0, tm, issue, 0, unroll=8)
        pltpu.make_async_copy(h_ref.at[pl.ds(0, tm)], xbuf, sem).wait()
        xb[...] = xbuf[:, 0:d].astype(BF16)
        acc[...] = jnp.zeros_like(acc)

    x = xb[...]
    g = _dot(x, wg_ref[0])
    u = _dot(x, wu_ref[0])
    hid = (g * _sigmoid(g) * u).astype(BF16)
    acc[...] += _dot(hid, wd_ref[0])

    @pl.when(f == nf - 1)
    def _():
        aff = xbuf[:, d:d + LANES]
        lane = lax.broadcasted_iota(I32, aff.shape, 1)
        gate = jnp.sum(jnp.where(lane == e, aff, 0.0), axis=1, keepdims=True)
        o_ref[0] = (acc[...] * gate).astype(BF16)


def _experts(idx_flat, h, wg, wu, wd, cap, tm=1024, tf=512):
    n, dx = h.shape
    E, d, dff = wg.shape
    tm = min(tm, cap)
    nf = dff // tf
    return pl.pallas_call(
        functools.partial(_expert_kernel, tm=tm, nf=nf, d=d),
        grid=(E, cap // tm, nf),
        in_specs=[pl.BlockSpec((cap,), lambda e, r, f: (e,), memory_space=pltpu.SMEM),
                  pl.BlockSpec(memory_space=pl.ANY),
                  pl.BlockSpec((1, d, tf), lambda e, r, f: (e, 0, f)),
                  pl.BlockSpec((1, d, tf), lambda e, r, f: (e, 0, f)),
                  pl.BlockSpec((1, tf, d), lambda e, r, f: (e, f, 0))],
        out_specs=pl.BlockSpec((1, tm, d), lambda e, r, f: (e, r, 0)),
        out_shape=jax.ShapeDtypeStruct((E, cap, d), BF16),
        scratch_shapes=[pltpu.VMEM((tm, dx), F32), pltpu.VMEM((tm, d), BF16), pltpu.VMEM((tm, d), F32),
                        pltpu.SemaphoreType.DMA(())],
        compiler_params=_cp(("parallel", "arbitrary", "arbitrary")),
    )(idx_flat, h, wg, wu, wd)


COMBINE_WIN = 128
ROW_ALIGN = 16


def _combine_kernel(st_ref, x_ref, pos_ref, ye_ref, o_ref, buf, sems, buf2, sem2, p_s, *, TT, cap, nt):
    i = pl.program_id(0)
    E = N_EXPERTS
    W = COMBINE_WIN

    def window(tile, e, w):
        s0 = st_ref[tile * E + e]
        lo = lax.shift_left(lax.shift_right_logical(s0, 4), 4) + W * w
        return jnp.minimum(lo, cap - W), lo

    def first_copy(tile, slot, e):
        a, _ = window(tile, e, 0)
        return pltpu.make_async_copy(ye_ref.at[e, pl.ds(pl.multiple_of(a, ROW_ALIGN), W)],
                                     buf.at[slot, pl.ds(e * W, W)], sems.at[slot, e])

    slot = lax.rem(i, 2)

    @pl.when(i == 0)
    def _():
        for e in range(E):
            first_copy(0, 0, e).start()

    @pl.when(i + 1 < nt)
    def _():
        for e in range(E):
            first_copy(i + 1, 1 - slot, e).start()

    lane = lax.broadcasted_iota(I32, (TT, W), 1)

    def onehot(pc, a, lo):
        hit = (pc - a == lane) & (pc >= lo) & (pc < lo + W)
        return jnp.where(hit, 1.0, 0.0).astype(BF16)

    for e in range(E):
        a, lo = window(i, e, 0)
        p_s[:, e * W:(e + 1) * W] = onehot(pos_ref[:, e:e + 1], a, lo)
    for e in range(E):
        first_copy(i, slot, e).wait()
    o_ref[...] = x_ref[...] + _dot(p_s[...], buf[slot])

    for e in range(E):
        _, lo = window(i, e, 0)
        s1 = st_ref[(i + 1) * E + e]
        nwin = lax.shift_right_logical(s1 - lo + (W - 1), 7)

        def extra(w, carry):
            a2, lo2 = window(i, e, w)
            cp = pltpu.make_async_copy(ye_ref.at[e, pl.ds(pl.multiple_of(a2, ROW_ALIGN), W)], buf2, sem2)
            cp.start()
            cp.wait()
            o_ref[...] += _dot(onehot(pos_ref[:, e:e + 1], a2, lo2), buf2[...])
            return carry

        lax.fori_loop(1, nwin, extra, 0)


def _combine(starts, x, pos_t, ye, cap, TT=256):
    n, d = x.shape
    E = N_EXPERTS
    TT = min(TT, n)
    nt = n // TT
    grid_spec = pltpu.PrefetchScalarGridSpec(
        num_scalar_prefetch=1,
        grid=(nt,),
        in_specs=[pl.BlockSpec((TT, d), lambda i, st: (i, 0)),
                  pl.BlockSpec((TT, E), lambda i, st: (i, 0)),
                  pl.BlockSpec(memory_space=pl.ANY)],
        out_specs=pl.BlockSpec((TT, d), lambda i, st: (i, 0)),
        scratch_shapes=[pltpu.VMEM((2, E * COMBINE_WIN, d), BF16), pltpu.SemaphoreType.DMA((2, E)),
                        pltpu.VMEM((COMBINE_WIN, d), BF16), pltpu.SemaphoreType.DMA(()),
                        pltpu.VMEM((TT, E * COMBINE_WIN), BF16)],
    )
    return pl.pallas_call(
        functools.partial(_combine_kernel, TT=TT, cap=cap, nt=nt),
        grid_spec=grid_spec,
        out_shape=jax.ShapeDtypeStruct((n, d), F32),
        compiler_params=_cp(("arbitrary",)),
    )(starts, x, pos_t, ye)


def _moe(x, g, router, wg, wu, wd):
    n, d = x.shape
    E = N_EXPERTS
    cap = EC_CAPACITY * n // E
    TT = min(256, n)
    h, aff = _router(x, g, router.T)
    pos, cs, mask = _select(aff, cap)
    idx = _compact(mask, n, cap)
    ye = _experts(idx, h, wg, wu, wd, cap)
    starts = jnp.concatenate([jnp.zeros((E, 1), I32), cs[:, TT - 1::TT]], axis=1).T.reshape(-1)
    return _combine(starts, x, pos.T, ye, cap, TT=TT)


def _rmsnorm_kernel(x_ref, g_ref, o_ref):
    xv = x_ref[...]
    o_ref[...] = xv * lax.rsqrt(jnp.mean(xv * xv, axis=-1, keepdims=True) + EPS) * g_ref[...]


def _rmsnorm(x, g, tm=512):
    n, d = x.shape
    return pl.pallas_call(
        _rmsnorm_kernel,
        grid=(n // tm,),
        in_specs=[pl.BlockSpec((tm, d), lambda i: (i, 0)), pl.BlockSpec((1, d), lambda i: (0, 0))],
        out_specs=pl.BlockSpec((tm, d), lambda i: (i, 0)),
        out_shape=jax.ShapeDtypeStruct((n, d), F32),
        compiler_params=_cp(("parallel",)),
    )(x, g.reshape(1, d))


def _even_mixer(x, B, L, p):
    proj = _rms_matmul(x, p["norm"], p["w_in"]).reshape(B, L, 5 * D_GROUP)
    hf = _lru(proj, None, p["conv_w"], p["conv_b"], p["wcat"][0], p["ba"][0], p["bx"][0], p["lam"][0],
              reverse=False)
    a_out = _lru(proj, hf, p["conv_w"], p["conv_b"], p["wcat"][1], p["ba"][1], p["bx"][1], p["lam"][1],
                 reverse=True)
    inv_freq = ROPE_THETA ** (-jnp.arange(0, DIFF_DH, 2, dtype=F32) / DIFF_DH)
    tabs = _rope_tables(L, DIFF_DH, inv_freq)
    half = DIFF_DH // 2
    qr = _rope(proj, 2, tabs, sh_a=LANES - half, sh_b=half, scale=DIFF_DH ** -0.5 * math.log2(math.e))
    kr = _rope(proj, 3, tabs, sh_a=LANES - half, sh_b=half, scale=1.0)
    vt = _v_transpose(proj, 4)
    b_out = _diff_attention(qr, kr, vt, p["diff_lam"], p["subln"], p["lam_init"])
    return _out_matmul(a_out.reshape(B * L, D_GROUP), b_out.reshape(B * L, D_GROUP), p["w_out"], x)


def _odd_mixer(x, B, L, p):
    proj = _rms_matmul(x, p["norm"], p["w_in"]).reshape(B, L, 7 * D_GROUP)
    c_out = _hyena(proj, p["conv_w"], p["conv_b"], p["w1"], p["b1"], p["w2"], p["b2"], p["w3"], p["freq"],
                   p["bias"])
    inv_freq = 1.0 / (10000.0 ** jnp.linspace(0.0, 1.0, RET_DH // 2, dtype=F32))
    tabs = _rope_tables(L, RET_DH, inv_freq)
    half = RET_DH // 2
    qr = _rope(proj, 3, tabs, sh_a=LANES - half, sh_b=half, scale=1.0)
    kr = _rope(proj, 4, tabs, sh_a=LANES - half, sh_b=half, scale=RET_DH ** -0.5)
    d_out = _retention(proj, qr, kr, p["rho"], 5 * N_HEAD_BLOCKS, 6 * N_HEAD_BLOCKS)
    return _out_matmul(c_out.reshape(B * L, D_GROUP), d_out.reshape(B * L, D_GROUP), p["w_out"], x)


def _trunk(x3, layers, final_norm):
    B, L, D = x3.shape
    x = x3.reshape(B * L, D)
    for kind, mp, ep in layers:
        x = (_even_mixer if kind == "even" else _odd_mixer)(x, B, L, mp)
        x = _moe(x, ep["norm"], ep["router"], ep["wg"], ep["wu"], ep["wd"])
    return _rmsnorm(x, final_norm).reshape(B, L, D)


def kernel(x_prompt, x_sample, ev_norm, ev_w_in, ev_conv_w, ev_conv_b, ev_rg_wa, ev_rg_ba, ev_rg_wx, ev_rg_bx, ev_rg_lam, ev_diff_lam, ev_subln, ev_w_out, od_norm, od_w_in, od_conv_w, od_conv_b, od_flt_w1, od_flt_b1, od_flt_w2, od_flt_b2, od_flt_w3, od_flt_freq, od_flt_bias, od_ret_rho, od_w_out, moe_norm, moe_router, moe_w_gate, moe_w_up, moe_w_down, final_norm):
    depth = moe_norm.shape[0]
    layers = []
    for layer in range(depth):
        j = layer // 2
        if layer % 2 == 0:
            mp = dict(norm=ev_norm[j], w_in=ev_w_in[j].astype(BF16), conv_w=ev_conv_w[j], conv_b=ev_conv_b[j],
                      wcat=jnp.concatenate([ev_rg_wa[j], ev_rg_wx[j]], axis=-1).astype(BF16),
                      ba=ev_rg_ba[j], bx=ev_rg_bx[j], lam=ev_rg_lam[j], diff_lam=ev_diff_lam[j],
                      subln=ev_subln[j], w_out=ev_w_out[j].astype(BF16),
                      lam_init=0.8 - 0.6 * math.exp(-0.3 * layer))
            kind = "even"
        else:
            mp = dict(norm=od_norm[j], w_in=od_w_in[j].astype(BF16), conv_w=od_conv_w[j], conv_b=od_conv_b[j],
                      w1=od_flt_w1[j], b1=od_flt_b1[j], w2=od_flt_w2[j], b2=od_flt_b2[j], w3=od_flt_w3[j],
                      freq=od_flt_freq[j], bias=od_flt_bias[j], rho=od_ret_rho[j],
                      w_out=od_w_out[j].astype(BF16))
            kind = "odd"
        ep = dict(norm=moe_norm[layer], router=moe_router[layer], wg=moe_w_gate[layer].astype(BF16),
                  wu=moe_w_up[layer].astype(BF16), wd=moe_w_down[layer].astype(BF16))
        layers.append((kind, mp, ep))
    return (_trunk(x_prompt, layers, final_norm), _trunk(x_sample, layers, final_norm))
```

```python
import functools
import math

import jax
import jax.numpy as jnp
import numpy as np
from jax import lax
from jax.experimental import pallas as pl
from jax.experimental.pallas import tpu as pltpu

F32 = jnp.float32
BF16 = jnp.bfloat16
I32 = jnp.int32

D_MODEL = 2048
D_GROUP = 1024
LANES = 128
N_HEAD_BLOCKS = D_GROUP // LANES
LRU_C = 8.0
DIFF_DH = 64
ROPE_THETA = 10000.0
RET_DH = 128
HY_EMB = 33
HY_FFN = 64
HY_FAST_DECAY = 0.3
HY_SLOW_DECAY = 1.5
HY_TARGET = 1e-2
N_EXPERTS = 16
EC_CAPACITY = 2
EPS = 1e-6
FFT_N2 = 128
HALO = 8
VMEM_LIMIT_MB = 56


def _cp(sem, vmem_mb=VMEM_LIMIT_MB):
    return pltpu.CompilerParams(dimension_semantics=sem, vmem_limit_bytes=vmem_mb * 1024 * 1024)


def _sigmoid(x):
    return 1.0 / (1.0 + jnp.exp(-x))


def _softplus(x):
    return jnp.maximum(x, 0.0) + jnp.log(1.0 + jnp.exp(-jnp.abs(x)))


def _dot(a, b):
    return jnp.dot(a, b, preferred_element_type=F32)


def _dot_nt(a, b):
    return lax.dot_general(a, b, (((1,), (1,)), ((), ())), preferred_element_type=F32)


def _rms_matmul_kernel(x_ref, g_ref, w_ref, o_ref, h_ref):
    @pl.when(pl.program_id(1) == 0)
    def _():
        xv = x_ref[...]
        ms = jnp.mean(xv * xv, axis=-1, keepdims=True)
        h_ref[...] = (xv * lax.rsqrt(ms + EPS) * g_ref[...]).astype(BF16)

    o_ref[...] = _dot(h_ref[...], w_ref[...])


def _rms_matmul(x, g, w_bf, tm=1024, tn=1024):
    n, d = x.shape
    tm = min(tm, n)
    nout = w_bf.shape[1]
    return pl.pallas_call(
        _rms_matmul_kernel,
        grid=(n // tm, nout // tn),
        in_specs=[
            pl.BlockSpec((tm, d), lambda i, j: (i, 0)),
            pl.BlockSpec((1, d), lambda i, j: (0, 0)),
            pl.BlockSpec((d, tn), lambda i, j: (0, j)),
        ],
        out_specs=pl.BlockSpec((tm, tn), lambda i, j: (i, j)),
        out_shape=jax.ShapeDtypeStruct((n, nout), F32),
        scratch_shapes=[pltpu.VMEM((tm, d), BF16)],
        compiler_params=_cp(("parallel", "arbitrary")),
    )(x, g.reshape(1, d), w_bf)


def _out_matmul_kernel(a_ref, b_ref, wa_ref, wb_ref, x_ref, o_ref):
    o_ref[...] = (x_ref[...] + _dot(a_ref[...].astype(BF16), wa_ref[...])
                  + _dot(b_ref[...].astype(BF16), wb_ref[...]))


def _out_matmul(a, b, w_bf, x, tm=1024, tn=1024):
    n, d = x.shape
    tm = min(tm, n)
    dg = a.shape[1]
    return pl.pallas_call(
        _out_matmul_kernel,
        grid=(n // tm, d // tn),
        in_specs=[
            pl.BlockSpec((tm, dg), lambda i, j: (i, 0)),
            pl.BlockSpec((tm, dg), lambda i, j: (i, 0)),
            pl.BlockSpec((dg, tn), lambda i, j: (0, j)),
            pl.BlockSpec((dg, tn), lambda i, j: (1, j)),
            pl.BlockSpec((tm, tn), lambda i, j: (i, j)),
        ],
        out_specs=pl.BlockSpec((tm, tn), lambda i, j: (i, j)),
        out_shape=jax.ShapeDtypeStruct((n, d), F32),
        compiler_params=_cp(("parallel", "parallel")),
    )(a, b, w_bf, w_bf, x)


def _lru_kernel(*refs, reverse, T, nt):
    if reverse:
        (x_ref, prev_ref, next_ref, cw_ref, cb_ref, w_ref, ba_ref, bx_ref, lam_ref,
         gate_ref, hf_ref, o_ref, a_s, b_s, carry, h_s) = refs
    else:
        (x_ref, prev_ref, next_ref, cw_ref, cb_ref, w_ref, ba_ref, bx_ref, lam_ref,
         o_ref, a_s, b_s, carry) = refs
    i = pl.program_id(1)
    ti = (nt - 1 - i) if reverse else i

    @pl.when(i == 0)
    def _():
        carry[...] = jnp.zeros_like(carry)

    x = x_ref[0]
    prev = jnp.where(ti == 0, 0.0, prev_ref[0])
    nxt = jnp.where(ti == nt - 1, 0.0, next_ref[0])
    xe = jnp.concatenate([prev, x, nxt], axis=0)
    cw = cw_ref[...]
    xc = (cb_ref[...] + cw[0:1] * xe[HALO - 2:HALO - 2 + T] + cw[1:2] * xe[HALO - 1:HALO - 1 + T]
          + cw[2:3] * xe[HALO:HALO + T] + cw[3:4] * xe[HALO + 1:HALO + 1 + T])
    sp = _softplus(-lam_ref[...])
    for blk in range(N_HEAD_BLOCKS):
        sl = slice(blk * LANES, (blk + 1) * LANES)
        xb = xc[:, sl]
        ri = _dot(xb.astype(BF16), w_ref[blk])
        r = _sigmoid(ri[:, :LANES] + ba_ref[:, sl])
        ig = _sigmoid(ri[:, LANES:] + bx_ref[:, sl])
        a = jnp.exp(-LRU_C * r * sp[:, sl])
        a_s[:, sl] = a
        b_s[:, sl] = jnp.sqrt(1.0 - a * a) * (ig * xb)

    dst = h_s if reverse else o_ref.at[0]

    def body(k, h):
        t = (T - 1 - k) if reverse else k
        h = a_s[pl.ds(t, 1), :] * h + b_s[pl.ds(t, 1), :]
        dst[pl.ds(t, 1), :] = h
        return h

    carry[...] = lax.fori_loop(0, T, body, carry[...], unroll=8)

    if reverse:
        g = gate_ref[0]
        gelu = 0.5 * g * (1.0 + jnp.tanh(math.sqrt(2.0 / math.pi) * (g + 0.044715 * g * g * g)))
        o_ref[0] = ((hf_ref[0] + h_s[...]) * gelu).astype(BF16)


def _lru(proj, hf, conv_w, conv_b, wcat, ba, bx, lam, *, reverse, T=512):
    B, L, _ = proj.shape
    T = min(T, L)
    nt = L // T
    hb = T // HALO
    nh = L // HALO

    def tile(i):
        return (nt - 1 - i) if reverse else i

    x_spec = pl.BlockSpec((1, T, D_GROUP), lambda b, i: (b, tile(i), 0))
    prev_spec = pl.BlockSpec((1, HALO, D_GROUP), lambda b, i: (b, jnp.maximum(tile(i) * hb - 1, 0), 0))
    next_spec = pl.BlockSpec((1, HALO, D_GROUP), lambda b, i: (b, jnp.minimum((tile(i) + 1) * hb, nh - 1), 0))
    vec = pl.BlockSpec((1, D_GROUP), lambda b, i: (0, 0))
    in_specs = [x_spec, prev_spec, next_spec,
                pl.BlockSpec((4, D_GROUP), lambda b, i: (0, 0)), vec,
                pl.BlockSpec((N_HEAD_BLOCKS, LANES, 2 * LANES), lambda b, i: (0, 0, 0)), vec, vec, vec]
    args = [proj, proj, proj, conv_w, conv_b.reshape(1, -1), wcat, ba.reshape(1, -1), bx.reshape(1, -1),
            lam.reshape(1, -1)]
    scratch = [pltpu.VMEM((T, D_GROUP), F32), pltpu.VMEM((T, D_GROUP), F32), pltpu.VMEM((1, D_GROUP), F32)]
    if reverse:
        in_specs += [pl.BlockSpec((1, T, D_GROUP), lambda b, i: (b, tile(i), 1)),
                     pl.BlockSpec((1, T, D_GROUP), lambda b, i: (b, tile(i), 0))]
        args += [proj, hf]
        scratch += [pltpu.VMEM((T, D_GROUP), F32)]
        out_dtype = BF16
    else:
        out_dtype = F32
    return pl.pallas_call(
        functools.partial(_lru_kernel, reverse=reverse, T=T, nt=nt),
        grid=(B, nt),
        in_specs=in_specs,
        out_specs=pl.BlockSpec((1, T, D_GROUP), lambda b, i: (b, tile(i), 0)),
        out_shape=jax.ShapeDtypeStruct((B, L, D_GROUP), out_dtype),
        scratch_shapes=scratch,
        compiler_params=_cp(("parallel", "arbitrary")),
    )(*args)


def _rope_kernel(x_ref, c_ref, sa_ref, sb_ref, o_ref, *, sh_a, sh_b, scale):
    c = c_ref[...]
    sa = sa_ref[...]
    sb = sb_ref[...]
    for h in range(N_HEAD_BLOCKS):
        sl = slice(h * LANES, (h + 1) * LANES)
        x = x_ref[0, :, sl]
        y = x * c + pltpu.roll(x, sh_a, 1) * sa + pltpu.roll(x, sh_b, 1) * sb
        o_ref[0, :, sl] = (y * scale).astype(o_ref.dtype)


def _rope(proj, col_block, tabs, *, sh_a, sh_b, scale, T=512):
    B, L, _ = proj.shape
    T = min(T, L)
    tab_spec = pl.BlockSpec((T, LANES), lambda b, i: (i, 0))
    return pl.pallas_call(
        functools.partial(_rope_kernel, sh_a=sh_a, sh_b=sh_b, scale=scale),
        grid=(B, L // T),
        in_specs=[pl.BlockSpec((1, T, D_GROUP), lambda b, i: (b, i, col_block)), tab_spec, tab_spec, tab_spec],
        out_specs=pl.BlockSpec((1, T, D_GROUP), lambda b, i: (b, i, 0)),
        out_shape=jax.ShapeDtypeStruct((B, L, D_GROUP), BF16),
        compiler_params=_cp(("parallel", "parallel")),
    )(proj, *tabs)


def _rope_tables(L, dh, inv_freq):
    lane = np.arange(LANES)
    d = lane % dh
    half = dh // 2
    ang = jnp.arange(L, dtype=F32)[:, None] * inv_freq[d % half][None, :]
    cos = jnp.cos(ang)
    sin = jnp.sin(ang)
    lo = jnp.asarray(d < half)[None, :]
    return cos, jnp.where(lo, -sin, 0.0), jnp.where(lo, 0.0, sin)


VT_ROWS = LANES + 16


def _vt_kernel(v_ref, o_ref):
    T = v_ref.shape[1]
    for h in range(N_HEAD_BLOCKS):
        sl = slice(h * LANES, (h + 1) * LANES)
        o_ref[0, h, 0:LANES, :] = v_ref[0, :, sl].T.astype(BF16)
        o_ref[0, h, LANES:VT_ROWS, :] = jnp.ones((VT_ROWS - LANES, T), BF16)


def _v_transpose(proj, col_block, T=512):
    B, L, _ = proj.shape
    T = min(T, L)
    return pl.pallas_call(
        _vt_kernel,
        grid=(B, L // T),
        in_specs=[pl.BlockSpec((1, T, D_GROUP), lambda b, i: (b, i, col_block))],
        out_specs=pl.BlockSpec((1, N_HEAD_BLOCKS, VT_ROWS, T), lambda b, i: (b, 0, 0, i)),
        out_shape=jax.ShapeDtypeStruct((B, N_HEAD_BLOCKS, VT_ROWS, L), BF16),
        compiler_params=_cp(("parallel", "parallel")),
    )(proj)


def _attn_kernel(q_ref, k_ref, vt_ref, lv_ref, sg_ref, o_ref, q2_s, m_s, acc_s, *, tq, tks, nsub, nk, lam_init):
    ki = pl.program_id(3)

    @pl.when(ki == 0)
    def _():
        q = q_ref[0]
        lane = lax.broadcasted_iota(I32, q.shape, 1)
        zero = jnp.zeros_like(q)
        q2_s[0:tq, :] = jnp.where(lane < DIFF_DH, q, zero)
        q2_s[tq:2 * tq, :] = jnp.where(lane >= DIFF_DH, q, zero)
        m_s[...] = jnp.full_like(m_s, -jnp.inf)
        acc_s[...] = jnp.zeros_like(acc_s)

    def scores(j):
        return _dot_nt(k_ref[0, j * tks:(j + 1) * tks, :], q2_s[...])

    s_next = scores(0)
    for j in range(nsub):
        ks = slice(j * tks, (j + 1) * tks)
        s = s_next
        if j + 1 < nsub:
            s_next = scores(j + 1)
        m_prev = m_s[...]
        m_new = jnp.maximum(m_prev, jnp.max(s, axis=0, keepdims=True))
        p = jnp.exp2(s - m_new).astype(BF16)
        acc_s[...] = jnp.exp2(m_prev - m_new) * acc_s[...] + _dot(vt_ref[0, 0, :, ks], p)
        m_s[...] = m_new

    @pl.when(ki == nk - 1)
    def _():
        o = acc_s[0:LANES, :] / acc_s[LANES:LANES + 1, :]
        lv = lv_ref[...]
        lam = (jnp.exp(jnp.sum(lv[0:1] * lv[1:2], axis=1, keepdims=True))
               - jnp.exp(jnp.sum(lv[2:3] * lv[3:4], axis=1, keepdims=True)) + lam_init)
        od = o[:, 0:tq] - lam * o[:, tq:2 * tq]
        ms = jnp.mean(od * od, axis=0, keepdims=True)
        y = od * lax.rsqrt(ms + EPS) * sg_ref[...] * (1.0 - lam_init)
        o_ref[0] = y.T.astype(BF16)


def _diff_attention(qr, kr, vt, diff_lam, subln, lam_init, tq=512, tk=8192, tks=1024):
    B, L, _ = qr.shape
    tq = min(tq, L)
    tk = min(tk, L)
    tks = min(tks, tk)
    nq, nk = L // tq, L // tk
    return pl.pallas_call(
        functools.partial(_attn_kernel, tq=tq, tks=tks, nsub=tk // tks, nk=nk, lam_init=lam_init),
        grid=(B, N_HEAD_BLOCKS, nq, nk),
        in_specs=[
            pl.BlockSpec((1, tq, LANES), lambda b, h, qi, ki: (b, qi, h)),
            pl.BlockSpec((1, tk, LANES), lambda b, h, qi, ki: (b, ki, h)),
            pl.BlockSpec((1, 1, VT_ROWS, tk), lambda b, h, qi, ki: (b, h, 0, ki)),
            pl.BlockSpec((4, DIFF_DH), lambda b, h, qi, ki: (0, 0)),
            pl.BlockSpec((LANES, 1), lambda b, h, qi, ki: (0, 0)),
        ],
        out_specs=pl.BlockSpec((1, tq, LANES), lambda b, h, qi, ki: (b, qi, h)),
        out_shape=jax.ShapeDtypeStruct((B, L, D_GROUP), BF16),
        scratch_shapes=[pltpu.VMEM((2 * tq, LANES), BF16), pltpu.VMEM((1, 2 * tq), F32),
                        pltpu.VMEM((VT_ROWS, 2 * tq), F32)],
        compiler_params=_cp(("parallel", "parallel", "parallel", "arbitrary")),
    )(qr, kr, vt, diff_lam, subln.reshape(LANES, 1))


def _log_gamma(rho_ref):
    return -_softplus(-rho_ref[0, 0])


def _ret_state_kernel(kf_ref, vf_ref, kb_ref, vb_ref, rf_ref, rb_ref, sf_ref, sb_ref, sf_s, sb_s, *, C):
    n = pl.program_id(2)

    @pl.when(n == 0)
    def _():
        sf_s[...] = jnp.zeros_like(sf_s)
        sb_s[...] = jnp.zeros_like(sb_s)

    sf_ref[0, 0, 0] = sf_s[...].astype(BF16)
    sb_ref[0, 0, 0] = sb_s[...].astype(BF16)
    lgf = _log_gamma(rf_ref)
    lgb = _log_gamma(rb_ref)
    row = lax.broadcasted_iota(I32, (C, LANES), 0).astype(F32)
    kdf = kf_ref[0].astype(F32) * jnp.exp(lgf * (C - 1.0 - row))
    kdb = kb_ref[0].astype(F32) * jnp.exp(lgb * row)
    sf_s[...] = sf_s[...] * jnp.exp(lgf * C) + _dot(kdf.T.astype(BF16), vf_ref[0].astype(BF16))
    sb_s[...] = sb_s[...] * jnp.exp(lgb * C) + _dot(kdb.T.astype(BF16), vb_ref[0].astype(BF16))


def _ret_main_kernel(q_ref, k_ref, v_ref, g_ref, sf_ref, sb_ref, rf_ref, rb_ref, o_ref, *, C):
    lgf = _log_gamma(rf_ref)
    lgb = _log_gamma(rb_ref)
    q = q_ref[0]
    s = _dot_nt(q, k_ref[0])
    r = lax.broadcasted_iota(I32, (C, C), 0)
    c = lax.broadcasted_iota(I32, (C, C), 1)
    dist = (r - c).astype(F32)
    dmat = jnp.exp(jnp.where(r >= c, lgf * dist, -lgb * dist))
    inner = _dot((s * dmat).astype(BF16), v_ref[0].astype(BF16))
    row = lax.broadcasted_iota(I32, (C, LANES), 0).astype(F32)
    qf = q.astype(F32)
    qcat = jnp.concatenate([qf * jnp.exp(lgf * (row + 1.0)), qf * jnp.exp(lgb * (C - row))], axis=1)
    scat = jnp.concatenate([sf_ref[0, 0, 0], sb_ref[0, 0, 0]], axis=0)
    y = inner + _dot(qcat.astype(BF16), scat)
    y = y * lax.rsqrt(jnp.mean(y * y, axis=-1, keepdims=True) + EPS)
    g = g_ref[0]
    o_ref[0] = (y * (g * _sigmoid(g))).astype(BF16)


def _retention(proj, qr, kr, rho, v_blk0, g_blk0, C=512):
    B, L, _ = proj.shape
    H = N_HEAD_BLOCKS
    C = min(C, L)
    N = L // C
    rho4 = rho.reshape(2, H, 1, 1)
    rf_spec = pl.BlockSpec((1, 1, 1, 1), lambda b, h, n: (0, h, 0, 0))
    rb_spec = pl.BlockSpec((1, 1, 1, 1), lambda b, h, n: (1, h, 0, 0))
    st_shape = jax.ShapeDtypeStruct((B, H, N, LANES, LANES), BF16)
    sf, sb = pl.pallas_call(
        functools.partial(_ret_state_kernel, C=C),
        grid=(B, H, N),
        in_specs=[
            pl.BlockSpec((1, C, LANES), lambda b, h, n: (b, n, h)),
            pl.BlockSpec((1, C, LANES), lambda b, h, n: (b, n, v_blk0 + h)),
            pl.BlockSpec((1, C, LANES), lambda b, h, n: (b, N - 1 - n, h)),
            pl.BlockSpec((1, C, LANES), lambda b, h, n: (b, N - 1 - n, v_blk0 + h)),
            rf_spec, rb_spec,
        ],
        out_specs=[pl.BlockSpec((1, 1, 1, LANES, LANES), lambda b, h, n: (b, h, n, 0, 0)),
                   pl.BlockSpec((1, 1, 1, LANES, LANES), lambda b, h, n: (b, h, N - 1 - n, 0, 0))],
        out_shape=[st_shape, st_shape],
        scratch_shapes=[pltpu.VMEM((LANES, LANES), F32), pltpu.VMEM((LANES, LANES), F32)],
        compiler_params=_cp(("parallel", "parallel", "arbitrary")),
    )(kr, proj, kr, proj, rho4, rho4)
    st_spec = pl.BlockSpec((1, 1, 1, LANES, LANES), lambda b, h, n: (b, h, n, 0, 0))
    return pl.pallas_call(
        functools.partial(_ret_main_kernel, C=C),
        grid=(B, H, N),
        in_specs=[
            pl.BlockSpec((1, C, LANES), lambda b, h, n: (b, n, h)),
            pl.BlockSpec((1, C, LANES), lambda b, h, n: (b, n, h)),
            pl.BlockSpec((1, C, LANES), lambda b, h, n: (b, n, v_blk0 + h)),
            pl.BlockSpec((1, C, LANES), lambda b, h, n: (b, n, g_blk0 + h)),
            st_spec, st_spec, rf_spec, rb_spec,
        ],
        out_specs=pl.BlockSpec((1, C, LANES), lambda b, h, n: (b, n, h)),
        out_shape=jax.ShapeDtypeStruct((B, L, D_GROUP), BF16),
        compiler_params=_cp(("parallel", "parallel", "parallel")),
    )(qr, kr, proj, proj, sf, sb, rho4, rho4)


def _dwconv3_kernel(x_ref, prev_ref, next_ref, w_ref, b_ref, o_ref, *, T, nt):
    i = pl.program_id(1)
    x = x_ref[0]
    prev = jnp.where(i == 0, 0.0, prev_ref[0])
    nxt = jnp.where(i == nt - 1, 0.0, next_ref[0])
    xe = jnp.concatenate([prev, x, nxt], axis=0)
    w = w_ref[...]
    o_ref[0] = (b_ref[...] + w[0:1] * xe[HALO - 1:HALO - 1 + T] + w[1:2] * xe[HALO:HALO + T]
                + w[2:3] * xe[HALO + 1:HALO + 1 + T])


def _dwconv3(proj, conv_w, conv_b, T=512):
    B, L, _ = proj.shape
    T = min(T, L)
    nt = L // T
    hb = T // HALO
    nh = L // HALO
    W = 3 * D_GROUP
    return pl.pallas_call(
        functools.partial(_dwconv3_kernel, T=T, nt=nt),
        grid=(B, nt),
        in_specs=[
            pl.BlockSpec((1, T, W), lambda b, i: (b, i, 0)),
            pl.BlockSpec((1, HALO, W), lambda b, i: (b, jnp.maximum(i * hb - 1, 0), 0)),
            pl.BlockSpec((1, HALO, W), lambda b, i: (b, jnp.minimum((i + 1) * hb, nh - 1), 0)),
            pl.BlockSpec((3, W), lambda b, i: (0, 0)),
            pl.BlockSpec((1, W), lambda b, i: (0, 0)),
        ],
        out_specs=pl.BlockSpec((1, T, W), lambda b, i: (b, i, 0)),
        out_shape=jax.ShapeDtypeStruct((B, L, W), F32),
        compiler_params=_cp(("parallel", "parallel")),
    )(proj, proj, proj, conv_w, conv_b.reshape(1, W))


def _filter_mlp_kernel(z_ref, w1_ref, b1_ref, w2_ref, b2_ref, w3_ref, fr_ref, hf_ref, sum_ref, *, TL, L):
    i = pl.program_id(0)
    fr = fr_ref[...]
    h = jnp.sin(fr * (_dot(z_ref[...].astype(BF16), w1_ref[...]) + b1_ref[...]))
    for j in range(2):
        h = jnp.sin(fr * (_dot(h.astype(BF16), w2_ref[j]) + b2_ref[j]))
    hf = _dot(h.astype(BF16), w3_ref[...])
    row = i * TL + lax.broadcasted_iota(I32, (TL, D_GROUP), 0)
    t = row.astype(F32) / (L - 1.0)
    ch = lax.broadcasted_iota(I32, (TL, D_GROUP), 1).astype(F32)
    max_decay = math.log(HY_TARGET) / HY_FAST_DECAY
    min_decay = math.log(HY_TARGET) / HY_SLOW_DECAY
    delta = jnp.abs(min_decay + (max_decay - min_decay) * ch / (D_GROUP - 1.0))
    dec = jnp.exp(-t * delta)
    keep = row < L - 1

    @pl.when(i == 0)
    def _():
        sum_ref[...] = jnp.zeros_like(sum_ref)

    for q in range(4):
        sl = slice(q * D_GROUP, (q + 1) * D_GROUP)
        v = hf[:, sl] * dec
        if q % 2 == 1:
            v = jnp.where(keep, v, 0.0)
        hf_ref[:, sl] = v
        sum_ref[:, sl] += jnp.sum(jnp.abs(v), axis=0, keepdims=True)


def _filter_mlp(L, w1, b1, w2, b2, w3, freq, TL=512, zero_last=True):
    TL = min(TL, L)
    t = jnp.linspace(0.0, 1.0, L, dtype=F32)[:, None]
    bands = (HY_EMB - 1) // 2
    wpos = 2.0 * math.pi * jnp.arange(L, dtype=F32)[:, None] / L
    f = jnp.linspace(1e-4, bands - 1, bands, dtype=F32)[None, :]
    z = jnp.concatenate([t, jnp.cos(f * wpos), jnp.sin(f * wpos)], axis=-1)
    z = jnp.pad(z, ((0, 0), (0, LANES - HY_EMB)))
    pf = LANES - HY_FFN
    w1p = jnp.pad(w1, ((0, LANES - HY_EMB), (0, pf))).astype(BF16)
    b1p = jnp.pad(b1, (0, pf)).reshape(1, LANES)
    w2p = jnp.pad(w2, ((0, 0), (0, pf), (0, pf))).astype(BF16)
    b2p = jnp.pad(b2, ((0, 0), (0, pf))).reshape(2, 1, LANES)
    w3p = jnp.pad(w3, ((0, pf), (0, 0))).astype(BF16)
    frp = jnp.pad(freq, (0, pf)).reshape(1, LANES)
    W = 4 * D_GROUP
    full = lambda *shape: pl.BlockSpec(shape, lambda i: (0,) * len(shape))
    return pl.pallas_call(
        functools.partial(_filter_mlp_kernel, TL=TL, L=L),
        grid=(L // TL,),
        in_specs=[pl.BlockSpec((TL, LANES), lambda i: (i, 0)), full(LANES, LANES), full(1, LANES),
                  full(2, LANES, LANES), full(2, 1, LANES), full(LANES, W), full(1, LANES)],
        out_specs=[pl.BlockSpec((TL, W), lambda i: (i, 0)), full(1, W)],
        out_shape=[jax.ShapeDtypeStruct((L, W), F32), jax.ShapeDtypeStruct((1, W), F32)],
        compiler_params=_cp(("arbitrary",)),
    )(z, w1p, b1p, w2p, b2p, w3p, frp)


def _dft_tables(N):
    N2 = FFT_N2
    N1 = N // N2
    k1 = np.arange(N1)
    a1 = 2.0 * math.pi * ((k1[:, None] * k1[None, :]) % N1) / N1
    f1 = jnp.asarray(np.concatenate([np.cos(a1), -np.sin(a1)], axis=0), F32)
    f1inv = jnp.asarray(np.concatenate([np.cos(a1), -np.sin(a1)], axis=1)[:N1 // 2] / N, F32)
    k2 = np.arange(N2)
    a2 = 2.0 * math.pi * ((k2[:, None] * k2[None, :]) % N2) / N2
    fr, fi = np.cos(a2), -np.sin(a2)
    f2 = jnp.asarray(np.block([[fr, -fi], [fi, fr]]), F32).astype(BF16)
    f2inv = jnp.asarray(np.block([[fr, fi], [-fi, fr]]), F32).astype(BF16)
    at = 2.0 * math.pi * (k2[:, None] * k1[None, :]) / N
    tw = lambda a: jnp.broadcast_to(jnp.asarray(a, F32)[:, :, None], a.shape + (LANES,))
    return dict(N1=N1, N2=N2, f1=f1.astype(BF16), f1half=f1[:, :N1 // 2].astype(BF16), f1inv=f1inv.astype(BF16),
                f2=f2, f2inv=f2inv, twa_c=tw(np.cos(at)), twa_s=tw(-np.sin(at)),
                twb_c=tw(np.cos(at.T)), twb_s=tw(np.sin(at.T)))


def _dft1_kernel(x_ref, f_ref, c_ref, s_ref, o_ref, *, N1, C):
    a = _dot(f_ref[...], x_ref[0].astype(BF16))
    c = c_ref[0]
    s = s_ref[0]
    for j in range(C // LANES):
        sl = slice(j * LANES, (j + 1) * LANES)
        ar = a[0:N1, sl]
        ai = a[N1:2 * N1, sl]
        o_ref[0, 0:N1, sl] = (ar * c - ai * s).astype(BF16)
        o_ref[0, N1:2 * N1, sl] = (ar * s + ai * c).astype(BF16)


def _dft1(x2d, fmat, tab, C):
    Bt, R, _ = x2d.shape
    N1, N2 = tab["N1"], tab["N2"]
    return pl.pallas_call(
        functools.partial(_dft1_kernel, N1=N1, C=C),
        grid=(Bt, N2),
        in_specs=[pl.BlockSpec((1, R, C), lambda b, t: (b, 0, t)),
                  pl.BlockSpec((2 * N1, R), lambda b, t: (0, 0)),
                  pl.BlockSpec((1, N1, LANES), lambda b, t: (t, 0, 0)),
                  pl.BlockSpec((1, N1, LANES), lambda b, t: (t, 0, 0))],
        out_specs=pl.BlockSpec((1, 2 * N1, C), lambda b, t: (b, 0, t)),
        out_shape=jax.ShapeDtypeStruct((Bt, 2 * N1, N2 * C), BF16),
        compiler_params=_cp(("parallel", "parallel")),
    )(x2d, fmat, tab["twa_c"], tab["twa_s"])


def _filter_dft2_kernel(a_ref, f_ref, inv_ref, o_ref, *, N2):
    x = a_ref[:, 0].reshape(2 * N2, a_ref.shape[-1])
    z = _dot(f_ref[...], x) * inv_ref[...]
    o_ref[:, 0] = z.reshape(2, N2, z.shape[-1]).astype(BF16)


def _filter_dft2(a4, tab, inv_sum, Ct=1024):
    _, N1, N2, Ck = a4.shape
    blk = pl.BlockSpec((2, 1, N2, Ct), lambda k, c: (0, k, 0, c))
    return pl.pallas_call(
        functools.partial(_filter_dft2_kernel, N2=N2),
        grid=(N1, Ck // Ct),
        in_specs=[blk, pl.BlockSpec((2 * N2, 2 * N2), lambda k, c: (0, 0)),
                  pl.BlockSpec((1, Ct), lambda k, c: (0, c))],
        out_specs=blk,
        out_shape=jax.ShapeDtypeStruct(a4.shape, BF16),
        compiler_params=_cp(("parallel", "parallel")),
    )(a4, tab["f2"], inv_sum)


def _conv_mid_kernel(a_ref, kf_ref, f_ref, fi_ref, c_ref, s_ref, o_ref, *, N2):
    Ct = a_ref.shape[-1]
    x = a_ref[0, :, 0].reshape(2 * N2, Ct)
    z = _dot(f_ref[...], x)
    zr, zi = z[0:N2], z[N2:2 * N2]
    kr = kf_ref[0, 0].astype(F32)
    ki = kf_ref[1, 0].astype(F32)
    y = jnp.concatenate([zr * kr - zi * ki, zr * ki + zi * kr], axis=0).astype(BF16)
    b = _dot(fi_ref[...], y)
    c = c_ref[0]
    s = s_ref[0]
    for j in range(Ct // LANES):
        sl = slice(j * LANES, (j + 1) * LANES)
        br = b[0:N2, sl]
        bi = b[N2:2 * N2, sl]
        o_ref[0, 0, 0, :, sl] = (br * c - bi * s).astype(BF16)
        o_ref[0, 1, 0, :, sl] = (br * s + bi * c).astype(BF16)


def _conv_mid(a5, kf4, order, tab, Ct=1024):
    B, _, N1, N2, C = a5.shape
    nc = C // Ct
    sq = pl.BlockSpec((2 * N2, 2 * N2), lambda b, k, c: (0, 0))
    twb = pl.BlockSpec((1, N2, LANES), lambda b, k, c: (k, 0, 0))
    blk = pl.BlockSpec((1, 2, 1, N2, Ct), lambda b, k, c: (b, 0, k, 0, c))
    return pl.pallas_call(
        functools.partial(_conv_mid_kernel, N2=N2),
        grid=(B, N1, nc),
        in_specs=[blk, pl.BlockSpec((2, 1, N2, Ct), lambda b, k, c: (0, k, 0, order * nc + c)), sq, sq, twb, twb],
        out_specs=blk,
        out_shape=jax.ShapeDtypeStruct(a5.shape, BF16),
        compiler_params=_cp(("parallel", "parallel", "parallel")),
    )(a5, kf4, tab["f2"], tab["f2inv"], tab["twb_c"], tab["twb_s"])


def _conv_out_kernel(b_ref, f_ref, g_ref, z_ref, bias_ref, o_ref):
    y = _dot(f_ref[...], b_ref[0])
    o_ref[0] = (g_ref[0] * (y + z_ref[0] * bias_ref[...])).astype(o_ref.dtype)


def _conv_out(b3, tab, uc2d, g_blk, z2d, z_blk, bias, out_dtype):
    B, _, M = b3.shape
    N1, N2 = tab["N1"], tab["N2"]
    C = M // N2
    R = N1 // 2
    nb_g = uc2d.shape[-1] // (N2 * C)
    nb_z = z2d.shape[-1] // (N2 * C)
    return pl.pallas_call(
        _conv_out_kernel,
        grid=(B, N2),
        in_specs=[pl.BlockSpec((1, 2 * N1, C), lambda b, t: (b, 0, t)),
                  pl.BlockSpec((R, 2 * N1), lambda b, t: (0, 0)),
                  pl.BlockSpec((1, R, C), lambda b, t: (b, 0, t * nb_g + g_blk)),
                  pl.BlockSpec((1, R, C), lambda b, t: (b, 0, t * nb_z + z_blk)),
                  pl.BlockSpec((1, C), lambda b, t: (0, 0))],
        out_specs=pl.BlockSpec((1, R, C), lambda b, t: (b, 0, t)),
        out_shape=jax.ShapeDtypeStruct((B, R, M), out_dtype),
        compiler_params=_cp(("parallel", "parallel")),
    )(b3, tab["f1inv"], uc2d, z2d, bias.reshape(1, C))


def _hyena(proj, conv_w, conv_b, w1, b1, w2, b2, w3, freq, bias):
    B, L, _ = proj.shape
    C = D_GROUP
    N = 2 * L
    tab = _dft_tables(N)
    N1, N2 = tab["N1"], tab["N2"]
    R = N1 // 2
    uc = _dwconv3(proj, conv_w, conv_b)
    hf, sums = _filter_mlp(L, w1, b1, w2, b2, w3, freq)
    hf4 = hf.reshape(L, 2, 2, C)
    kfull = jnp.concatenate([hf4[:, :, 0], jnp.zeros((1, 2, C), F32), jnp.flip(hf4[:L - 1, :, 1], 0)], axis=0)
    s4 = sums.reshape(2, 2, C)
    inv_sum = (1.0 / (s4[:, 0] + s4[:, 1])).reshape(1, 2 * C)
    fa = _dft1(kfull.reshape(1, N1, N2 * 2 * C), tab["f1"], tab, 2 * C)
    kf4 = _filter_dft2(fa.reshape(2, N1, N2, 2 * C), tab, inv_sum)
    uc2d = uc.reshape(B, R, N2 * 3 * C)
    z2d, z_blk = uc2d, 0
    for o in range(2):
        if o == 0:
            a3 = _dft1_cols(uc2d, tab, C)
        else:
            a3 = _dft1(z2d, tab["f1half"], tab, C)
        b5 = _conv_mid(a3.reshape(B, 2, N1, N2, C), kf4, o, tab)
        z2d = _conv_out(b5.reshape(B, 2 * N1, N2 * C), tab, uc2d, 1 + o, z2d, z_blk, bias[o],
                        F32 if o == 0 else BF16)
        z_blk = 0
    return z2d.reshape(B, L, C)


def _dft1_cols(uc2d, tab, C):
    B, R, _ = uc2d.shape
    N1, N2 = tab["N1"], tab["N2"]
    return pl.pallas_call(
        functools.partial(_dft1_kernel, N1=N1, C=C),
        grid=(B, N2),
        in_specs=[pl.BlockSpec((1, R, C), lambda b, t: (b, 0, 3 * t)),
                  pl.BlockSpec((2 * N1, R), lambda b, t: (0, 0)),
                  pl.BlockSpec((1, N1, LANES), lambda b, t: (t, 0, 0)),
                  pl.BlockSpec((1, N1, LANES), lambda b, t: (t, 0, 0))],
        out_specs=pl.BlockSpec((1, 2 * N1, C), lambda b, t: (b, 0, t)),
        out_shape=jax.ShapeDtypeStruct((B, 2 * N1, N2 * C), BF16),
        compiler_params=_cp(("parallel", "parallel")),
    )(uc2d, tab["f1half"], tab["twa_c"], tab["twa_s"])


T2_GROUP = 8
SLAB_PAD = 8


def _dft_tables_h(N):
    N2 = FFT_N2
    N1 = N // N2
    R = N1 // 2
    K1 = R + SLAB_PAD
    k1 = np.arange(K1)
    live = (k1 <= R).astype(np.float64)
    t1 = np.arange(R)
    a1 = 2.0 * math.pi * ((k1[:, None] * t1[None, :]) % N1) / N1
    f1 = np.concatenate([np.cos(a1) * live[:, None], -np.sin(a1) * live[:, None]], axis=0)
    wgt = np.where((k1 == 0) | (k1 == R), 1.0, 2.0) * live
    f1inv = np.concatenate([np.cos(a1) * wgt[:, None], -np.sin(a1) * wgt[:, None]], axis=0).T / N
    k2 = np.arange(N2)
    a2 = 2.0 * math.pi * ((k2[:, None] * k2[None, :]) % N2) / N2
    fr, fi = np.cos(a2), -np.sin(a2)
    at = 2.0 * math.pi * (k2[:, None] * k1[None, :]) / N
    ak = 2.0 * math.pi * (k1[:, None] + N1 * k2[None, :]) / N
    tw = lambda a: jnp.broadcast_to(jnp.asarray(a, F32)[:, :, None], a.shape + (LANES,))
    bf = lambda a: jnp.asarray(a, F32).astype(BF16)
    return dict(N1=N1, N2=N2, R=R, K1=K1, f1=bf(f1), f1inv=bf(f1inv),
                f2=bf(np.block([[fr, -fi], [fi, fr]])), f2inv=bf(np.block([[fr, fi], [-fi, fr]])),
                twa_c=tw(np.cos(at)), twa_s=tw(-np.sin(at)), twb_c=tw(np.cos(at.T)), twb_s=tw(np.sin(at.T)),
                wk_c=tw(np.cos(ak)), wk_s=tw(np.sin(ak)))


def _fft1_kernel(x_ref, f_ref, c_ref, s_ref, o_ref, *, K1, C):
    for j in range(T2_GROUP):
        a = _dot(f_ref[...], x_ref[0, :, j, :].astype(BF16))
        c = c_ref[j]
        s = s_ref[j]
        for cb in range(C // LANES):
            sl = slice(cb * LANES, (cb + 1) * LANES)
            ar = a[0:K1, sl]
            ai = a[K1:2 * K1, sl]
            o_ref[0, 0, j, 0:K1, sl] = ar * c - ai * s
            o_ref[0, 0, j, K1:2 * K1, sl] = ar * s + ai * c


def _fft1(x4, tab, col0, ncol):
    Bx, R, N2, _ = x4.shape
    K1 = tab["K1"]
    C = D_GROUP
    G = T2_GROUP
    return pl.pallas_call(
        functools.partial(_fft1_kernel, K1=K1, C=C),
        grid=(Bx, ncol, N2 // G),
        in_specs=[pl.BlockSpec((1, R, G, C), lambda b, ci, g: (b, 0, g, col0 + ci)),
                  pl.BlockSpec((2 * K1, R), lambda b, ci, g: (0, 0)),
                  pl.BlockSpec((G, K1, LANES), lambda b, ci, g: (g, 0, 0)),
                  pl.BlockSpec((G, K1, LANES), lambda b, ci, g: (g, 0, 0))],
        out_specs=pl.BlockSpec((1, 1, G, 2 * K1, C), lambda b, ci, g: (b, ci, g, 0, 0)),
        out_shape=jax.ShapeDtypeStruct((Bx, ncol, N2, 2 * K1, C), F32),
        compiler_params=_cp(("parallel", "parallel", "parallel")),
    )(x4, tab["f1"], tab["twa_c"], tab["twa_s"])


def _strided_cat(re_ref, im_ref, kk):
    return jnp.concatenate([re_ref[0, 0, :, kk, :], im_ref[0, 0, :, kk, :]], axis=0).astype(BF16)


def _filter_fft2_kernel(pr_ref, pi_ref, mr_ref, mi_ref, f_ref, c_ref, s_ref, inv_ref, o_ref, *, N2):
    for kk in range(T2_GROUP):
        zp = _dot(f_ref[...], _strided_cat(pr_ref, pi_ref, kk))
        zm = _dot(f_ref[...], _strided_cat(mr_ref, mi_ref, kk))
        c = c_ref[kk]
        s = s_ref[kk]
        inv = inv_ref[...]
        for cb in range(zp.shape[1] // LANES):
            sl = slice(cb * LANES, (cb + 1) * LANES)
            zmr, zmi = zm[0:N2, sl], zm[N2:2 * N2, sl]
            o_ref[kk, 0, :, sl] = ((zp[0:N2, sl] + c * zmr + s * zmi) * inv[:, sl]).astype(BF16)
            o_ref[kk, 1, :, sl] = ((zp[N2:2 * N2, sl] + s * zmr - c * zmi) * inv[:, sl]).astype(BF16)


def _filter_fft2(af, tab, inv_sum, Ct=512):
    _, _, N2, _, C = af.shape
    K1 = tab["K1"]
    G = T2_GROUP
    nkg = K1 // G
    nc = C // Ct

    def a_spec(side, im):
        return pl.BlockSpec((1, 1, N2, G, Ct), lambda kg, o, cj: (0, 2 * o + side, 0, im * nkg + kg, cj))

    tw_spec = pl.BlockSpec((G, N2, LANES), lambda kg, o, cj: (kg, 0, 0))
    return pl.pallas_call(
        functools.partial(_filter_fft2_kernel, N2=N2),
        grid=(nkg, 2, nc),
        in_specs=[a_spec(0, 0), a_spec(0, 1), a_spec(1, 0), a_spec(1, 1),
                  pl.BlockSpec((2 * N2, 2 * N2), lambda kg, o, cj: (0, 0)), tw_spec, tw_spec,
                  pl.BlockSpec((1, Ct), lambda kg, o, cj: (0, o * nc + cj))],
        out_specs=pl.BlockSpec((G, 2, N2, Ct), lambda kg, o, cj: (kg, 0, 0, o * nc + cj)),
        out_shape=jax.ShapeDtypeStruct((K1, 2, N2, 2 * C), BF16),
        compiler_params=_cp(("parallel", "parallel", "parallel")),
    )(af, af, af, af, tab["f2"], tab["wk_c"], tab["wk_s"], inv_sum)


def _conv_mid_h_kernel(ar_ref, ai_ref, kf_ref, f_ref, fi_ref, c_ref, s_ref, o_ref, *, N2):
    for kk in range(T2_GROUP):
        z = _dot(f_ref[...], _strided_cat(ar_ref, ai_ref, kk))
        zr, zi = z[0:N2], z[N2:2 * N2]
        kr = kf_ref[kk, 0].astype(F32)
        ki = kf_ref[kk, 1].astype(F32)
        y = jnp.concatenate([zr * kr - zi * ki, zr * ki + zi * kr], axis=0).astype(BF16)
        b = _dot(fi_ref[...], y)
        c = c_ref[kk]
        s = s_ref[kk]
        for cb in range(b.shape[1] // LANES):
            sl = slice(cb * LANES, (cb + 1) * LANES)
            br = b[0:N2, sl]
            bi = b[N2:2 * N2, sl]
            o_ref[0, 0, kk, :, sl] = br * c - bi * s
            o_ref[0, 1, kk, :, sl] = br * s + bi * c


def _conv_mid_h(a5, kf, order, tab, Ct=512):
    B, _, N2, _, C = a5.shape
    K1 = tab["K1"]
    G = T2_GROUP
    nkg = K1 // G
    nc = C // Ct
    sq = pl.BlockSpec((2 * N2, 2 * N2), lambda b, kg, cj: (0, 0))
    tw_spec = pl.BlockSpec((G, N2, LANES), lambda b, kg, cj: (kg, 0, 0))
    return pl.pallas_call(
        functools.partial(_conv_mid_h_kernel, N2=N2),
        grid=(B, nkg, nc),
        in_specs=[pl.BlockSpec((1, 1, N2, G, Ct), lambda b, kg, cj: (b, 0, 0, kg, cj)),
                  pl.BlockSpec((1, 1, N2, G, Ct), lambda b, kg, cj: (b, 0, 0, nkg + kg, cj)),
                  pl.BlockSpec((G, 2, N2, Ct), lambda b, kg, cj: (kg, 0, 0, order * nc + cj)),
                  sq, sq, tw_spec, tw_spec],
        out_specs=pl.BlockSpec((1, 2, G, N2, Ct), lambda b, kg, cj: (b, 0, kg, 0, cj)),
        out_shape=jax.ShapeDtypeStruct((B, 2, K1, N2, C), F32),
        compiler_params=_cp(("parallel", "parallel", "parallel")),
    )(a5, a5, kf, tab["f2"], tab["f2inv"], tab["twb_c"], tab["twb_s"])


def _conv_out_h_kernel(b_ref, f_ref, g_ref, z_ref, bias_ref, o_ref, *, K1):
    for j in range(T2_GROUP):
        bj = jnp.concatenate([b_ref[0, 0, :, j, :], b_ref[0, 1, :, j, :]], axis=0).astype(BF16)
        y = _dot(f_ref[...], bj)
        o_ref[0, :, j, :] = g_ref[0, :, j, :] * (y + z_ref[0, :, j, :] * bias_ref[...])


def _conv_out_h(b5, tab, uc4, g_blk, z4, z_blk, bias):
    B, _, K1, N2, C = b5.shape
    R = tab["R"]
    G = T2_GROUP
    return pl.pallas_call(
        functools.partial(_conv_out_h_kernel, K1=K1),
        grid=(B, N2 // G),
        in_specs=[pl.BlockSpec((1, 2, K1, G, C), lambda b, g: (b, 0, 0, g, 0)),
                  pl.BlockSpec((R, 2 * K1), lambda b, g: (0, 0)),
                  pl.BlockSpec((1, R, G, C), lambda b, g: (b, 0, g, g_blk)),
                  pl.BlockSpec((1, R, G, C), lambda b, g: (b, 0, g, z_blk)),
                  pl.BlockSpec((1, C), lambda b, g: (0, 0))],
        out_specs=pl.BlockSpec((1, R, G, C), lambda b, g: (b, 0, g, 0)),
        out_shape=jax.ShapeDtypeStruct((B, R, N2, C), F32),
        compiler_params=_cp(("parallel", "parallel")),
    )(b5, tab["f1inv"], uc4, z4, bias.reshape(1, C))


def _hyena_h(proj, conv_w, conv_b, w1, b1, w2, b2, w3, freq, bias):
    B, L, _ = proj.shape
    C = D_GROUP
    tab = _dft_tables_h(2 * L)
    N2, R = tab["N2"], tab["R"]
    uc4 = _dwconv3(proj, conv_w, conv_b).reshape(B, R, N2, 3 * C)
    hf, sums = _filter_mlp(L, w1, b1, w2, b2, w3, freq, zero_last=True)
    s4 = sums.reshape(2, 2, C)
    inv_sum = (1.0 / (s4[:, 0] + s4[:, 1])).reshape(1, 2 * C)
    af = _fft1(hf.reshape(1, R, N2, 4 * C), tab, 0, 4)
    kf = _filter_fft2(af, tab, inv_sum)
    z4, z_blk = uc4, 0
    for o in range(2):
        a5 = _fft1(z4, tab, z_blk, 1)
        b5 = _conv_mid_h(a5, kf, o, tab)
        z4 = _conv_out_h(b5, tab, uc4, 1 + o, z4, z_blk, bias[o])
        z_blk = 0
    return z4.reshape(B, L, C)


def _router_kernel(x_ref, g_ref, rt_ref, h_ref, aff_ref):
    d = x_ref.shape[1]
    xv = x_ref[...]
    ms = jnp.mean(xv * xv, axis=-1, keepdims=True)
    h = xv * lax.rsqrt(ms + EPS) * g_ref[...]
    h_ref[:, 0:d] = h
    logits = lax.dot_general(rt_ref[...], h, (((1,), (1,)), ((), ())), precision=lax.Precision.HIGHEST,
                             preferred_element_type=F32)
    row = lax.broadcasted_iota(I32, logits.shape, 0)
    logits = jnp.where(row < N_EXPERTS, logits, -jnp.inf)
    e = jnp.exp(logits - jnp.max(logits, axis=0, keepdims=True))
    aff = e / jnp.sum(e, axis=0, keepdims=True)
    aff_ref[...] = aff[0:N_EXPERTS]
    h_ref[:, d:d + LANES] = aff.T


def _router(x, g, router_t, tm=512):
    n, d = x.shape
    rt = jnp.pad(router_t, ((0, LANES - N_EXPERTS), (0, 0)))
    return pl.pallas_call(
        _router_kernel,
        grid=(n // tm,),
        in_specs=[pl.BlockSpec((tm, d), lambda i: (i, 0)), pl.BlockSpec((1, d), lambda i: (0, 0)),
                  pl.BlockSpec((LANES, d), lambda i: (0, 0))],
        out_specs=[pl.BlockSpec((tm, d + LANES), lambda i: (i, 0)), pl.BlockSpec((N_EXPERTS, tm), lambda i: (0, i))],
        out_shape=[jax.ShapeDtypeStruct((n, d + LANES), F32), jax.ShapeDtypeStruct((N_EXPERTS, n), F32)],
        compiler_params=_cp(("parallel",)),
    )(x, g.reshape(1, d), rt)


def _select_kernel(aff_ref, pos_ref, cs_ref, m_s, u_s, *, n, cap, CH):
    E = N_EXPERTS
    bits = pltpu.bitcast(aff_ref[...], I32)

    def search(it, thr):
        cand = thr | jnp.left_shift(jnp.int32(1), 30 - it)
        cnt = jnp.sum((bits >= cand).astype(F32), axis=1, keepdims=True)
        return jnp.where(cnt >= cap, cand, thr)

    thr = lax.fori_loop(0, 31, search, jnp.zeros((E, 1), I32))
    need = cap - jnp.sum((bits > thr).astype(F32), axis=1, keepdims=True)
    r = lax.broadcasted_iota(I32, (CH, CH), 0)
    c = lax.broadcasted_iota(I32, (CH, CH), 1)
    u_s[...] = (r <= c).astype(BF16)

    def chunk_bits(j):
        off = pl.multiple_of(j * CH, CH)
        return off, pltpu.bitcast(aff_ref[:, pl.ds(off, CH)], I32)

    def ties(j, carry):
        off, b = chunk_bits(j)
        eq = (b == thr).astype(F32)
        incl = _dot(eq.astype(BF16), u_s[...]) + carry
        sel = (b > thr) | ((b == thr) & (incl - eq < need))
        m_s[:, pl.ds(off, CH)] = sel.astype(F32)
        return incl[:, CH - 1:CH]

    lax.fori_loop(0, n // CH, ties, jnp.zeros((E, 1), F32))

    def slots(j, carry):
        off, _ = chunk_bits(j)
        m = m_s[:, pl.ds(off, CH)]
        incl = _dot(m.astype(BF16), u_s[...]) + carry
        cs_ref[:, pl.ds(off, CH)] = incl.astype(I32)
        pos_ref[:, pl.ds(off, CH)] = jnp.where(m > 0.0, incl - 1.0, -1.0).astype(I32)
        return incl[:, CH - 1:CH]

    lax.fori_loop(0, n // CH, slots, jnp.zeros((E, 1), F32))


def _select(aff, cap, CH=512):
    E, n = aff.shape
    CH = min(CH, n)
    full = pl.BlockSpec((E, n), lambda: (0, 0))
    return pl.pallas_call(
        functools.partial(_select_kernel, n=n, cap=cap, CH=CH),
        in_specs=[full],
        out_specs=[full, full, full],
        out_shape=[jax.ShapeDtypeStruct((E, n), I32), jax.ShapeDtypeStruct((E, n), I32),
                   jax.ShapeDtypeStruct((E, n), F32)],
        scratch_shapes=[pltpu.VMEM((CH, CH), BF16)],
        compiler_params=_cp(None),
    )(aff)


def _compact_kernel(m_ref, idx_ref, *, cap, RB):
    m = m_ref[0]
    nch = m.shape[0]
    r = lax.broadcasted_iota(I32, (LANES, LANES), 0)
    c = lax.broadcasted_iota(I32, (LANES, LANES), 1)
    local = _dot(m, (r <= c).astype(F32))
    tot = _dot(m, (r >= 0).astype(F32))
    rr = lax.broadcasted_iota(I32, (nch, nch), 0)
    cc = lax.broadcasted_iota(I32, (nch, nch), 1)
    cend = _dot((cc <= rr).astype(F32), tot)
    cend_row = cend.T[0:1, :]
    cstart_row = cend_row - tot.T[0:1, :]
    chunk_id = lax.broadcasted_iota(I32, (RB, nch), 1).astype(F32)

    def block(b, carry):
        j0 = pl.multiple_of(b * RB, RB)
        slot = (j0 + lax.broadcasted_iota(I32, (RB, 1), 0)).astype(F32)
        cstar = jnp.sum((slot >= cend_row).astype(F32), axis=1, keepdims=True)
        oh = chunk_id == cstar
        counts = _dot(jnp.where(oh, 1.0, 0.0), local)
        first = jnp.sum(jnp.where(oh, cstart_row, 0.0), axis=1, keepdims=True)
        within = jnp.sum((counts <= slot - first).astype(F32), axis=1, keepdims=True)
        tok = jnp.broadcast_to(LANES * cstar + within, (RB, LANES)).T
        idx_ref[0, :, pl.ds(j0, RB)] = tok[0:8].astype(I32)
        return carry

    lax.fori_loop(0, cap // RB, block, 0)


def _compact(mask, n, cap, RB=512):
    E = N_EXPERTS
    nch = n // LANES
    m3 = mask.reshape(E, nch, LANES)
    if nch < LANES:
        m3 = jnp.pad(m3, ((0, 0), (0, LANES - nch), (0, 0)))
        nch = LANES
    RB = min(RB, cap)
    idx8 = pl.pallas_call(
        functools.partial(_compact_kernel, cap=cap, RB=RB),
        grid=(E,),
        in_specs=[pl.BlockSpec((1, nch, LANES), lambda e: (e, 0, 0))],
        out_specs=pl.BlockSpec((1, 8, cap), lambda e: (e, 0, 0)),
        out_shape=jax.ShapeDtypeStruct((E, 8, cap), I32),
        compiler_params=_cp(("parallel",)),
    )(m3)
    return idx8[:, 0, :].reshape(-1)


def _expert_kernel(idx_ref, h_ref, wg_ref, wu_ref, wd_ref, o_ref, xbuf, xb, acc, sem, *, tm, nf, d):
    e = pl.program_id(0)
    r = pl.program_id(1)
    f = pl.program_id(2)

    @pl.when(f == 0)
    def _():
        def issue(j, carry):
            t = idx_ref[r * tm + j]
            pltpu.make_async_copy(h_ref.at[pl.ds(t, 1)], xbuf.at[pl.ds(j, 1)], sem).start()
            return carry

        lax.fori_loop(0, tm, issue, 0, unroll=8)
        pltpu.make_async_copy(h_ref.at[pl.ds(0, tm)], xbuf, sem).wait()
        xb[...] = xbuf[:, 0:d].astype(BF16)
        acc[...] = jnp.zeros_like(acc)

    x = xb[...]
    g = _dot(x, wg_ref[0])
    u = _dot(x, wu_ref[0])
    hid = (g * _sigmoid(g) * u).astype(BF16)
    acc[...] += _dot(hid, wd_ref[0])

    @pl.when(f == nf - 1)
    def _():
        aff = xbuf[:, d:d + LANES]
        lane = lax.broadcasted_iota(I32, aff.shape, 1)
        gate = jnp.sum(jnp.where(lane == e, aff, 0.0), axis=1, keepdims=True)
        o_ref[0] = (acc[...] * gate).astype(BF16)


def _experts(idx_flat, h, wg, wu, wd, cap, tm=1024, tf=512):
    n, dx = h.shape
    E, d, dff = wg.shape
    tm = min(tm, cap)
    nf = dff // tf
    return pl.pallas_call(
        functools.partial(_expert_kernel, tm=tm, nf=nf, d=d),
        grid=(E, cap // tm, nf),
        in_specs=[pl.BlockSpec((cap,), lambda e, r, f: (e,), memory_space=pltpu.SMEM),
                  pl.BlockSpec(memory_space=pl.ANY),
                  pl.BlockSpec((1, d, tf), lambda e, r, f: (e, 0, f)),
                  pl.BlockSpec((1, d, tf), lambda e, r, f: (e, 0, f)),
                  pl.BlockSpec((1, tf, d), lambda e, r, f: (e, f, 0))],
        out_specs=pl.BlockSpec((1, tm, d), lambda e, r, f: (e, r, 0)),
        out_shape=jax.ShapeDtypeStruct((E, cap, d), BF16),
        scratch_shapes=[pltpu.VMEM((tm, dx), F32), pltpu.VMEM((tm, d), BF16), pltpu.VMEM((tm, d), F32),
                        pltpu.SemaphoreType.DMA(())],
        compiler_params=_cp(("parallel", "arbitrary", "arbitrary")),
    )(idx_flat, h, wg, wu, wd)


COMBINE_WIN = 128
ROW_ALIGN = 16


def _combine_kernel(st_ref, x_ref, pos_ref, ye_ref, o_ref, buf, sems, buf2, sem2, p_s, *, TT, cap, nt):
    i = pl.program_id(0)
    E = N_EXPERTS
    W = COMBINE_WIN

    def window(tile, e, w):
        s0 = st_ref[tile * E + e]
        lo = lax.shift_left(lax.shift_right_logical(s0, 4), 4) + W * w
        return jnp.minimum(lo, cap - W), lo

    def first_copy(tile, slot, e):
        a, _ = window(tile, e, 0)
        return pltpu.make_async_copy(ye_ref.at[e, pl.ds(pl.multiple_of(a, ROW_ALIGN), W)],
                                     buf.at[slot, pl.ds(e * W, W)], sems.at[slot, e])

    slot = lax.rem(i, 2)

    @pl.when(i == 0)
    def _():
        for e in range(E):
            first_copy(0, 0, e).start()

    @pl.when(i + 1 < nt)
    def _():
        for e in range(E):
            first_copy(i + 1, 1 - slot, e).start()

    lane = lax.broadcasted_iota(I32, (TT, W), 1)

    def onehot(pc, a, lo):
        hit = (pc - a == lane) & (pc >= lo) & (pc < lo + W)
        return jnp.where(hit, 1.0, 0.0).astype(BF16)

    for e in range(E):
        a, lo = window(i, e, 0)
        p_s[:, e * W:(e + 1) * W] = onehot(pos_ref[:, e:e + 1], a, lo)
    for e in range(E):
        first_copy(i, slot, e).wait()
    o_ref[...] = x_ref[...] + _dot(p_s[...], buf[slot])

    for e in range(E):
        _, lo = window(i, e, 0)
        s1 = st_ref[(i + 1) * E + e]
        nwin = lax.shift_right_logical(s1 - lo + (W - 1), 7)

        def extra(w, carry):
            a2, lo2 = window(i, e, w)
            cp = pltpu.make_async_copy(ye_ref.at[e, pl.ds(pl.multiple_of(a2, ROW_ALIGN), W)], buf2, sem2)
            cp.start()
            cp.wait()
            o_ref[...] += _dot(onehot(pos_ref[:, e:e + 1], a2, lo2), buf2[...])
            return carry

        lax.fori_loop(1, nwin, extra, 0)


def _combine(starts, x, pos_t, ye, cap, TT=256):
    n, d = x.shape
    E = N_EXPERTS
    TT = min(TT, n)
    nt = n // TT
    grid_spec = pltpu.PrefetchScalarGridSpec(
        num_scalar_prefetch=1,
        grid=(nt,),
        in_specs=[pl.BlockSpec((TT, d), lambda i, st: (i, 0)),
                  pl.BlockSpec((TT, E), lambda i, st: (i, 0)),
                  pl.BlockSpec(memory_space=pl.ANY)],
        out_specs=pl.BlockSpec((TT, d), lambda i, st: (i, 0)),
        scratch_shapes=[pltpu.VMEM((2, E * COMBINE_WIN, d), BF16), pltpu.SemaphoreType.DMA((2, E)),
                        pltpu.VMEM((COMBINE_WIN, d), BF16), pltpu.SemaphoreType.DMA(()),
                        pltpu.VMEM((TT, E * COMBINE_WIN), BF16)],
    )
    return pl.pallas_call(
        functools.partial(_combine_kernel, TT=TT, cap=cap, nt=nt),
        grid_spec=grid_spec,
        out_shape=jax.ShapeDtypeStruct((n, d), F32),
        compiler_params=_cp(("arbitrary",)),
    )(starts, x, pos_t, ye)


def _moe(x, g, router, wg, wu, wd):
    n, d = x.shape
    E = N_EXPERTS
    cap = EC_CAPACITY * n // E
    TT = min(256, n)
    h, aff = _router(x, g, router.T)
    pos, cs, mask = _select(aff, cap)
    idx = _compact(mask, n, cap)
    ye = _experts(idx, h, wg, wu, wd, cap)
    starts = jnp.concatenate([jnp.zeros((E, 1), I32), cs[:, TT - 1::TT]], axis=1).T.reshape(-1)
    return _combine(starts, x, pos.T, ye, cap, TT=TT)


def _rmsnorm_kernel(x_ref, g_ref, o_ref):
    xv = x_ref[...]
    o_ref[...] = xv * lax.rsqrt(jnp.mean(xv * xv, axis=-1, keepdims=True) + EPS) * g_ref[...]


def _rmsnorm(x, g, tm=512):
    n, d = x.shape
    return pl.pallas_call(
        _rmsnorm_kernel,
        grid=(n // tm,),
        in_specs=[pl.BlockSpec((tm, d), lambda i: (i, 0)), pl.BlockSpec((1, d), lambda i: (0, 0))],
        out_specs=pl.BlockSpec((tm, d), lambda i: (i, 0)),
        out_shape=jax.ShapeDtypeStruct((n, d), F32),
        compiler_params=_cp(("parallel",)),
    )(x, g.reshape(1, d))


def _even_mixer(x, B, L, p):
    proj = _rms_matmul(x, p["norm"], p["w_in"]).reshape(B, L, 5 * D_GROUP)
    hf = _lru(proj, None, p["conv_w"], p["conv_b"], p["wcat"][0], p["ba"][0], p["bx"][0], p["lam"][0],
              reverse=False)
    a_out = _lru(proj, hf, p["conv_w"], p["conv_b"], p["wcat"][1], p["ba"][1], p["bx"][1], p["lam"][1],
                 reverse=True)
    inv_freq = ROPE_THETA ** (-jnp.arange(0, DIFF_DH, 2, dtype=F32) / DIFF_DH)
    tabs = _rope_tables(L, DIFF_DH, inv_freq)
    half = DIFF_DH // 2
    qr = _rope(proj, 2, tabs, sh_a=LANES - half, sh_b=half, scale=DIFF_DH ** -0.5 * math.log2(math.e))
    kr = _rope(proj, 3, tabs, sh_a=LANES - half, sh_b=half, scale=1.0)
    vt = _v_transpose(proj, 4)
    b_out = _diff_attention(qr, kr, vt, p["diff_lam"], p["subln"], p["lam_init"])
    return _out_matmul(a_out.reshape(B * L, D_GROUP), b_out.reshape(B * L, D_GROUP), p["w_out"], x)


def _odd_mixer(x, B, L, p):
    proj = _rms_matmul(x, p["norm"], p["w_in"]).reshape(B, L, 7 * D_GROUP)
    c_out = _hyena_h(proj, p["conv_w"], p["conv_b"], p["w1"], p["b1"], p["w2"], p["b2"], p["w3"], p["freq"],
                   p["bias"])
    inv_freq = 1.0 / (10000.0 ** jnp.linspace(0.0, 1.0, RET_DH // 2, dtype=F32))
    tabs = _rope_tables(L, RET_DH, inv_freq)
    half = RET_DH // 2
    qr = _rope(proj, 3, tabs, sh_a=LANES - half, sh_b=half, scale=1.0)
    kr = _rope(proj, 4, tabs, sh_a=LANES - half, sh_b=half, scale=RET_DH ** -0.5)
    d_out = _retention(proj, qr, kr, p["rho"], 5 * N_HEAD_BLOCKS, 6 * N_HEAD_BLOCKS)
    return _out_matmul(c_out.reshape(B * L, D_GROUP), d_out.reshape(B * L, D_GROUP), p["w_out"], x)


def _trunk(x3, layers, final_norm):
    B, L, D = x3.shape
    x = x3.reshape(B * L, D)
    for kind, mp, ep in layers:
        x = (_even_mixer if kind == "even" else _odd_mixer)(x, B, L, mp)
        x = _moe(x, ep["norm"], ep["router"], ep["wg"], ep["wu"], ep["wd"])
    return _rmsnorm(x, final_norm).reshape(B, L, D)


def kernel(x_prompt, x_sample, ev_norm, ev_w_in, ev_conv_w, ev_conv_b, ev_rg_wa, ev_rg_ba, ev_rg_wx, ev_rg_bx, ev_rg_lam, ev_diff_lam, ev_subln, ev_w_out, od_norm, od_w_in, od_conv_w, od_conv_b, od_flt_w1, od_flt_b1, od_flt_w2, od_flt_b2, od_flt_w3, od_flt_freq, od_flt_bias, od_ret_rho, od_w_out, moe_norm, moe_router, moe_w_gate, moe_w_up, moe_w_down, final_norm):
    depth = moe_norm.shape[0]
    layers = []
    for layer in range(depth):
        j = layer // 2
        if layer % 2 == 0:
            mp = dict(norm=ev_norm[j], w_in=ev_w_in[j].astype(BF16), conv_w=ev_conv_w[j], conv_b=ev_conv_b[j],
                      wcat=jnp.concatenate([ev_rg_wa[j], ev_rg_wx[j]], axis=-1).astype(BF16),
                      ba=ev_rg_ba[j], bx=ev_rg_bx[j], lam=ev_rg_lam[j], diff_lam=ev_diff_lam[j],
                      subln=ev_subln[j], w_out=ev_w_out[j].astype(BF16),
                      lam_init=0.8 - 0.6 * math.exp(-0.3 * layer))
            kind = "even"
        else:
            mp = dict(norm=od_norm[j], w_in=od_w_in[j].astype(BF16), conv_w=od_conv_w[j], conv_b=od_conv_b[j],
                      w1=od_flt_w1[j], b1=od_flt_b1[j], w2=od_flt_w2[j], b2=od_flt_b2[j], w3=od_flt_w3[j],
                      freq=od_flt_freq[j], bias=od_flt_bias[j], rho=od_ret_rho[j],
                      w_out=od_w_out[j].astype(BF16))
            kind = "odd"
        ep = dict(norm=moe_norm[layer], router=moe_router[layer], wg=moe_w_gate[layer].astype(BF16),
                  wu=moe_w_up[layer].astype(BF16), wd=moe_w_down[layer].astype(BF16))
        layers.append((kind, mp, ep))
    return (_trunk(x_prompt, layers, final_norm), _trunk(x_sample, layers, final_norm))
```

```python
import functools
import math

import jax
import jax.numpy as jnp
import numpy as np
from jax import lax
from jax.experimental import pallas as pl
from jax.experimental.pallas import tpu as pltpu

F32 = jnp.float32
BF16 = jnp.bfloat16
I32 = jnp.int32

D_MODEL = 2048
D_GROUP = 1024
LANES = 128
N_HEAD_BLOCKS = D_GROUP // LANES
LRU_C = 8.0
DIFF_DH = 64
ROPE_THETA = 10000.0
RET_DH = 128
HY_EMB = 33
HY_FFN = 64
HY_FAST_DECAY = 0.3
HY_SLOW_DECAY = 1.5
HY_TARGET = 1e-2
N_EXPERTS = 16
EC_CAPACITY = 2
EPS = 1e-6
FFT_N2 = 128
HALO = 8
VMEM_LIMIT_MB = 56


def _cp(sem, vmem_mb=VMEM_LIMIT_MB):
    return pltpu.CompilerParams(dimension_semantics=sem, vmem_limit_bytes=vmem_mb * 1024 * 1024)


def _sigmoid(x):
    return 1.0 / (1.0 + jnp.exp(-x))


def _softplus(x):
    return jnp.maximum(x, 0.0) + jnp.log(1.0 + jnp.exp(-jnp.abs(x)))


def _dot(a, b):
    return jnp.dot(a, b, preferred_element_type=F32)


def _dot_nt(a, b):
    return lax.dot_general(a, b, (((1,), (1,)), ((), ())), preferred_element_type=F32)


def _rms_matmul_kernel(x_ref, g_ref, w_ref, o_ref, h_ref):
    @pl.when(pl.program_id(1) == 0)
    def _():
        xv = x_ref[...]
        ms = jnp.mean(xv * xv, axis=-1, keepdims=True)
        h_ref[...] = (xv * lax.rsqrt(ms + EPS) * g_ref[...]).astype(BF16)

    o_ref[...] = _dot(h_ref[...], w_ref[...])


def _rms_matmul(x, g, w_bf, tm=1024, tn=1024):
    n, d = x.shape
    tm = min(tm, n)
    nout = w_bf.shape[1]
    return pl.pallas_call(
        _rms_matmul_kernel,
        grid=(n // tm, nout // tn),
        in_specs=[
            pl.BlockSpec((tm, d), lambda i, j: (i, 0)),
            pl.BlockSpec((1, d), lambda i, j: (0, 0)),
            pl.BlockSpec((d, tn), lambda i, j: (0, j)),
        ],
        out_specs=pl.BlockSpec((tm, tn), lambda i, j: (i, j)),
        out_shape=jax.ShapeDtypeStruct((n, nout), F32),
        scratch_shapes=[pltpu.VMEM((tm, d), BF16)],
        compiler_params=_cp(("parallel", "arbitrary")),
    )(x, g.reshape(1, d), w_bf)


def _out_matmul_kernel(a_ref, b_ref, wa_ref, wb_ref, x_ref, o_ref):
    o_ref[...] = (x_ref[...] + _dot(a_ref[...].astype(BF16), wa_ref[...])
                  + _dot(b_ref[...].astype(BF16), wb_ref[...]))


def _out_matmul(a, b, w_bf, x, tm=1024, tn=1024):
    n, d = x.shape
    tm = min(tm, n)
    dg = a.shape[1]
    return pl.pallas_call(
        _out_matmul_kernel,
        grid=(n // tm, d // tn),
        in_specs=[
            pl.BlockSpec((tm, dg), lambda i, j: (i, 0)),
            pl.BlockSpec((tm, dg), lambda i, j: (i, 0)),
            pl.BlockSpec((dg, tn), lambda i, j: (0, j)),
            pl.BlockSpec((dg, tn), lambda i, j: (1, j)),
            pl.BlockSpec((tm, tn), lambda i, j: (i, j)),
        ],
        out_specs=pl.BlockSpec((tm, tn), lambda i, j: (i, j)),
        out_shape=jax.ShapeDtypeStruct((n, d), F32),
        compiler_params=_cp(("parallel", "parallel")),
    )(a, b, w_bf, w_bf, x)


def _lru_kernel(*refs, reverse, T, nt):
    if reverse:
        (x_ref, prev_ref, next_ref, cw_ref, cb_ref, w_ref, ba_ref, bx_ref, lam_ref,
         gate_ref, hf_ref, o_ref, a_s, b_s, carry, h_s) = refs
    else:
        (x_ref, prev_ref, next_ref, cw_ref, cb_ref, w_ref, ba_ref, bx_ref, lam_ref,
         o_ref, a_s, b_s, carry) = refs
    i = pl.program_id(1)
    ti = (nt - 1 - i) if reverse else i

    @pl.when(i == 0)
    def _():
        carry[...] = jnp.zeros_like(carry)

    x = x_ref[0]
    prev = jnp.where(ti == 0, 0.0, prev_ref[0])
    nxt = jnp.where(ti == nt - 1, 0.0, next_ref[0])
    xe = jnp.concatenate([prev, x, nxt], axis=0)
    cw = cw_ref[...]
    xc = (cb_ref[...] + cw[0:1] * xe[HALO - 2:HALO - 2 + T] + cw[1:2] * xe[HALO - 1:HALO - 1 + T]
          + cw[2:3] * xe[HALO:HALO + T] + cw[3:4] * xe[HALO + 1:HALO + 1 + T])
    sp = _softplus(-lam_ref[...])
    for blk in range(N_HEAD_BLOCKS):
        sl = slice(blk * LANES, (blk + 1) * LANES)
        xb = xc[:, sl]
        ri = _dot(xb.astype(BF16), w_ref[blk])
        r = _sigmoid(ri[:, :LANES] + ba_ref[:, sl])
        ig = _sigmoid(ri[:, LANES:] + bx_ref[:, sl])
        a = jnp.exp(-LRU_C * r * sp[:, sl])
        a_s[:, sl] = a
        b_s[:, sl] = jnp.sqrt(1.0 - a * a) * (ig * xb)

    dst = h_s if reverse else o_ref.at[0]

    def body(k, h):
        t = (T - 1 - k) if reverse else k
        h = a_s[pl.ds(t, 1), :] * h + b_s[pl.ds(t, 1), :]
        dst[pl.ds(t, 1), :] = h
        return h

    carry[...] = lax.fori_loop(0, T, body, carry[...], unroll=8)

    if reverse:
        g = gate_ref[0]
        gelu = 0.5 * g * (1.0 + jnp.tanh(math.sqrt(2.0 / math.pi) * (g + 0.044715 * g * g * g)))
        o_ref[0] = ((hf_ref[0] + h_s[...]) * gelu).astype(BF16)


def _lru(proj, hf, conv_w, conv_b, wcat, ba, bx, lam, *, reverse, T=512):
    B, L, _ = proj.shape
    T = min(T, L)
    nt = L // T
    hb = T // HALO
    nh = L // HALO

    def tile(i):
        return (nt - 1 - i) if reverse else i

    x_spec = pl.BlockSpec((1, T, D_GROUP), lambda b, i: (b, tile(i), 0))
    prev_spec = pl.BlockSpec((1, HALO, D_GROUP), lambda b, i: (b, jnp.maximum(tile(i) * hb - 1, 0), 0))
    next_spec = pl.BlockSpec((1, HALO, D_GROUP), lambda b, i: (b, jnp.minimum((tile(i) + 1) * hb, nh - 1), 0))
    vec = pl.BlockSpec((1, D_GROUP), lambda b, i: (0, 0))
    in_specs = [x_spec, prev_spec, next_spec,
                pl.BlockSpec((4, D_GROUP), lambda b, i: (0, 0)), vec,
                pl.BlockSpec((N_HEAD_BLOCKS, LANES, 2 * LANES), lambda b, i: (0, 0, 0)), vec, vec, vec]
    args = [proj, proj, proj, conv_w, conv_b.reshape(1, -1), wcat, ba.reshape(1, -1), bx.reshape(1, -1),
            lam.reshape(1, -1)]
    scratch = [pltpu.VMEM((T, D_GROUP), F32), pltpu.VMEM((T, D_GROUP), F32), pltpu.VMEM((1, D_GROUP), F32)]
    if reverse:
        in_specs += [pl.BlockSpec((1, T, D_GROUP), lambda b, i: (b, tile(i), 1)),
                     pl.BlockSpec((1, T, D_GROUP), lambda b, i: (b, tile(i), 0))]
        args += [proj, hf]
        scratch += [pltpu.VMEM((T, D_GROUP), F32)]
        out_dtype = BF16
    else:
        out_dtype = F32
    return pl.pallas_call(
        functools.partial(_lru_kernel, reverse=reverse, T=T, nt=nt),
        grid=(B, nt),
        in_specs=in_specs,
        out_specs=pl.BlockSpec((1, T, D_GROUP), lambda b, i: (b, tile(i), 0)),
        out_shape=jax.ShapeDtypeStruct((B, L, D_GROUP), out_dtype),
        scratch_shapes=scratch,
        compiler_params=_cp(("parallel", "arbitrary")),
    )(*args)


def _rope_kernel(x_ref, c_ref, sa_ref, sb_ref, o_ref, *, sh_a, sh_b, scale):
    c = c_ref[...]
    sa = sa_ref[...]
    sb = sb_ref[...]
    for h in range(N_HEAD_BLOCKS):
        sl = slice(h * LANES, (h + 1) * LANES)
        x = x_ref[0, :, sl]
        y = x * c + pltpu.roll(x, sh_a, 1) * sa + pltpu.roll(x, sh_b, 1) * sb
        o_ref[0, :, sl] = (y * scale).astype(o_ref.dtype)


def _rope(proj, col_block, tabs, *, sh_a, sh_b, scale, T=512):
    B, L, _ = proj.shape
    T = min(T, L)
    tab_spec = pl.BlockSpec((T, LANES), lambda b, i: (i, 0))
    return pl.pallas_call(
        functools.partial(_rope_kernel, sh_a=sh_a, sh_b=sh_b, scale=scale),
        grid=(B, L // T),
        in_specs=[pl.BlockSpec((1, T, D_GROUP), lambda b, i: (b, i, col_block)), tab_spec, tab_spec, tab_spec],
        out_specs=pl.BlockSpec((1, T, D_GROUP), lambda b, i: (b, i, 0)),
        out_shape=jax.ShapeDtypeStruct((B, L, D_GROUP), BF16),
        compiler_params=_cp(("parallel", "parallel")),
    )(proj, *tabs)


def _rope_tables(L, dh, inv_freq):
    lane = np.arange(LANES)
    d = lane % dh
    half = dh // 2
    ang = jnp.arange(L, dtype=F32)[:, None] * inv_freq[d % half][None, :]
    cos = jnp.cos(ang)
    sin = jnp.sin(ang)
    lo = jnp.asarray(d < half)[None, :]
    return cos, jnp.where(lo, -sin, 0.0), jnp.where(lo, 0.0, sin)


VT_ROWS = LANES + 16


def _vt_kernel(v_ref, o_ref):
    T = v_ref.shape[1]
    for h in range(N_HEAD_BLOCKS):
        sl = slice(h * LANES, (h + 1) * LANES)
        o_ref[0, h, 0:LANES, :] = v_ref[0, :, sl].T.astype(BF16)
        o_ref[0, h, LANES:VT_ROWS, :] = jnp.ones((VT_ROWS - LANES, T), BF16)


def _v_transpose(proj, col_block, T=512):
    B, L, _ = proj.shape
    T = min(T, L)
    return pl.pallas_call(
        _vt_kernel,
        grid=(B, L // T),
        in_specs=[pl.BlockSpec((1, T, D_GROUP), lambda b, i: (b, i, col_block))],
        out_specs=pl.BlockSpec((1, N_HEAD_BLOCKS, VT_ROWS, T), lambda b, i: (b, 0, 0, i)),
        out_shape=jax.ShapeDtypeStruct((B, N_HEAD_BLOCKS, VT_ROWS, L), BF16),
        compiler_params=_cp(("parallel", "parallel")),
    )(proj)


def _attn_kernel(q_ref, k_ref, vt_ref, lv_ref, sg_ref, o_ref, q2_s, m_s, acc_s, *, tq, tks, nsub, nk, lam_init):
    ki = pl.program_id(3)

    @pl.when(ki == 0)
    def _():
        q = q_ref[0]
        lane = lax.broadcasted_iota(I32, q.shape, 1)
        zero = jnp.zeros_like(q)
        q2_s[0:tq, :] = jnp.where(lane < DIFF_DH, q, zero)
        q2_s[tq:2 * tq, :] = jnp.where(lane >= DIFF_DH, q, zero)
        m_s[...] = jnp.full_like(m_s, -jnp.inf)
        acc_s[...] = jnp.zeros_like(acc_s)

    def scores(j):
        return _dot_nt(k_ref[0, j * tks:(j + 1) * tks, :], q2_s[...])

    s_next = scores(0)
    for j in range(nsub):
        ks = slice(j * tks, (j + 1) * tks)
        s = s_next
        if j + 1 < nsub:
            s_next = scores(j + 1)
        m_prev = m_s[...]
        m_new = jnp.maximum(m_prev, jnp.max(s, axis=0, keepdims=True))
        p = jnp.exp2(s - m_new).astype(BF16)
        acc_s[...] = jnp.exp2(m_prev - m_new) * acc_s[...] + _dot(vt_ref[0, 0, :, ks], p)
        m_s[...] = m_new

    @pl.when(ki == nk - 1)
    def _():
        o = acc_s[0:LANES, :] / acc_s[LANES:LANES + 1, :]
        lv = lv_ref[...]
        lam = (jnp.exp(jnp.sum(lv[0:1] * lv[1:2], axis=1, keepdims=True))
               - jnp.exp(jnp.sum(lv[2:3] * lv[3:4], axis=1, keepdims=True)) + lam_init)
        od = o[:, 0:tq] - lam * o[:, tq:2 * tq]
        ms = jnp.mean(od * od, axis=0, keepdims=True)
        y = od * lax.rsqrt(ms + EPS) * sg_ref[...] * (1.0 - lam_init)
        o_ref[0] = y.T.astype(BF16)


def _diff_attention(qr, kr, vt, diff_lam, subln, lam_init, tq=512, tk=16384, tks=1024):
    B, L, _ = qr.shape
    tq = min(tq, L)
    tk = min(tk, L)
    tks = min(tks, tk)
    nq, nk = L // tq, L // tk
    return pl.pallas_call(
        functools.partial(_attn_kernel, tq=tq, tks=tks, nsub=tk // tks, nk=nk, lam_init=lam_init),
        grid=(B, N_HEAD_BLOCKS, nq, nk),
        in_specs=[
            pl.BlockSpec((1, tq, LANES), lambda b, h, qi, ki: (b, qi, h)),
            pl.BlockSpec((1, tk, LANES), lambda b, h, qi, ki: (b, ki, h)),
            pl.BlockSpec((1, 1, VT_ROWS, tk), lambda b, h, qi, ki: (b, h, 0, ki)),
            pl.BlockSpec((4, DIFF_DH), lambda b, h, qi, ki: (0, 0)),
            pl.BlockSpec((LANES, 1), lambda b, h, qi, ki: (0, 0)),
        ],
        out_specs=pl.BlockSpec((1, tq, LANES), lambda b, h, qi, ki: (b, qi, h)),
        out_shape=jax.ShapeDtypeStruct((B, L, D_GROUP), BF16),
        scratch_shapes=[pltpu.VMEM((2 * tq, LANES), BF16), pltpu.VMEM((1, 2 * tq), F32),
                        pltpu.VMEM((VT_ROWS, 2 * tq), F32)],
        compiler_params=_cp(("parallel", "parallel", "parallel", "arbitrary")),
    )(qr, kr, vt, diff_lam, subln.reshape(LANES, 1))


def _log_gamma(rho_ref):
    return -_softplus(-rho_ref[0, 0])


def _ret_state_kernel(kf_ref, vf_ref, kb_ref, vb_ref, rf_ref, rb_ref, sf_ref, sb_ref, sf_s, sb_s, *, C):
    n = pl.program_id(2)

    @pl.when(n == 0)
    def _():
        sf_s[...] = jnp.zeros_like(sf_s)
        sb_s[...] = jnp.zeros_like(sb_s)

    sf_ref[0, 0, 0] = sf_s[...].astype(BF16)
    sb_ref[0, 0, 0] = sb_s[...].astype(BF16)
    lgf = _log_gamma(rf_ref)
    lgb = _log_gamma(rb_ref)
    row = lax.broadcasted_iota(I32, (C, LANES), 0).astype(F32)
    kdf = kf_ref[0].astype(F32) * jnp.exp(lgf * (C - 1.0 - row))
    kdb = kb_ref[0].astype(F32) * jnp.exp(lgb * row)
    sf_s[...] = sf_s[...] * jnp.exp(lgf * C) + _dot(kdf.T.astype(BF16), vf_ref[0].astype(BF16))
    sb_s[...] = sb_s[...] * jnp.exp(lgb * C) + _dot(kdb.T.astype(BF16), vb_ref[0].astype(BF16))


def _ret_main_kernel(q_ref, k_ref, v_ref, g_ref, sf_ref, sb_ref, rf_ref, rb_ref, o_ref, *, C):
    lgf = _log_gamma(rf_ref)
    lgb = _log_gamma(rb_ref)
    q = q_ref[0]
    s = _dot_nt(q, k_ref[0])
    r = lax.broadcasted_iota(I32, (C, C), 0)
    c = lax.broadcasted_iota(I32, (C, C), 1)
    dist = (r - c).astype(F32)
    dmat = jnp.exp(jnp.where(r >= c, lgf * dist, -lgb * dist))
    inner = _dot((s * dmat).astype(BF16), v_ref[0].astype(BF16))
    row = lax.broadcasted_iota(I32, (C, LANES), 0).astype(F32)
    qf = q.astype(F32)
    qcat = jnp.concatenate([qf * jnp.exp(lgf * (row + 1.0)), qf * jnp.exp(lgb * (C - row))], axis=1)
    scat = jnp.concatenate([sf_ref[0, 0, 0], sb_ref[0, 0, 0]], axis=0)
    y = inner + _dot(qcat.astype(BF16), scat)
    y = y * lax.rsqrt(jnp.mean(y * y, axis=-1, keepdims=True) + EPS)
    g = g_ref[0]
    o_ref[0] = (y * (g * _sigmoid(g))).astype(BF16)


def _retention(proj, qr, kr, rho, v_blk0, g_blk0, C=512):
    B, L, _ = proj.shape
    H = N_HEAD_BLOCKS
    C = min(C, L)
    N = L // C
    rho4 = rho.reshape(2, H, 1, 1)
    rf_spec = pl.BlockSpec((1, 1, 1, 1), lambda b, h, n: (0, h, 0, 0))
    rb_spec = pl.BlockSpec((1, 1, 1, 1), lambda b, h, n: (1, h, 0, 0))
    st_shape = jax.ShapeDtypeStruct((B, H, N, LANES, LANES), BF16)
    sf, sb = pl.pallas_call(
        functools.partial(_ret_state_kernel, C=C),
        grid=(B, H, N),
        in_specs=[
            pl.BlockSpec((1, C, LANES), lambda b, h, n: (b, n, h)),
            pl.BlockSpec((1, C, LANES), lambda b, h, n: (b, n, v_blk0 + h)),
            pl.BlockSpec((1, C, LANES), lambda b, h, n: (b, N - 1 - n, h)),
            pl.BlockSpec((1, C, LANES), lambda b, h, n: (b, N - 1 - n, v_blk0 + h)),
            rf_spec, rb_spec,
        ],
        out_specs=[pl.BlockSpec((1, 1, 1, LANES, LANES), lambda b, h, n: (b, h, n, 0, 0)),
                   pl.BlockSpec((1, 1, 1, LANES, LANES), lambda b, h, n: (b, h, N - 1 - n, 0, 0))],
        out_shape=[st_shape, st_shape],
        scratch_shapes=[pltpu.VMEM((LANES, LANES), F32), pltpu.VMEM((LANES, LANES), F32)],
        compiler_params=_cp(("parallel", "parallel", "arbitrary")),
    )(kr, proj, kr, proj, rho4, rho4)
    st_spec = pl.BlockSpec((1, 1, 1, LANES, LANES), lambda b, h, n: (b, h, n, 0, 0))
    return pl.pallas_call(
        functools.partial(_ret_main_kernel, C=C),
        grid=(B, H, N),
        in_specs=[
            pl.BlockSpec((1, C, LANES), lambda b, h, n: (b, n, h)),
            pl.BlockSpec((1, C, LANES), lambda b, h, n: (b, n, h)),
            pl.BlockSpec((1, C, LANES), lambda b, h, n: (b, n, v_blk0 + h)),
            pl.BlockSpec((1, C, LANES), lambda b, h, n: (b, n, g_blk0 + h)),
            st_spec, st_spec, rf_spec, rb_spec,
        ],
        out_specs=pl.BlockSpec((1, C, LANES), lambda b, h, n: (b, n, h)),
        out_shape=jax.ShapeDtypeStruct((B, L, D_GROUP), BF16),
        compiler_params=_cp(("parallel", "parallel", "parallel")),
    )(qr, kr, proj, proj, sf, sb, rho4, rho4)


def _dwconv3_kernel(x_ref, prev_ref, next_ref, w_ref, b_ref, o_ref, *, T, nt):
    i = pl.program_id(1)
    x = x_ref[0]
    prev = jnp.where(i == 0, 0.0, prev_ref[0])
    nxt = jnp.where(i == nt - 1, 0.0, next_ref[0])
    xe = jnp.concatenate([prev, x, nxt], axis=0)
    w = w_ref[...]
    o_ref[0] = (b_ref[...] + w[0:1] * xe[HALO - 1:HALO - 1 + T] + w[1:2] * xe[HALO:HALO + T]
                + w[2:3] * xe[HALO + 1:HALO + 1 + T])


def _dwconv3(proj, conv_w, conv_b, T=512):
    B, L, _ = proj.shape
    T = min(T, L)
    nt = L // T
    hb = T // HALO
    nh = L // HALO
    W = 3 * D_GROUP
    return pl.pallas_call(
        functools.partial(_dwconv3_kernel, T=T, nt=nt),
        grid=(B, nt),
        in_specs=[
            pl.BlockSpec((1, T, W), lambda b, i: (b, i, 0)),
            pl.BlockSpec((1, HALO, W), lambda b, i: (b, jnp.maximum(i * hb - 1, 0), 0)),
            pl.BlockSpec((1, HALO, W), lambda b, i: (b, jnp.minimum((i + 1) * hb, nh - 1), 0)),
            pl.BlockSpec((3, W), lambda b, i: (0, 0)),
            pl.BlockSpec((1, W), lambda b, i: (0, 0)),
        ],
        out_specs=pl.BlockSpec((1, T, W), lambda b, i: (b, i, 0)),
        out_shape=jax.ShapeDtypeStruct((B, L, W), F32),
        compiler_params=_cp(("parallel", "parallel")),
    )(proj, proj, proj, conv_w, conv_b.reshape(1, W))


def _filter_mlp_kernel(z_ref, w1_ref, b1_ref, w2_ref, b2_ref, w3_ref, fr_ref, hf_ref, sum_ref, *, TL, L):
    i = pl.program_id(0)
    fr = fr_ref[...]
    h = jnp.sin(fr * (_dot(z_ref[...].astype(BF16), w1_ref[...]) + b1_ref[...]))
    for j in range(2):
        h = jnp.sin(fr * (_dot(h.astype(BF16), w2_ref[j]) + b2_ref[j]))
    hf = _dot(h.astype(BF16), w3_ref[...])
    row = i * TL + lax.broadcasted_iota(I32, (TL, D_GROUP), 0)
    t = row.astype(F32) / (L - 1.0)
    ch = lax.broadcasted_iota(I32, (TL, D_GROUP), 1).astype(F32)
    max_decay = math.log(HY_TARGET) / HY_FAST_DECAY
    min_decay = math.log(HY_TARGET) / HY_SLOW_DECAY
    delta = jnp.abs(min_decay + (max_decay - min_decay) * ch / (D_GROUP - 1.0))
    dec = jnp.exp(-t * delta)
    keep = row < L - 1

    @pl.when(i == 0)
    def _():
        sum_ref[...] = jnp.zeros_like(sum_ref)

    for q in range(4):
        sl = slice(q * D_GROUP, (q + 1) * D_GROUP)
        v = hf[:, sl] * dec
        if q % 2 == 1:
            v = jnp.where(keep, v, 0.0)
        hf_ref[:, sl] = v
        sum_ref[:, sl] += jnp.sum(jnp.abs(v), axis=0, keepdims=True)


def _filter_mlp(L, w1, b1, w2, b2, w3, freq, TL=512):
    TL = min(TL, L)
    t = jnp.linspace(0.0, 1.0, L, dtype=F32)[:, None]
    bands = (HY_EMB - 1) // 2
    wpos = 2.0 * math.pi * jnp.arange(L, dtype=F32)[:, None] / L
    f = jnp.linspace(1e-4, bands - 1, bands, dtype=F32)[None, :]
    z = jnp.concatenate([t, jnp.cos(f * wpos), jnp.sin(f * wpos)], axis=-1)
    z = jnp.pad(z, ((0, 0), (0, LANES - HY_EMB)))
    pf = LANES - HY_FFN
    w1p = jnp.pad(w1, ((0, LANES - HY_EMB), (0, pf))).astype(BF16)
    b1p = jnp.pad(b1, (0, pf)).reshape(1, LANES)
    w2p = jnp.pad(w2, ((0, 0), (0, pf), (0, pf))).astype(BF16)
    b2p = jnp.pad(b2, ((0, 0), (0, pf))).reshape(2, 1, LANES)
    w3p = jnp.pad(w3, ((0, pf), (0, 0))).astype(BF16)
    frp = jnp.pad(freq, (0, pf)).reshape(1, LANES)
    W = 4 * D_GROUP
    full = lambda *shape: pl.BlockSpec(shape, lambda i: (0,) * len(shape))
    return pl.pallas_call(
        functools.partial(_filter_mlp_kernel, TL=TL, L=L),
        grid=(L // TL,),
        in_specs=[pl.BlockSpec((TL, LANES), lambda i: (i, 0)), full(LANES, LANES), full(1, LANES),
                  full(2, LANES, LANES), full(2, 1, LANES), full(LANES, W), full(1, LANES)],
        out_specs=[pl.BlockSpec((TL, W), lambda i: (i, 0)), full(1, W)],
        out_shape=[jax.ShapeDtypeStruct((L, W), F32), jax.ShapeDtypeStruct((1, W), F32)],
        compiler_params=_cp(("arbitrary",)),
    )(z, w1p, b1p, w2p, b2p, w3p, frp)


T2_GROUP = 8
SLAB_PAD = 8


def _dft_tables_h(N):
    N2 = FFT_N2
    N1 = N // N2
    R = N1 // 2
    K1 = R + SLAB_PAD
    k1 = np.arange(K1)
    live = (k1 <= R).astype(np.float64)
    t1 = np.arange(R)
    a1 = 2.0 * math.pi * ((k1[:, None] * t1[None, :]) % N1) / N1
    f1 = np.concatenate([np.cos(a1) * live[:, None], -np.sin(a1) * live[:, None]], axis=0)
    wgt = np.where((k1 == 0) | (k1 == R), 1.0, 2.0) * live
    f1inv = np.concatenate([np.cos(a1) * wgt[:, None], -np.sin(a1) * wgt[:, None]], axis=0).T / N
    k2 = np.arange(N2)
    a2 = 2.0 * math.pi * ((k2[:, None] * k2[None, :]) % N2) / N2
    fr, fi = np.cos(a2), -np.sin(a2)
    at = 2.0 * math.pi * (k2[:, None] * k1[None, :]) / N
    ak = 2.0 * math.pi * (k1[:, None] + N1 * k2[None, :]) / N
    tw = lambda a: jnp.broadcast_to(jnp.asarray(a, F32)[:, :, None], a.shape + (LANES,))
    bf = lambda a: jnp.asarray(a, F32).astype(BF16)
    return dict(N1=N1, N2=N2, R=R, K1=K1, f1=bf(f1), f1inv=bf(f1inv),
                f2=bf(np.block([[fr, -fi], [fi, fr]])), f2inv=bf(np.block([[fr, fi], [-fi, fr]])),
                twa_c=tw(np.cos(at)), twa_s=tw(-np.sin(at)), twb_c=tw(np.cos(at.T)), twb_s=tw(np.sin(at.T)),
                wk_c=tw(np.cos(ak)), wk_s=tw(np.sin(ak)))


def _fft1_kernel(x_ref, f_ref, c_ref, s_ref, o_ref, *, K1, C):
    for j in range(T2_GROUP):
        a = _dot(f_ref[...], x_ref[0, :, j, :].astype(BF16))
        c = c_ref[j]
        s = s_ref[j]
        for cb in range(C // LANES):
            sl = slice(cb * LANES, (cb + 1) * LANES)
            ar = a[0:K1, sl]
            ai = a[K1:2 * K1, sl]
            o_ref[0, 0, j, 0:K1, sl] = ar * c - ai * s
            o_ref[0, 0, j, K1:2 * K1, sl] = ar * s + ai * c


def _fft1(x4, tab, col0, ncol):
    Bx, R, N2, _ = x4.shape
    K1 = tab["K1"]
    C = D_GROUP
    G = T2_GROUP
    return pl.pallas_call(
        functools.partial(_fft1_kernel, K1=K1, C=C),
        grid=(Bx, ncol, N2 // G),
        in_specs=[pl.BlockSpec((1, R, G, C), lambda b, ci, g: (b, 0, g, col0 + ci)),
                  pl.BlockSpec((2 * K1, R), lambda b, ci, g: (0, 0)),
                  pl.BlockSpec((G, K1, LANES), lambda b, ci, g: (g, 0, 0)),
                  pl.BlockSpec((G, K1, LANES), lambda b, ci, g: (g, 0, 0))],
        out_specs=pl.BlockSpec((1, 1, G, 2 * K1, C), lambda b, ci, g: (b, ci, g, 0, 0)),
        out_shape=jax.ShapeDtypeStruct((Bx, ncol, N2, 2 * K1, C), F32),
        compiler_params=_cp(("parallel", "parallel", "parallel")),
    )(x4, tab["f1"], tab["twa_c"], tab["twa_s"])


def _strided_cat(re_ref, im_ref, kk):
    return jnp.concatenate([re_ref[0, 0, :, kk, :], im_ref[0, 0, :, kk, :]], axis=0).astype(BF16)


def _filter_fft2_kernel(pr_ref, pi_ref, mr_ref, mi_ref, f_ref, c_ref, s_ref, inv_ref, o_ref, *, N2):
    for kk in range(T2_GROUP):
        zp = _dot(f_ref[...], _strided_cat(pr_ref, pi_ref, kk))
        zm = _dot(f_ref[...], _strided_cat(mr_ref, mi_ref, kk))
        c = c_ref[kk]
        s = s_ref[kk]
        inv = inv_ref[...]
        for cb in range(zp.shape[1] // LANES):
            sl = slice(cb * LANES, (cb + 1) * LANES)
            zmr, zmi = zm[0:N2, sl], zm[N2:2 * N2, sl]
            o_ref[kk, 0, :, sl] = ((zp[0:N2, sl] + c * zmr + s * zmi) * inv[:, sl]).astype(BF16)
            o_ref[kk, 1, :, sl] = ((zp[N2:2 * N2, sl] + s * zmr - c * zmi) * inv[:, sl]).astype(BF16)


def _filter_fft2(af, tab, inv_sum, Ct=512):
    _, _, N2, _, C = af.shape
    K1 = tab["K1"]
    G = T2_GROUP
    nkg = K1 // G
    nc = C // Ct

    def a_spec(side, im):
        return pl.BlockSpec((1, 1, N2, G, Ct), lambda kg, o, cj: (0, 2 * o + side, 0, im * nkg + kg, cj))

    tw_spec = pl.BlockSpec((G, N2, LANES), lambda kg, o, cj: (kg, 0, 0))
    return pl.pallas_call(
        functools.partial(_filter_fft2_kernel, N2=N2),
        grid=(nkg, 2, nc),
        in_specs=[a_spec(0, 0), a_spec(0, 1), a_spec(1, 0), a_spec(1, 1),
                  pl.BlockSpec((2 * N2, 2 * N2), lambda kg, o, cj: (0, 0)), tw_spec, tw_spec,
                  pl.BlockSpec((1, Ct), lambda kg, o, cj: (0, o * nc + cj))],
        out_specs=pl.BlockSpec((G, 2, N2, Ct), lambda kg, o, cj: (kg, 0, 0, o * nc + cj)),
        out_shape=jax.ShapeDtypeStruct((K1, 2, N2, 2 * C), BF16),
        compiler_params=_cp(("parallel", "parallel", "parallel")),
    )(af, af, af, af, tab["f2"], tab["wk_c"], tab["wk_s"], inv_sum)


def _conv_mid_h_kernel(ar_ref, ai_ref, kf_ref, f_ref, fi_ref, c_ref, s_ref, o_ref, *, N2):
    for kk in range(T2_GROUP):
        z = _dot(f_ref[...], _strided_cat(ar_ref, ai_ref, kk))
        zr, zi = z[0:N2], z[N2:2 * N2]
        kr = kf_ref[kk, 0].astype(F32)
        ki = kf_ref[kk, 1].astype(F32)
        y = jnp.concatenate([zr * kr - zi * ki, zr * ki + zi * kr], axis=0).astype(BF16)
        b = _dot(fi_ref[...], y)
        c = c_ref[kk]
        s = s_ref[kk]
        for cb in range(b.shape[1] // LANES):
            sl = slice(cb * LANES, (cb + 1) * LANES)
            br = b[0:N2, sl]
            bi = b[N2:2 * N2, sl]
            o_ref[0, 0, kk, :, sl] = br * c - bi * s
            o_ref[0, 1, kk, :, sl] = br * s + bi * c


def _conv_mid_h(a5, kf, order, tab, Ct=512):
    B, _, N2, _, C = a5.shape
    K1 = tab["K1"]
    G = T2_GROUP
    nkg = K1 // G
    nc = C // Ct
    sq = pl.BlockSpec((2 * N2, 2 * N2), lambda b, kg, cj: (0, 0))
    tw_spec = pl.BlockSpec((G, N2, LANES), lambda b, kg, cj: (kg, 0, 0))
    return pl.pallas_call(
        functools.partial(_conv_mid_h_kernel, N2=N2),
        grid=(B, nkg, nc),
        in_specs=[pl.BlockSpec((1, 1, N2, G, Ct), lambda b, kg, cj: (b, 0, 0, kg, cj)),
                  pl.BlockSpec((1, 1, N2, G, Ct), lambda b, kg, cj: (b, 0, 0, nkg + kg, cj)),
                  pl.BlockSpec((G, 2, N2, Ct), lambda b, kg, cj: (kg, 0, 0, order * nc + cj)),
                  sq, sq, tw_spec, tw_spec],
        out_specs=pl.BlockSpec((1, 2, G, N2, Ct), lambda b, kg, cj: (b, 0, kg, 0, cj)),
        out_shape=jax.ShapeDtypeStruct((B, 2, K1, N2, C), F32),
        compiler_params=_cp(("parallel", "parallel", "parallel")),
    )(a5, a5, kf, tab["f2"], tab["f2inv"], tab["twb_c"], tab["twb_s"])


def _conv_out_h_kernel(b_ref, f_ref, g_ref, z_ref, bias_ref, o_ref, *, K1):
    for j in range(T2_GROUP):
        bj = jnp.concatenate([b_ref[0, 0, :, j, :], b_ref[0, 1, :, j, :]], axis=0).astype(BF16)
        y = _dot(f_ref[...], bj)
        o_ref[0, :, j, :] = g_ref[0, :, j, :] * (y + z_ref[0, :, j, :] * bias_ref[...])


def _conv_out_h(b5, tab, uc4, g_blk, z4, z_blk, bias):
    B, _, K1, N2, C = b5.shape
    R = tab["R"]
    G = T2_GROUP
    return pl.pallas_call(
        functools.partial(_conv_out_h_kernel, K1=K1),
        grid=(B, N2 // G),
        in_specs=[pl.BlockSpec((1, 2, K1, G, C), lambda b, g: (b, 0, 0, g, 0)),
                  pl.BlockSpec((R, 2 * K1), lambda b, g: (0, 0)),
                  pl.BlockSpec((1, R, G, C), lambda b, g: (b, 0, g, g_blk)),
                  pl.BlockSpec((1, R, G, C), lambda b, g: (b, 0, g, z_blk)),
                  pl.BlockSpec((1, C), lambda b, g: (0, 0))],
        out_specs=pl.BlockSpec((1, R, G, C), lambda b, g: (b, 0, g, 0)),
        out_shape=jax.ShapeDtypeStruct((B, R, N2, C), F32),
        compiler_params=_cp(("parallel", "parallel")),
    )(b5, tab["f1inv"], uc4, z4, bias.reshape(1, C))


def _hyena_h(proj, conv_w, conv_b, w1, b1, w2, b2, w3, freq, bias):
    B, L, _ = proj.shape
    C = D_GROUP
    tab = _dft_tables_h(2 * L)
    N2, R = tab["N2"], tab["R"]
    uc4 = _dwconv3(proj, conv_w, conv_b).reshape(B, R, N2, 3 * C)
    hf, sums = _filter_mlp(L, w1, b1, w2, b2, w3, freq)
    s4 = sums.reshape(2, 2, C)
    inv_sum = (1.0 / (s4[:, 0] + s4[:, 1])).reshape(1, 2 * C)
    af = _fft1(hf.reshape(1, R, N2, 4 * C), tab, 0, 4)
    kf = _filter_fft2(af, tab, inv_sum)
    z4, z_blk = uc4, 0
    for o in range(2):
        a5 = _fft1(z4, tab, z_blk, 1)
        b5 = _conv_mid_h(a5, kf, o, tab)
        z4 = _conv_out_h(b5, tab, uc4, 1 + o, z4, z_blk, bias[o])
        z_blk = 0
    return z4.reshape(B, L, C)


def _router_kernel(x_ref, g_ref, rt_ref, h_ref, aff_ref):
    d = x_ref.shape[1]
    xv = x_ref[...]
    ms = jnp.mean(xv * xv, axis=-1, keepdims=True)
    h = xv * lax.rsqrt(ms + EPS) * g_ref[...]
    h_ref[:, 0:d] = h
    logits = lax.dot_general(rt_ref[...], h, (((1,), (1,)), ((), ())), precision=lax.Precision.HIGHEST,
                             preferred_element_type=F32)
    row = lax.broadcasted_iota(I32, logits.shape, 0)
    logits = jnp.where(row < N_EXPERTS, logits, -jnp.inf)
    e = jnp.exp(logits - jnp.max(logits, axis=0, keepdims=True))
    aff = e / jnp.sum(e, axis=0, keepdims=True)
    aff_ref[...] = aff[0:N_EXPERTS]
    h_ref[:, d:d + LANES] = aff.T


def _router(x, g, router_t, tm=512):
    n, d = x.shape
    rt = jnp.pad(router_t, ((0, LANES - N_EXPERTS), (0, 0)))
    return pl.pallas_call(
        _router_kernel,
        grid=(n // tm,),
        in_specs=[pl.BlockSpec((tm, d), lambda i: (i, 0)), pl.BlockSpec((1, d), lambda i: (0, 0)),
                  pl.BlockSpec((LANES, d), lambda i: (0, 0))],
        out_specs=[pl.BlockSpec((tm, d + LANES), lambda i: (i, 0)), pl.BlockSpec((N_EXPERTS, tm), lambda i: (0, i))],
        out_shape=[jax.ShapeDtypeStruct((n, d + LANES), F32), jax.ShapeDtypeStruct((N_EXPERTS, n), F32)],
        compiler_params=_cp(("parallel",)),
    )(x, g.reshape(1, d), rt)


def _select_kernel(aff_ref, pos_ref, cs_ref, m_s, u_s, *, n, cap, CH):
    E = N_EXPERTS
    bits = pltpu.bitcast(aff_ref[...], I32)

    def search(it, thr):
        cand = thr | jnp.left_shift(jnp.int32(1), 30 - it)
        cnt = jnp.sum((bits >= cand).astype(F32), axis=1, keepdims=True)
        return jnp.where(cnt >= cap, cand, thr)

    thr = lax.fori_loop(0, 31, search, jnp.zeros((E, 1), I32))
    need = cap - jnp.sum((bits > thr).astype(F32), axis=1, keepdims=True)
    r = lax.broadcasted_iota(I32, (CH, CH), 0)
    c = lax.broadcasted_iota(I32, (CH, CH), 1)
    u_s[...] = (r <= c).astype(BF16)

    def chunk_bits(j):
        off = pl.multiple_of(j * CH, CH)
        return off, pltpu.bitcast(aff_ref[:, pl.ds(off, CH)], I32)

    def ties(j, carry):
        off, b = chunk_bits(j)
        eq = (b == thr).astype(F32)
        incl = _dot(eq.astype(BF16), u_s[...]) + carry
        sel = (b > thr) | ((b == thr) & (incl - eq < need))
        m_s[:, pl.ds(off, CH)] = sel.astype(F32)
        return incl[:, CH - 1:CH]

    lax.fori_loop(0, n // CH, ties, jnp.zeros((E, 1), F32))

    def slots(j, carry):
        off, _ = chunk_bits(j)
        m = m_s[:, pl.ds(off, CH)]
        incl = _dot(m.astype(BF16), u_s[...]) + carry
        cs_ref[:, pl.ds(off, CH)] = incl.astype(I32)
        pos_ref[:, pl.ds(off, CH)] = jnp.where(m > 0.0, incl - 1.0, -1.0).astype(I32)
        return incl[:, CH - 1:CH]

    lax.fori_loop(0, n // CH, slots, jnp.zeros((E, 1), F32))


def _select(aff, cap, CH=512):
    E, n = aff.shape
    CH = min(CH, n)
    full = pl.BlockSpec((E, n), lambda: (0, 0))
    return pl.pallas_call(
        functools.partial(_select_kernel, n=n, cap=cap, CH=CH),
        in_specs=[full],
        out_specs=[full, full, full],
        out_shape=[jax.ShapeDtypeStruct((E, n), I32), jax.ShapeDtypeStruct((E, n), I32),
                   jax.ShapeDtypeStruct((E, n), F32)],
        scratch_shapes=[pltpu.VMEM((CH, CH), BF16)],
        compiler_params=_cp(None),
    )(aff)


def _compact_kernel(m_ref, idx_ref, *, cap, RB):
    m = m_ref[0]
    nch = m.shape[0]
    r = lax.broadcasted_iota(I32, (LANES, LANES), 0)
    c = lax.broadcasted_iota(I32, (LANES, LANES), 1)
    local = _dot(m, (r <= c).astype(F32))
    tot = _dot(m, (r >= 0).astype(F32))
    rr = lax.broadcasted_iota(I32, (nch, nch), 0)
    cc = lax.broadcasted_iota(I32, (nch, nch), 1)
    cend = _dot((cc <= rr).astype(F32), tot)
    cend_row = cend.T[0:1, :]
    cstart_row = cend_row - tot.T[0:1, :]
    chunk_id = lax.broadcasted_iota(I32, (RB, nch), 1).astype(F32)

    def block(b, carry):
        j0 = pl.multiple_of(b * RB, RB)
        slot = (j0 + lax.broadcasted_iota(I32, (RB, 1), 0)).astype(F32)
        cstar = jnp.sum((slot >= cend_row).astype(F32), axis=1, keepdims=True)
        oh = chunk_id == cstar
        counts = _dot(jnp.where(oh, 1.0, 0.0), local)
        first = jnp.sum(jnp.where(oh, cstart_row, 0.0), axis=1, keepdims=True)
        within = jnp.sum((counts <= slot - first).astype(F32), axis=1, keepdims=True)
        tok = jnp.broadcast_to(LANES * cstar + within, (RB, LANES)).T
        idx_ref[0, :, pl.ds(j0, RB)] = tok[0:8].astype(I32)
        return carry

    lax.fori_loop(0, cap // RB, block, 0)


def _compact(mask, n, cap, RB=512):
    E = N_EXPERTS
    nch = n // LANES
    m3 = mask.reshape(E, nch, LANES)
    if nch < LANES:
        m3 = jnp.pad(m3, ((0, 0), (0, LANES - nch), (0, 0)))
        nch = LANES
    RB = min(RB, cap)
    idx8 = pl.pallas_call(
        functools.partial(_compact_kernel, cap=cap, RB=RB),
        grid=(E,),
        in_specs=[pl.BlockSpec((1, nch, LANES), lambda e: (e, 0, 0))],
        out_specs=pl.BlockSpec((1, 8, cap), lambda e: (e, 0, 0)),
        out_shape=jax.ShapeDtypeStruct((E, 8, cap), I32),
        compiler_params=_cp(("parallel",)),
    )(m3)
    return idx8[:, 0, :].reshape(-1)


def _expert_kernel(idx_ref, nidx_ref, h_ref, wg_ref, wu_ref, wd_ref, o_ref, xbuf, xb, acc, sems,
                   *, tm, nf, d, nrt, cap):
    e = pl.program_id(0)
    r = pl.program_id(1)
    f = pl.program_id(2)
    q = e * nrt + r
    slot = lax.rem(q, 2)
    last = q == N_EXPERTS * nrt - 1
    per = tm // nf

    def row_copy(t, s, j):
        return pltpu.make_async_copy(h_ref.at[pl.ds(t, 1)], xbuf.at[s, pl.ds(j, 1)], sems.at[s])

    def tile_wait(s):
        pltpu.make_async_copy(h_ref.at[pl.ds(0, tm)], xbuf.at[s, :, :], sems.at[s]).wait()

    @pl.when((q == 0) & (f == 0))
    def _():
        def issue(j, carry):
            row_copy(idx_ref[j], 0, j).start()
            return carry

        lax.fori_loop(0, tm, issue, 0, unroll=8)

    @pl.when(f == 0)
    def _():
        tile_wait(slot)
        xb[...] = xbuf[slot, :, 0:d].astype(BF16)
        acc[...] = jnp.zeros_like(acc)

    same = (r + 1 < nrt) | last
    base = jnp.where(last, r * tm, jnp.minimum((r + 1) * tm, cap - tm))
    for u in range(per):
        j = f * per + u
        t = jnp.where(same, idx_ref[base + j], nidx_ref[j])
        row_copy(t, 1 - slot, j).start()

    x = xb[...]
    g = _dot(x, wg_ref[0, 0])
    u_ = _dot(x, wu_ref[0, 0])
    hid = (g * _sigmoid(g) * u_).astype(BF16)
    acc[...] += _dot(hid, wd_ref[0, 0])

    @pl.when(f == nf - 1)
    def _():
        aff = xbuf[slot, :, d:d + LANES]
        lane = lax.broadcasted_iota(I32, aff.shape, 1)
        gate = jnp.sum(jnp.where(lane == e, aff, 0.0), axis=1, keepdims=True)
        o_ref[0] = (acc[...] * gate).astype(BF16)

    @pl.when(last & (f == nf - 1))
    def _():
        tile_wait(1 - slot)


def _experts(idx_flat, h, wg, wu, wd, layer, cap, tm=1024, tf=512):
    n, dx = h.shape
    _, E, d, dff = wg.shape
    tm = min(tm, cap)
    nf = dff // tf
    nrt = cap // tm
    return pl.pallas_call(
        functools.partial(_expert_kernel, tm=tm, nf=nf, d=d, nrt=nrt, cap=cap),
        grid=(E, nrt, nf),
        in_specs=[pl.BlockSpec((cap,), lambda e, r, f: (e,), memory_space=pltpu.SMEM),
                  pl.BlockSpec((cap,), lambda e, r, f: (jnp.minimum(e + 1, E - 1),), memory_space=pltpu.SMEM),
                  pl.BlockSpec(memory_space=pl.ANY),
                  pl.BlockSpec((1, 1, d, tf), lambda e, r, f: (layer, e, 0, f)),
                  pl.BlockSpec((1, 1, d, tf), lambda e, r, f: (layer, e, 0, f)),
                  pl.BlockSpec((1, 1, tf, d), lambda e, r, f: (layer, e, f, 0))],
        out_specs=pl.BlockSpec((1, tm, d), lambda e, r, f: (e, r, 0)),
        out_shape=jax.ShapeDtypeStruct((E, cap, d), BF16),
        scratch_shapes=[pltpu.VMEM((2, tm, dx), F32), pltpu.VMEM((tm, d), BF16), pltpu.VMEM((tm, d), F32),
                        pltpu.SemaphoreType.DMA((2,))],
        compiler_params=_cp(("arbitrary", "arbitrary", "arbitrary")),
    )(idx_flat, idx_flat, h, wg, wu, wd)


COMBINE_WIN = 128
ROW_ALIGN = 16


def _combine_kernel(st_ref, x_ref, pos_ref, ye_ref, o_ref, buf, sems, buf2, sem2, p_s, *, TT, cap, nt):
    i = pl.program_id(0)
    E = N_EXPERTS
    W = COMBINE_WIN

    def window(tile, e, w):
        s0 = st_ref[tile * E + e]
        lo = lax.shift_left(lax.shift_right_logical(s0, 4), 4) + W * w
        return jnp.minimum(lo, cap - W), lo

    def first_copy(tile, slot, e):
        a, _ = window(tile, e, 0)
        return pltpu.make_async_copy(ye_ref.at[e, pl.ds(pl.multiple_of(a, ROW_ALIGN), W)],
                                     buf.at[slot, pl.ds(e * W, W)], sems.at[slot, e])

    slot = lax.rem(i, 2)

    @pl.when(i == 0)
    def _():
        for e in range(E):
            first_copy(0, 0, e).start()

    @pl.when(i + 1 < nt)
    def _():
        for e in range(E):
            first_copy(i + 1, 1 - slot, e).start()

    lane = lax.broadcasted_iota(I32, (TT, W), 1)

    def onehot(pc, a, lo):
        hit = (pc - a == lane) & (pc >= lo) & (pc < lo + W)
        return jnp.where(hit, 1.0, 0.0).astype(BF16)

    for e in range(E):
        a, lo = window(i, e, 0)
        p_s[:, e * W:(e + 1) * W] = onehot(pos_ref[:, e:e + 1], a, lo)
    for e in range(E):
        first_copy(i, slot, e).wait()
    o_ref[...] = x_ref[...] + _dot(p_s[...], buf[slot])

    for e in range(E):
        _, lo = window(i, e, 0)
        s1 = st_ref[(i + 1) * E + e]
        nwin = lax.shift_right_logical(s1 - lo + (W - 1), 7)

        def extra(w, carry):
            a2, lo2 = window(i, e, w)
            cp = pltpu.make_async_copy(ye_ref.at[e, pl.ds(pl.multiple_of(a2, ROW_ALIGN), W)], buf2, sem2)
            cp.start()
            cp.wait()
            o_ref[...] += _dot(onehot(pos_ref[:, e:e + 1], a2, lo2), buf2[...])
            return carry

        lax.fori_loop(1, nwin, extra, 0)


def _combine(starts, x, pos_t, ye, cap, TT=256):
    n, d = x.shape
    E = N_EXPERTS
    TT = min(TT, n)
    nt = n // TT
    grid_spec = pltpu.PrefetchScalarGridSpec(
        num_scalar_prefetch=1,
        grid=(nt,),
        in_specs=[pl.BlockSpec((TT, d), lambda i, st: (i, 0)),
                  pl.BlockSpec((TT, E), lambda i, st: (i, 0)),
                  pl.BlockSpec(memory_space=pl.ANY)],
        out_specs=pl.BlockSpec((TT, d), lambda i, st: (i, 0)),
        scratch_shapes=[pltpu.VMEM((2, E * COMBINE_WIN, d), BF16), pltpu.SemaphoreType.DMA((2, E)),
                        pltpu.VMEM((COMBINE_WIN, d), BF16), pltpu.SemaphoreType.DMA(()),
                        pltpu.VMEM((TT, E * COMBINE_WIN), BF16)],
    )
    return pl.pallas_call(
        functools.partial(_combine_kernel, TT=TT, cap=cap, nt=nt),
        grid_spec=grid_spec,
        out_shape=jax.ShapeDtypeStruct((n, d), F32),
        compiler_params=_cp(("arbitrary",)),
    )(starts, x, pos_t, ye)


def _moe(x, g, router, wg, wu, wd, layer):
    n, d = x.shape
    E = N_EXPERTS
    cap = EC_CAPACITY * n // E
    TT = min(256, n)
    h, aff = _router(x, g, router.T)
    pos, cs, mask = _select(aff, cap)
    idx = _compact(mask, n, cap)
    ye = _experts(idx, h, wg, wu, wd, layer, cap)
    starts = jnp.concatenate([jnp.zeros((E, 1), I32), cs[:, TT - 1::TT]], axis=1).T.reshape(-1)
    return _combine(starts, x, pos.T, ye, cap, TT=TT)


def _rmsnorm_kernel(x_ref, g_ref, o_ref):
    xv = x_ref[...]
    o_ref[...] = xv * lax.rsqrt(jnp.mean(xv * xv, axis=-1, keepdims=True) + EPS) * g_ref[...]


def _rmsnorm(x, g, tm=512):
    n, d = x.shape
    return pl.pallas_call(
        _rmsnorm_kernel,
        grid=(n // tm,),
        in_specs=[pl.BlockSpec((tm, d), lambda i: (i, 0)), pl.BlockSpec((1, d), lambda i: (0, 0))],
        out_specs=pl.BlockSpec((tm, d), lambda i: (i, 0)),
        out_shape=jax.ShapeDtypeStruct((n, d), F32),
        compiler_params=_cp(("parallel",)),
    )(x, g.reshape(1, d))


def _even_mixer(x, B, L, p):
    proj = _rms_matmul(x, p["norm"], p["w_in"]).reshape(B, L, 5 * D_GROUP)
    hf = _lru(proj, None, p["conv_w"], p["conv_b"], p["wcat"][0], p["ba"][0], p["bx"][0], p["lam"][0],
              reverse=False)
    a_out = _lru(proj, hf, p["conv_w"], p["conv_b"], p["wcat"][1], p["ba"][1], p["bx"][1], p["lam"][1],
                 reverse=True)
    inv_freq = ROPE_THETA ** (-jnp.arange(0, DIFF_DH, 2, dtype=F32) / DIFF_DH)
    tabs = _rope_tables(L, DIFF_DH, inv_freq)
    half = DIFF_DH // 2
    qr = _rope(proj, 2, tabs, sh_a=LANES - half, sh_b=half, scale=DIFF_DH ** -0.5 * math.log2(math.e))
    kr = _rope(proj, 3, tabs, sh_a=LANES - half, sh_b=half, scale=1.0)
    vt = _v_transpose(proj, 4)
    b_out = _diff_attention(qr, kr, vt, p["diff_lam"], p["subln"], p["lam_init"])
    return _out_matmul(a_out.reshape(B * L, D_GROUP), b_out.reshape(B * L, D_GROUP), p["w_out"], x)


def _odd_mixer(x, B, L, p):
    proj = _rms_matmul(x, p["norm"], p["w_in"]).reshape(B, L, 7 * D_GROUP)
    c_out = _hyena_h(proj, p["conv_w"], p["conv_b"], p["w1"], p["b1"], p["w2"], p["b2"], p["w3"], p["freq"],
                   p["bias"])
    inv_freq = 1.0 / (10000.0 ** jnp.linspace(0.0, 1.0, RET_DH // 2, dtype=F32))
    tabs = _rope_tables(L, RET_DH, inv_freq)
    half = RET_DH // 2
    qr = _rope(proj, 3, tabs, sh_a=LANES - half, sh_b=half, scale=1.0)
    kr = _rope(proj, 4, tabs, sh_a=LANES - half, sh_b=half, scale=RET_DH ** -0.5)
    d_out = _retention(proj, qr, kr, p["rho"], 5 * N_HEAD_BLOCKS, 6 * N_HEAD_BLOCKS)
    return _out_matmul(c_out.reshape(B * L, D_GROUP), d_out.reshape(B * L, D_GROUP), p["w_out"], x)


def _trunk(x3, layers, final_norm):
    B, L, D = x3.shape
    x = x3.reshape(B * L, D)
    for kind, mp, ep in layers:
        x = (_even_mixer if kind == "even" else _odd_mixer)(x, B, L, mp)
        x = _moe(x, ep["norm"], ep["router"], ep["wg"], ep["wu"], ep["wd"], ep["layer"])
    return _rmsnorm(x, final_norm).reshape(B, L, D)


def kernel(x_prompt, x_sample, ev_norm, ev_w_in, ev_conv_w, ev_conv_b, ev_rg_wa, ev_rg_ba, ev_rg_wx, ev_rg_bx, ev_rg_lam, ev_diff_lam, ev_subln, ev_w_out, od_norm, od_w_in, od_conv_w, od_conv_b, od_flt_w1, od_flt_b1, od_flt_w2, od_flt_b2, od_flt_w3, od_flt_freq, od_flt_bias, od_ret_rho, od_w_out, moe_norm, moe_router, moe_w_gate, moe_w_up, moe_w_down, final_norm):
    depth = moe_norm.shape[0]
    wg_bf, wu_bf, wd_bf = moe_w_gate.astype(BF16), moe_w_up.astype(BF16), moe_w_down.astype(BF16)
    layers = []
    for layer in range(depth):
        j = layer // 2
        if layer % 2 == 0:
            mp = dict(norm=ev_norm[j], w_in=ev_w_in[j].astype(BF16), conv_w=ev_conv_w[j], conv_b=ev_conv_b[j],
                      wcat=jnp.concatenate([ev_rg_wa[j], ev_rg_wx[j]], axis=-1).astype(BF16),
                      ba=ev_rg_ba[j], bx=ev_rg_bx[j], lam=ev_rg_lam[j], diff_lam=ev_diff_lam[j],
                      subln=ev_subln[j], w_out=ev_w_out[j].astype(BF16),
                      lam_init=0.8 - 0.6 * math.exp(-0.3 * layer))
            kind = "even"
        else:
            mp = dict(norm=od_norm[j], w_in=od_w_in[j].astype(BF16), conv_w=od_conv_w[j], conv_b=od_conv_b[j],
                      w1=od_flt_w1[j], b1=od_flt_b1[j], w2=od_flt_w2[j], b2=od_flt_b2[j], w3=od_flt_w3[j],
                      freq=od_flt_freq[j], bias=od_flt_bias[j], rho=od_ret_rho[j],
                      w_out=od_w_out[j].astype(BF16))
            kind = "odd"
        ep = dict(norm=moe_norm[layer], router=moe_router[layer], wg=wg_bf, wu=wu_bf, wd=wd_bf, layer=layer)
        layers.append((kind, mp, ep))
    return (_trunk(x_prompt, layers, final_norm), _trunk(x_sample, layers, final_norm))
```

```python
import functools
import math

import jax
import jax.numpy as jnp
import numpy as np
from jax import lax
from jax.experimental import pallas as pl
from jax.experimental.pallas import tpu as pltpu

F32 = jnp.float32
BF16 = jnp.bfloat16
I32 = jnp.int32

D_MODEL = 2048
D_GROUP = 1024
LANES = 128
N_HEAD_BLOCKS = D_GROUP // LANES
LRU_C = 8.0
DIFF_DH = 64
ROPE_THETA = 10000.0
RET_DH = 128
HY_EMB = 33
HY_FFN = 64
HY_FAST_DECAY = 0.3
HY_SLOW_DECAY = 1.5
HY_TARGET = 1e-2
N_EXPERTS = 16
EC_CAPACITY = 2
EPS = 1e-6
FFT_N2 = 128
HALO = 8
VMEM_LIMIT_MB = 56


def _cp(sem, vmem_mb=VMEM_LIMIT_MB):
    return pltpu.CompilerParams(dimension_semantics=sem, vmem_limit_bytes=vmem_mb * 1024 * 1024)


def _sigmoid(x):
    return 1.0 / (1.0 + jnp.exp(-x))


def _softplus(x):
    return jnp.maximum(x, 0.0) + jnp.log(1.0 + jnp.exp(-jnp.abs(x)))


def _dot(a, b):
    return jnp.dot(a, b, preferred_element_type=F32)


def _dot_nt(a, b):
    return lax.dot_general(a, b, (((1,), (1,)), ((), ())), preferred_element_type=F32)


def _rms_matmul_kernel(x_ref, g_ref, w_ref, o_ref, h_ref):
    @pl.when(pl.program_id(1) == 0)
    def _():
        xv = x_ref[...]
        ms = jnp.mean(xv * xv, axis=-1, keepdims=True)
        h_ref[...] = (xv * lax.rsqrt(ms + EPS) * g_ref[...]).astype(BF16)

    o_ref[...] = _dot(h_ref[...], w_ref[...])


def _rms_matmul(x, g, w_bf, tm=1024, tn=1024):
    n, d = x.shape
    tm = min(tm, n)
    nout = w_bf.shape[1]
    return pl.pallas_call(
        _rms_matmul_kernel,
        grid=(n // tm, nout // tn),
        in_specs=[
            pl.BlockSpec((tm, d), lambda i, j: (i, 0)),
            pl.BlockSpec((1, d), lambda i, j: (0, 0)),
            pl.BlockSpec((d, tn), lambda i, j: (0, j)),
        ],
        out_specs=pl.BlockSpec((tm, tn), lambda i, j: (i, j)),
        out_shape=jax.ShapeDtypeStruct((n, nout), F32),
        scratch_shapes=[pltpu.VMEM((tm, d), BF16)],
        compiler_params=_cp(("parallel", "arbitrary")),
    )(x, g.reshape(1, d), w_bf)


def _out_matmul_kernel(a_ref, b_ref, wa_ref, wb_ref, x_ref, o_ref):
    o_ref[...] = (x_ref[...] + _dot(a_ref[...].astype(BF16), wa_ref[...])
                  + _dot(b_ref[...].astype(BF16), wb_ref[...]))


def _out_matmul(a, b, w_bf, x, tm=1024, tn=1024):
    n, d = x.shape
    tm = min(tm, n)
    dg = a.shape[1]
    return pl.pallas_call(
        _out_matmul_kernel,
        grid=(n // tm, d // tn),
        in_specs=[
            pl.BlockSpec((tm, dg), lambda i, j: (i, 0)),
            pl.BlockSpec((tm, dg), lambda i, j: (i, 0)),
            pl.BlockSpec((dg, tn), lambda i, j: (0, j)),
            pl.BlockSpec((dg, tn), lambda i, j: (1, j)),
            pl.BlockSpec((tm, tn), lambda i, j: (i, j)),
        ],
        out_specs=pl.BlockSpec((tm, tn), lambda i, j: (i, j)),
        out_shape=jax.ShapeDtypeStruct((n, d), F32),
        compiler_params=_cp(("parallel", "parallel")),
    )(a, b, w_bf, w_bf, x)


def _lru_kernel(*refs, reverse, T, nt):
    if reverse:
        (x_ref, prev_ref, next_ref, cw_ref, cb_ref, w_ref, ba_ref, bx_ref, lam_ref,
         gate_ref, hf_ref, o_ref, a_s, b_s, carry, h_s) = refs
    else:
        (x_ref, prev_ref, next_ref, cw_ref, cb_ref, w_ref, ba_ref, bx_ref, lam_ref,
         o_ref, a_s, b_s, carry) = refs
    i = pl.program_id(1)
    ti = (nt - 1 - i) if reverse else i

    @pl.when(i == 0)
    def _():
        carry[...] = jnp.zeros_like(carry)

    x = x_ref[0]
    prev = jnp.where(ti == 0, 0.0, prev_ref[0])
    nxt = jnp.where(ti == nt - 1, 0.0, next_ref[0])
    xe = jnp.concatenate([prev, x, nxt], axis=0)
    cw = cw_ref[...]
    xc = (cb_ref[...] + cw[0:1] * xe[HALO - 2:HALO - 2 + T] + cw[1:2] * xe[HALO - 1:HALO - 1 + T]
          + cw[2:3] * xe[HALO:HALO + T] + cw[3:4] * xe[HALO + 1:HALO + 1 + T])
    sp = _softplus(-lam_ref[...])
    for blk in range(N_HEAD_BLOCKS):
        sl = slice(blk * LANES, (blk + 1) * LANES)
        xb = xc[:, sl]
        ri = _dot(xb.astype(BF16), w_ref[blk])
        r = _sigmoid(ri[:, :LANES] + ba_ref[:, sl])
        ig = _sigmoid(ri[:, LANES:] + bx_ref[:, sl])
        a = jnp.exp(-LRU_C * r * sp[:, sl])
        a_s[:, sl] = a
        b_s[:, sl] = jnp.sqrt(1.0 - a * a) * (ig * xb)

    dst = h_s if reverse else o_ref.at[0]

    def body(k, h):
        t = (T - 1 - k) if reverse else k
        h = a_s[pl.ds(t, 1), :] * h + b_s[pl.ds(t, 1), :]
        dst[pl.ds(t, 1), :] = h
        return h

    carry[...] = lax.fori_loop(0, T, body, carry[...], unroll=8)

    if reverse:
        g = gate_ref[0]
        gelu = 0.5 * g * (1.0 + jnp.tanh(math.sqrt(2.0 / math.pi) * (g + 0.044715 * g * g * g)))
        o_ref[0] = ((hf_ref[0] + h_s[...]) * gelu).astype(BF16)


def _lru(proj, hf, conv_w, conv_b, wcat, ba, bx, lam, *, reverse, T=512):
    B, L, _ = proj.shape
    T = min(T, L)
    nt = L // T
    hb = T // HALO
    nh = L // HALO

    def tile(i):
        return (nt - 1 - i) if reverse else i

    x_spec = pl.BlockSpec((1, T, D_GROUP), lambda b, i: (b, tile(i), 0))
    prev_spec = pl.BlockSpec((1, HALO, D_GROUP), lambda b, i: (b, jnp.maximum(tile(i) * hb - 1, 0), 0))
    next_spec = pl.BlockSpec((1, HALO, D_GROUP), lambda b, i: (b, jnp.minimum((tile(i) + 1) * hb, nh - 1), 0))
    vec = pl.BlockSpec((1, D_GROUP), lambda b, i: (0, 0))
    in_specs = [x_spec, prev_spec, next_spec,
                pl.BlockSpec((4, D_GROUP), lambda b, i: (0, 0)), vec,
                pl.BlockSpec((N_HEAD_BLOCKS, LANES, 2 * LANES), lambda b, i: (0, 0, 0)), vec, vec, vec]
    args = [proj, proj, proj, conv_w, conv_b.reshape(1, -1), wcat, ba.reshape(1, -1), bx.reshape(1, -1),
            lam.reshape(1, -1)]
    scratch = [pltpu.VMEM((T, D_GROUP), F32), pltpu.VMEM((T, D_GROUP), F32), pltpu.VMEM((1, D_GROUP), F32)]
    if reverse:
        in_specs += [pl.BlockSpec((1, T, D_GROUP), lambda b, i: (b, tile(i), 1)),
                     pl.BlockSpec((1, T, D_GROUP), lambda b, i: (b, tile(i), 0))]
        args += [proj, hf]
        scratch += [pltpu.VMEM((T, D_GROUP), F32)]
        out_dtype = BF16
    else:
        out_dtype = F32
    return pl.pallas_call(
        functools.partial(_lru_kernel, reverse=reverse, T=T, nt=nt),
        grid=(B, nt),
        in_specs=in_specs,
        out_specs=pl.BlockSpec((1, T, D_GROUP), lambda b, i: (b, tile(i), 0)),
        out_shape=jax.ShapeDtypeStruct((B, L, D_GROUP), out_dtype),
        scratch_shapes=scratch,
        compiler_params=_cp(("parallel", "arbitrary")),
    )(*args)


def _rope_kernel(x_ref, c_ref, sa_ref, sb_ref, o_ref, *, sh_a, sh_b, scale):
    c = c_ref[...]
    sa = sa_ref[...]
    sb = sb_ref[...]
    for h in range(N_HEAD_BLOCKS):
        sl = slice(h * LANES, (h + 1) * LANES)
        x = x_ref[0, :, sl]
        y = x * c + pltpu.roll(x, sh_a, 1) * sa + pltpu.roll(x, sh_b, 1) * sb
        o_ref[0, :, sl] = (y * scale).astype(o_ref.dtype)


def _rope(proj, col_block, tabs, *, sh_a, sh_b, scale, T=512):
    B, L, _ = proj.shape
    T = min(T, L)
    tab_spec = pl.BlockSpec((T, LANES), lambda b, i: (i, 0))
    return pl.pallas_call(
        functools.partial(_rope_kernel, sh_a=sh_a, sh_b=sh_b, scale=scale),
        grid=(B, L // T),
        in_specs=[pl.BlockSpec((1, T, D_GROUP), lambda b, i: (b, i, col_block)), tab_spec, tab_spec, tab_spec],
        out_specs=pl.BlockSpec((1, T, D_GROUP), lambda b, i: (b, i, 0)),
        out_shape=jax.ShapeDtypeStruct((B, L, D_GROUP), BF16),
        compiler_params=_cp(("parallel", "parallel")),
    )(proj, *tabs)


def _rope_tables(L, dh, inv_freq):
    lane = np.arange(LANES)
    d = lane % dh
    half = dh // 2
    ang = jnp.arange(L, dtype=F32)[:, None] * inv_freq[d % half][None, :]
    cos = jnp.cos(ang)
    sin = jnp.sin(ang)
    lo = jnp.asarray(d < half)[None, :]
    return cos, jnp.where(lo, -sin, 0.0), jnp.where(lo, 0.0, sin)


VT_ROWS = LANES + 16
MAX_EAGER_JUMP = 60.0


def _vt_kernel(v_ref, o_ref):
    T = v_ref.shape[1]
    for h in range(N_HEAD_BLOCKS):
        sl = slice(h * LANES, (h + 1) * LANES)
        o_ref[0, h, 0:LANES, :] = v_ref[0, :, sl].T.astype(BF16)
        o_ref[0, h, LANES:VT_ROWS, :] = jnp.ones((VT_ROWS - LANES, T), BF16)


def _v_transpose(proj, col_block, T=512):
    B, L, _ = proj.shape
    T = min(T, L)
    return pl.pallas_call(
        _vt_kernel,
        grid=(B, L // T),
        in_specs=[pl.BlockSpec((1, T, D_GROUP), lambda b, i: (b, i, col_block))],
        out_specs=pl.BlockSpec((1, N_HEAD_BLOCKS, VT_ROWS, T), lambda b, i: (b, 0, 0, i)),
        out_shape=jax.ShapeDtypeStruct((B, N_HEAD_BLOCKS, VT_ROWS, L), BF16),
        compiler_params=_cp(("parallel", "parallel")),
    )(proj)


def _attn_kernel(q_ref, k_ref, vt_ref, lv_ref, sg_ref, o_ref, q2_s, m_s, acc_s, *, tq, tks, nsub, nk, lam_init):
    ki = pl.program_id(3)

    @pl.when(ki == 0)
    def _():
        q = q_ref[0]
        lane = lax.broadcasted_iota(I32, q.shape, 1)
        zero = jnp.zeros_like(q)
        q2_s[0:tq, :] = jnp.where(lane < DIFF_DH, q, zero)
        q2_s[tq:2 * tq, :] = jnp.where(lane >= DIFF_DH, q, zero)
        m_s[...] = jnp.full_like(m_s, -jnp.inf)
        acc_s[...] = jnp.zeros_like(acc_s)

    def scores(j):
        return _dot_nt(k_ref[0, j * tks:(j + 1) * tks, :], q2_s[...])

    def softmax_pass(eager):
        m_in = m_s[...]
        acc_in = acc_s[...]
        jump = jnp.zeros_like(m_in)
        s_next = scores(0)
        for j in range(nsub):
            ks = slice(j * tks, (j + 1) * tks)
            s = s_next
            if j + 1 < nsub:
                s_next = scores(j + 1)
            m_prev = m_in if j == 0 else m_s[...]
            acc_prev = acc_in if j == 0 else acc_s[...]
            m_new = jnp.maximum(m_prev, jnp.max(s, axis=0, keepdims=True))
            if eager and j > 0:
                p = jnp.exp2(s - m_prev).astype(BF16)
                acc_s[...] = jnp.exp2(m_prev - m_new) * (acc_prev + _dot(vt_ref[0, 0, :, ks], p))
                jump = jnp.maximum(jump, m_new - m_prev)
            else:
                p = jnp.exp2(s - m_new).astype(BF16)
                acc_s[...] = jnp.exp2(m_prev - m_new) * acc_prev + _dot(vt_ref[0, 0, :, ks], p)
            m_s[...] = m_new
        return m_in, acc_in, jnp.max(jump)

    m_in, acc_in, jump = softmax_pass(True)

    @pl.when(jump > MAX_EAGER_JUMP)
    def _():
        m_s[...] = m_in
        acc_s[...] = acc_in
        softmax_pass(False)

    @pl.when(ki == nk - 1)
    def _():
        o = acc_s[0:LANES, :] / acc_s[LANES:LANES + 1, :]
        lv = lv_ref[...]
        lam = (jnp.exp(jnp.sum(lv[0:1] * lv[1:2], axis=1, keepdims=True))
               - jnp.exp(jnp.sum(lv[2:3] * lv[3:4], axis=1, keepdims=True)) + lam_init)
        od = o[:, 0:tq] - lam * o[:, tq:2 * tq]
        ms = jnp.mean(od * od, axis=0, keepdims=True)
        y = od * lax.rsqrt(ms + EPS) * sg_ref[...] * (1.0 - lam_init)
        o_ref[0] = y.T.astype(BF16)


def _diff_attention(qr, kr, vt, diff_lam, subln, lam_init, tq=512, tk=16384, tks=1024):
    B, L, _ = qr.shape
    tq = min(tq, L)
    tk = min(tk, L)
    tks = min(tks, tk)
    nq, nk = L // tq, L // tk
    return pl.pallas_call(
        functools.partial(_attn_kernel, tq=tq, tks=tks, nsub=tk // tks, nk=nk, lam_init=lam_init),
        grid=(B, N_HEAD_BLOCKS, nq, nk),
        in_specs=[
            pl.BlockSpec((1, tq, LANES), lambda b, h, qi, ki: (b, qi, h)),
            pl.BlockSpec((1, tk, LANES), lambda b, h, qi, ki: (b, ki, h)),
            pl.BlockSpec((1, 1, VT_ROWS, tk), lambda b, h, qi, ki: (b, h, 0, ki)),
            pl.BlockSpec((4, DIFF_DH), lambda b, h, qi, ki: (0, 0)),
            pl.BlockSpec((LANES, 1), lambda b, h, qi, ki: (0, 0)),
        ],
        out_specs=pl.BlockSpec((1, tq, LANES), lambda b, h, qi, ki: (b, qi, h)),
        out_shape=jax.ShapeDtypeStruct((B, L, D_GROUP), BF16),
        scratch_shapes=[pltpu.VMEM((2 * tq, LANES), BF16), pltpu.VMEM((1, 2 * tq), F32),
                        pltpu.VMEM((VT_ROWS, 2 * tq), F32)],
        compiler_params=_cp(("parallel", "parallel", "parallel", "arbitrary")),
    )(qr, kr, vt, diff_lam, subln.reshape(LANES, 1))


def _log_gamma(rho_ref):
    return -_softplus(-rho_ref[0, 0])


def _ret_state_kernel(kf_ref, vf_ref, kb_ref, vb_ref, rf_ref, rb_ref, sf_ref, sb_ref, sf_s, sb_s, *, C):
    n = pl.program_id(2)

    @pl.when(n == 0)
    def _():
        sf_s[...] = jnp.zeros_like(sf_s)
        sb_s[...] = jnp.zeros_like(sb_s)

    sf_ref[0, 0, 0] = sf_s[...].astype(BF16)
    sb_ref[0, 0, 0] = sb_s[...].astype(BF16)
    lgf = _log_gamma(rf_ref)
    lgb = _log_gamma(rb_ref)
    row = lax.broadcasted_iota(I32, (C, LANES), 0).astype(F32)
    kdf = kf_ref[0].astype(F32) * jnp.exp(lgf * (C - 1.0 - row))
    kdb = kb_ref[0].astype(F32) * jnp.exp(lgb * row)
    sf_s[...] = sf_s[...] * jnp.exp(lgf * C) + _dot(kdf.T.astype(BF16), vf_ref[0].astype(BF16))
    sb_s[...] = sb_s[...] * jnp.exp(lgb * C) + _dot(kdb.T.astype(BF16), vb_ref[0].astype(BF16))


def _ret_main_kernel(q_ref, k_ref, v_ref, g_ref, sf_ref, sb_ref, rf_ref, rb_ref, o_ref, *, C):
    lgf = _log_gamma(rf_ref)
    lgb = _log_gamma(rb_ref)
    q = q_ref[0]
    s = _dot_nt(q, k_ref[0])
    r = lax.broadcasted_iota(I32, (C, C), 0)
    c = lax.broadcasted_iota(I32, (C, C), 1)
    dist = (r - c).astype(F32)
    dmat = jnp.exp(jnp.where(r >= c, lgf * dist, -lgb * dist))
    inner = _dot((s * dmat).astype(BF16), v_ref[0].astype(BF16))
    row = lax.broadcasted_iota(I32, (C, LANES), 0).astype(F32)
    qf = q.astype(F32)
    qcat = jnp.concatenate([qf * jnp.exp(lgf * (row + 1.0)), qf * jnp.exp(lgb * (C - row))], axis=1)
    scat = jnp.concatenate([sf_ref[0, 0, 0], sb_ref[0, 0, 0]], axis=0)
    y = inner + _dot(qcat.astype(BF16), scat)
    y = y * lax.rsqrt(jnp.mean(y * y, axis=-1, keepdims=True) + EPS)
    g = g_ref[0]
    o_ref[0] = (y * (g * _sigmoid(g))).astype(BF16)


def _retention(proj, qr, kr, rho, v_blk0, g_blk0, C=512):
    B, L, _ = proj.shape
    H = N_HEAD_BLOCKS
    C = min(C, L)
    N = L // C
    rho4 = rho.reshape(2, H, 1, 1)
    rf_spec = pl.BlockSpec((1, 1, 1, 1), lambda b, h, n: (0, h, 0, 0))
    rb_spec = pl.BlockSpec((1, 1, 1, 1), lambda b, h, n: (1, h, 0, 0))
    st_shape = jax.ShapeDtypeStruct((B, H, N, LANES, LANES), BF16)
    sf, sb = pl.pallas_call(
        functools.partial(_ret_state_kernel, C=C),
        grid=(B, H, N),
        in_specs=[
            pl.BlockSpec((1, C, LANES), lambda b, h, n: (b, n, h)),
            pl.BlockSpec((1, C, LANES), lambda b, h, n: (b, n, v_blk0 + h)),
            pl.BlockSpec((1, C, LANES), lambda b, h, n: (b, N - 1 - n, h)),
            pl.BlockSpec((1, C, LANES), lambda b, h, n: (b, N - 1 - n, v_blk0 + h)),
            rf_spec, rb_spec,
        ],
        out_specs=[pl.BlockSpec((1, 1, 1, LANES, LANES), lambda b, h, n: (b, h, n, 0, 0)),
                   pl.BlockSpec((1, 1, 1, LANES, LANES), lambda b, h, n: (b, h, N - 1 - n, 0, 0))],
        out_shape=[st_shape, st_shape],
        scratch_shapes=[pltpu.VMEM((LANES, LANES), F32), pltpu.VMEM((LANES, LANES), F32)],
        compiler_params=_cp(("parallel", "parallel", "arbitrary")),
    )(kr, proj, kr, proj, rho4, rho4)
    st_spec = pl.BlockSpec((1, 1, 1, LANES, LANES), lambda b, h, n: (b, h, n, 0, 0))
    return pl.pallas_call(
        functools.partial(_ret_main_kernel, C=C),
        grid=(B, H, N),
        in_specs=[
            pl.BlockSpec((1, C, LANES), lambda b, h, n: (b, n, h)),
            pl.BlockSpec((1, C, LANES), lambda b, h, n: (b, n, h)),
            pl.BlockSpec((1, C, LANES), lambda b, h, n: (b, n, v_blk0 + h)),
            pl.BlockSpec((1, C, LANES), lambda b, h, n: (b, n, g_blk0 + h)),
            st_spec, st_spec, rf_spec, rb_spec,
        ],
        out_specs=pl.BlockSpec((1, C, LANES), lambda b, h, n: (b, n, h)),
        out_shape=jax.ShapeDtypeStruct((B, L, D_GROUP), BF16),
        compiler_params=_cp(("parallel", "parallel", "parallel")),
    )(qr, kr, proj, proj, sf, sb, rho4, rho4)


def _dwconv3_kernel(x_ref, prev_ref, next_ref, w_ref, b_ref, o_ref, *, T, nt):
    i = pl.program_id(1)
    x = x_ref[0]
    prev = jnp.where(i == 0, 0.0, prev_ref[0])
    nxt = jnp.where(i == nt - 1, 0.0, next_ref[0])
    xe = jnp.concatenate([prev, x, nxt], axis=0)
    w = w_ref[...]
    o_ref[0] = (b_ref[...] + w[0:1] * xe[HALO - 1:HALO - 1 + T] + w[1:2] * xe[HALO:HALO + T]
                + w[2:3] * xe[HALO + 1:HALO + 1 + T])


def _dwconv3(proj, conv_w, conv_b, T=512):
    B, L, _ = proj.shape
    T = min(T, L)
    nt = L // T
    hb = T // HALO
    nh = L // HALO
    W = 3 * D_GROUP
    return pl.pallas_call(
        functools.partial(_dwconv3_kernel, T=T, nt=nt),
        grid=(B, nt),
        in_specs=[
            pl.BlockSpec((1, T, W), lambda b, i: (b, i, 0)),
            pl.BlockSpec((1, HALO, W), lambda b, i: (b, jnp.maximum(i * hb - 1, 0), 0)),
            pl.BlockSpec((1, HALO, W), lambda b, i: (b, jnp.minimum((i + 1) * hb, nh - 1), 0)),
            pl.BlockSpec((3, W), lambda b, i: (0, 0)),
            pl.BlockSpec((1, W), lambda b, i: (0, 0)),
        ],
        out_specs=pl.BlockSpec((1, T, W), lambda b, i: (b, i, 0)),
        out_shape=jax.ShapeDtypeStruct((B, L, W), F32),
        compiler_params=_cp(("parallel", "parallel")),
    )(proj, proj, proj, conv_w, conv_b.reshape(1, W))


def _filter_mlp_kernel(z_ref, w1_ref, b1_ref, w2_ref, b2_ref, w3_ref, fr_ref, hf_ref, sum_ref, *, TL, L):
    i = pl.program_id(0)
    fr = fr_ref[...]
    h = jnp.sin(fr * (_dot(z_ref[...].astype(BF16), w1_ref[...]) + b1_ref[...]))
    for j in range(2):
        h = jnp.sin(fr * (_dot(h.astype(BF16), w2_ref[j]) + b2_ref[j]))
    hf = _dot(h.astype(BF16), w3_ref[...])
    row = i * TL + lax.broadcasted_iota(I32, (TL, D_GROUP), 0)
    t = row.astype(F32) / (L - 1.0)
    ch = lax.broadcasted_iota(I32, (TL, D_GROUP), 1).astype(F32)
    max_decay = math.log(HY_TARGET) / HY_FAST_DECAY
    min_decay = math.log(HY_TARGET) / HY_SLOW_DECAY
    delta = jnp.abs(min_decay + (max_decay - min_decay) * ch / (D_GROUP - 1.0))
    dec = jnp.exp(-t * delta)
    keep = row < L - 1

    @pl.when(i == 0)
    def _():
        sum_ref[...] = jnp.zeros_like(sum_ref)

    for q in range(4):
        sl = slice(q * D_GROUP, (q + 1) * D_GROUP)
        v = hf[:, sl] * dec
        if q % 2 == 1:
            v = jnp.where(keep, v, 0.0)
        hf_ref[:, sl] = v
        sum_ref[:, sl] += jnp.sum(jnp.abs(v), axis=0, keepdims=True)


def _filter_mlp(L, w1, b1, w2, b2, w3, freq, TL=512):
    TL = min(TL, L)
    t = jnp.linspace(0.0, 1.0, L, dtype=F32)[:, None]
    bands = (HY_EMB - 1) // 2
    wpos = 2.0 * math.pi * jnp.arange(L, dtype=F32)[:, None] / L
    f = jnp.linspace(1e-4, bands - 1, bands, dtype=F32)[None, :]
    z = jnp.concatenate([t, jnp.cos(f * wpos), jnp.sin(f * wpos)], axis=-1)
    z = jnp.pad(z, ((0, 0), (0, LANES - HY_EMB)))
    pf = LANES - HY_FFN
    w1p = jnp.pad(w1, ((0, LANES - HY_EMB), (0, pf))).astype(BF16)
    b1p = jnp.pad(b1, (0, pf)).reshape(1, LANES)
    w2p = jnp.pad(w2, ((0, 0), (0, pf), (0, pf))).astype(BF16)
    b2p = jnp.pad(b2, ((0, 0), (0, pf))).reshape(2, 1, LANES)
    w3p = jnp.pad(w3, ((0, pf), (0, 0))).astype(BF16)
    frp = jnp.pad(freq, (0, pf)).reshape(1, LANES)
    W = 4 * D_GROUP
    full = lambda *shape: pl.BlockSpec(shape, lambda i: (0,) * len(shape))
    return pl.pallas_call(
        functools.partial(_filter_mlp_kernel, TL=TL, L=L),
        grid=(L // TL,),
        in_specs=[pl.BlockSpec((TL, LANES), lambda i: (i, 0)), full(LANES, LANES), full(1, LANES),
                  full(2, LANES, LANES), full(2, 1, LANES), full(LANES, W), full(1, LANES)],
        out_specs=[pl.BlockSpec((TL, W), lambda i: (i, 0)), full(1, W)],
        out_shape=[jax.ShapeDtypeStruct((L, W), F32), jax.ShapeDtypeStruct((1, W), F32)],
        compiler_params=_cp(("arbitrary",)),
    )(z, w1p, b1p, w2p, b2p, w3p, frp)


T2_GROUP = 8
SLAB_PAD = 8


def _dft_tables_h(N):
    N2 = FFT_N2
    N1 = N // N2
    R = N1 // 2
    K1 = R + SLAB_PAD
    k1 = np.arange(K1)
    live = (k1 <= R).astype(np.float64)
    t1 = np.arange(R)
    a1 = 2.0 * math.pi * ((k1[:, None] * t1[None, :]) % N1) / N1
    f1 = np.concatenate([np.cos(a1) * live[:, None], -np.sin(a1) * live[:, None]], axis=0)
    wgt = np.where((k1 == 0) | (k1 == R), 1.0, 2.0) * live
    f1inv = np.concatenate([np.cos(a1) * wgt[:, None], -np.sin(a1) * wgt[:, None]], axis=0).T / N
    k2 = np.arange(N2)
    a2 = 2.0 * math.pi * ((k2[:, None] * k2[None, :]) % N2) / N2
    fr, fi = np.cos(a2), -np.sin(a2)
    at = 2.0 * math.pi * (k2[:, None] * k1[None, :]) / N
    ak = 2.0 * math.pi * (k1[:, None] + N1 * k2[None, :]) / N
    tw = lambda a: jnp.broadcast_to(jnp.asarray(a, F32)[:, :, None], a.shape + (LANES,))
    bf = lambda a: jnp.asarray(a, F32).astype(BF16)
    return dict(N1=N1, N2=N2, R=R, K1=K1, f1=bf(f1), f1inv=bf(f1inv),
                f2=bf(np.block([[fr, -fi], [fi, fr]])), f2inv=bf(np.block([[fr, fi], [-fi, fr]])),
                twa_c=tw(np.cos(at)), twa_s=tw(-np.sin(at)), twb_c=tw(np.cos(at.T)), twb_s=tw(np.sin(at.T)),
                wk_c=tw(np.cos(ak)), wk_s=tw(np.sin(ak)))


def _fft1_kernel(x_ref, f_ref, c_ref, s_ref, o_ref, *, K1, C):
    for j in range(T2_GROUP):
        a = _dot(f_ref[...], x_ref[0, :, j, :].astype(BF16))
        c = c_ref[j]
        s = s_ref[j]
        for cb in range(C // LANES):
            sl = slice(cb * LANES, (cb + 1) * LANES)
            ar = a[0:K1, sl]
            ai = a[K1:2 * K1, sl]
            o_ref[0, 0, j, 0:K1, sl] = ar * c - ai * s
            o_ref[0, 0, j, K1:2 * K1, sl] = ar * s + ai * c


def _fft1(x4, tab, col0, ncol):
    Bx, R, N2, _ = x4.shape
    K1 = tab["K1"]
    C = D_GROUP
    G = T2_GROUP
    return pl.pallas_call(
        functools.partial(_fft1_kernel, K1=K1, C=C),
        grid=(Bx, ncol, N2 // G),
        in_specs=[pl.BlockSpec((1, R, G, C), lambda b, ci, g: (b, 0, g, col0 + ci)),
                  pl.BlockSpec((2 * K1, R), lambda b, ci, g: (0, 0)),
                  pl.BlockSpec((G, K1, LANES), lambda b, ci, g: (g, 0, 0)),
                  pl.BlockSpec((G, K1, LANES), lambda b, ci, g: (g, 0, 0))],
        out_specs=pl.BlockSpec((1, 1, G, 2 * K1, C), lambda b, ci, g: (b, ci, g, 0, 0)),
        out_shape=jax.ShapeDtypeStruct((Bx, ncol, N2, 2 * K1, C), F32),
        compiler_params=_cp(("parallel", "parallel", "parallel")),
    )(x4, tab["f1"], tab["twa_c"], tab["twa_s"])


def _strided_cat(re_ref, im_ref, kk):
    return jnp.concatenate([re_ref[0, 0, :, kk, :], im_ref[0, 0, :, kk, :]], axis=0).astype(BF16)


def _filter_fft2_kernel(pr_ref, pi_ref, mr_ref, mi_ref, f_ref, c_ref, s_ref, inv_ref, o_ref, *, N2):
    for kk in range(T2_GROUP):
        zp = _dot(f_ref[...], _strided_cat(pr_ref, pi_ref, kk))
        zm = _dot(f_ref[...], _strided_cat(mr_ref, mi_ref, kk))
        c = c_ref[kk]
        s = s_ref[kk]
        inv = inv_ref[...]
        for cb in range(zp.shape[1] // LANES):
            sl = slice(cb * LANES, (cb + 1) * LANES)
            zmr, zmi = zm[0:N2, sl], zm[N2:2 * N2, sl]
            o_ref[kk, 0, :, sl] = ((zp[0:N2, sl] + c * zmr + s * zmi) * inv[:, sl]).astype(BF16)
            o_ref[kk, 1, :, sl] = ((zp[N2:2 * N2, sl] + s * zmr - c * zmi) * inv[:, sl]).astype(BF16)


def _filter_fft2(af, tab, inv_sum, Ct=512):
    _, _, N2, _, C = af.shape
    K1 = tab["K1"]
    G = T2_GROUP
    nkg = K1 // G
    nc = C // Ct

    def a_spec(side, im):
        return pl.BlockSpec((1, 1, N2, G, Ct), lambda kg, o, cj: (0, 2 * o + side, 0, im * nkg + kg, cj))

    tw_spec = pl.BlockSpec((G, N2, LANES), lambda kg, o, cj: (kg, 0, 0))
    return pl.pallas_call(
        functools.partial(_filter_fft2_kernel, N2=N2),
        grid=(nkg, 2, nc),
        in_specs=[a_spec(0, 0), a_spec(0, 1), a_spec(1, 0), a_spec(1, 1),
                  pl.BlockSpec((2 * N2, 2 * N2), lambda kg, o, cj: (0, 0)), tw_spec, tw_spec,
                  pl.BlockSpec((1, Ct), lambda kg, o, cj: (0, o * nc + cj))],
        out_specs=pl.BlockSpec((G, 2, N2, Ct), lambda kg, o, cj: (kg, 0, 0, o * nc + cj)),
        out_shape=jax.ShapeDtypeStruct((K1, 2, N2, 2 * C), BF16),
        compiler_params=_cp(("parallel", "parallel", "parallel")),
    )(af, af, af, af, tab["f2"], tab["wk_c"], tab["wk_s"], inv_sum)


def _conv_mid_h_kernel(ar_ref, ai_ref, kf_ref, f_ref, fi_ref, c_ref, s_ref, o_ref, *, N2):
    for kk in range(T2_GROUP):
        z = _dot(f_ref[...], _strided_cat(ar_ref, ai_ref, kk))
        zr, zi = z[0:N2], z[N2:2 * N2]
        kr = kf_ref[kk, 0].astype(F32)
        ki = kf_ref[kk, 1].astype(F32)
        y = jnp.concatenate([zr * kr - zi * ki, zr * ki + zi * kr], axis=0).astype(BF16)
        b = _dot(fi_ref[...], y)
        c = c_ref[kk]
        s = s_ref[kk]
        for cb in range(b.shape[1] // LANES):
            sl = slice(cb * LANES, (cb + 1) * LANES)
            br = b[0:N2, sl]
            bi = b[N2:2 * N2, sl]
            o_ref[0, 0, kk, :, sl] = br * c - bi * s
            o_ref[0, 1, kk, :, sl] = br * s + bi * c


def _conv_mid_h(a5, kf, order, tab, Ct=512):
    B, _, N2, _, C = a5.shape
    K1 = tab["K1"]
    G = T2_GROUP
    nkg = K1 // G
    nc = C // Ct
    sq = pl.BlockSpec((2 * N2, 2 * N2), lambda b, kg, cj: (0, 0))
    tw_spec = pl.BlockSpec((G, N2, LANES), lambda b, kg, cj: (kg, 0, 0))
    return pl.pallas_call(
        functools.partial(_conv_mid_h_kernel, N2=N2),
        grid=(B, nkg, nc),
        in_specs=[pl.BlockSpec((1, 1, N2, G, Ct), lambda b, kg, cj: (b, 0, 0, kg, cj)),
                  pl.BlockSpec((1, 1, N2, G, Ct), lambda b, kg, cj: (b, 0, 0, nkg + kg, cj)),
                  pl.BlockSpec((G, 2, N2, Ct), lambda b, kg, cj: (kg, 0, 0, order * nc + cj)),
                  sq, sq, tw_spec, tw_spec],
        out_specs=pl.BlockSpec((1, 2, G, N2, Ct), lambda b, kg, cj: (b, 0, kg, 0, cj)),
        out_shape=jax.ShapeDtypeStruct((B, 2, K1, N2, C), F32),
        compiler_params=_cp(("parallel", "parallel", "parallel")),
    )(a5, a5, kf, tab["f2"], tab["f2inv"], tab["twb_c"], tab["twb_s"])


def _conv_out_h_kernel(b_ref, f_ref, g_ref, z_ref, bias_ref, o_ref, *, K1):
    for j in range(T2_GROUP):
        bj = jnp.concatenate([b_ref[0, 0, :, j, :], b_ref[0, 1, :, j, :]], axis=0).astype(BF16)
        y = _dot(f_ref[...], bj)
        o_ref[0, :, j, :] = g_ref[0, :, j, :] * (y + z_ref[0, :, j, :] * bias_ref[...])


def _conv_out_h(b5, tab, uc4, g_blk, z4, z_blk, bias):
    B, _, K1, N2, C = b5.shape
    R = tab["R"]
    G = T2_GROUP
    return pl.pallas_call(
        functools.partial(_conv_out_h_kernel, K1=K1),
        grid=(B, N2 // G),
        in_specs=[pl.BlockSpec((1, 2, K1, G, C), lambda b, g: (b, 0, 0, g, 0)),
                  pl.BlockSpec((R, 2 * K1), lambda b, g: (0, 0)),
                  pl.BlockSpec((1, R, G, C), lambda b, g: (b, 0, g, g_blk)),
                  pl.BlockSpec((1, R, G, C), lambda b, g: (b, 0, g, z_blk)),
                  pl.BlockSpec((1, C), lambda b, g: (0, 0))],
        out_specs=pl.BlockSpec((1, R, G, C), lambda b, g: (b, 0, g, 0)),
        out_shape=jax.ShapeDtypeStruct((B, R, N2, C), F32),
        compiler_params=_cp(("parallel", "parallel")),
    )(b5, tab["f1inv"], uc4, z4, bias.reshape(1, C))


def _hyena_h(proj, conv_w, conv_b, w1, b1, w2, b2, w3, freq, bias):
    B, L, _ = proj.shape
    C = D_GROUP
    tab = _dft_tables_h(2 * L)
    N2, R = tab["N2"], tab["R"]
    uc4 = _dwconv3(proj, conv_w, conv_b).reshape(B, R, N2, 3 * C)
    hf, sums = _filter_mlp(L, w1, b1, w2, b2, w3, freq)
    s4 = sums.reshape(2, 2, C)
    inv_sum = (1.0 / (s4[:, 0] + s4[:, 1])).reshape(1, 2 * C)
    af = _fft1(hf.reshape(1, R, N2, 4 * C), tab, 0, 4)
    kf = _filter_fft2(af, tab, inv_sum)
    z4, z_blk = uc4, 0
    for o in range(2):
        a5 = _fft1(z4, tab, z_blk, 1)
        b5 = _conv_mid_h(a5, kf, o, tab)
        z4 = _conv_out_h(b5, tab, uc4, 1 + o, z4, z_blk, bias[o])
        z_blk = 0
    return z4.reshape(B, L, C)


def _router_kernel(x_ref, g_ref, rt_ref, h_ref, aff_ref):
    d = x_ref.shape[1]
    xv = x_ref[...]
    ms = jnp.mean(xv * xv, axis=-1, keepdims=True)
    h = xv * lax.rsqrt(ms + EPS) * g_ref[...]
    h_ref[:, 0:d] = h
    logits = lax.dot_general(rt_ref[...], h, (((1,), (1,)), ((), ())), precision=lax.Precision.HIGHEST,
                             preferred_element_type=F32)
    row = lax.broadcasted_iota(I32, logits.shape, 0)
    logits = jnp.where(row < N_EXPERTS, logits, -jnp.inf)
    e = jnp.exp(logits - jnp.max(logits, axis=0, keepdims=True))
    aff = e / jnp.sum(e, axis=0, keepdims=True)
    aff_ref[...] = aff[0:N_EXPERTS]
    h_ref[:, d:d + LANES] = aff.T


def _router(x, g, router_t, tm=512):
    n, d = x.shape
    rt = jnp.pad(router_t, ((0, LANES - N_EXPERTS), (0, 0)))
    return pl.pallas_call(
        _router_kernel,
        grid=(n // tm,),
        in_specs=[pl.BlockSpec((tm, d), lambda i: (i, 0)), pl.BlockSpec((1, d), lambda i: (0, 0)),
                  pl.BlockSpec((LANES, d), lambda i: (0, 0))],
        out_specs=[pl.BlockSpec((tm, d + LANES), lambda i: (i, 0)), pl.BlockSpec((N_EXPERTS, tm), lambda i: (0, i))],
        out_shape=[jax.ShapeDtypeStruct((n, d + LANES), F32), jax.ShapeDtypeStruct((N_EXPERTS, n), F32)],
        compiler_params=_cp(("parallel",)),
    )(x, g.reshape(1, d), rt)


def _select_kernel(aff_ref, pos_ref, cs_ref, m_s, u_s, *, n, cap, CH):
    E = N_EXPERTS
    bits = pltpu.bitcast(aff_ref[...], I32)

    def search(it, thr):
        cand = thr | jnp.left_shift(jnp.int32(1), 30 - it)
        cnt = jnp.sum((bits >= cand).astype(F32), axis=1, keepdims=True)
        return jnp.where(cnt >= cap, cand, thr)

    thr = lax.fori_loop(0, 31, search, jnp.zeros((E, 1), I32))
    need = cap - jnp.sum((bits > thr).astype(F32), axis=1, keepdims=True)
    r = lax.broadcasted_iota(I32, (CH, CH), 0)
    c = lax.broadcasted_iota(I32, (CH, CH), 1)
    u_s[...] = (r <= c).astype(BF16)

    def chunk_bits(j):
        off = pl.multiple_of(j * CH, CH)
        return off, pltpu.bitcast(aff_ref[:, pl.ds(off, CH)], I32)

    def ties(j, carry):
        off, b = chunk_bits(j)
        eq = (b == thr).astype(F32)
        incl = _dot(eq.astype(BF16), u_s[...]) + carry
        sel = (b > thr) | ((b == thr) & (incl - eq < need))
        m_s[:, pl.ds(off, CH)] = sel.astype(F32)
        return incl[:, CH - 1:CH]

    lax.fori_loop(0, n // CH, ties, jnp.zeros((E, 1), F32))

    def slots(j, carry):
        off, _ = chunk_bits(j)
        m = m_s[:, pl.ds(off, CH)]
        incl = _dot(m.astype(BF16), u_s[...]) + carry
        cs_ref[:, pl.ds(off, CH)] = incl.astype(I32)
        pos_ref[:, pl.ds(off, CH)] = jnp.where(m > 0.0, incl - 1.0, -1.0).astype(I32)
        return incl[:, CH - 1:CH]

    lax.fori_loop(0, n // CH, slots, jnp.zeros((E, 1), F32))


def _select(aff, cap, CH=512):
    E, n = aff.shape
    CH = min(CH, n)
    full = pl.BlockSpec((E, n), lambda: (0, 0))
    return pl.pallas_call(
        functools.partial(_select_kernel, n=n, cap=cap, CH=CH),
        in_specs=[full],
        out_specs=[full, full, full],
        out_shape=[jax.ShapeDtypeStruct((E, n), I32), jax.ShapeDtypeStruct((E, n), I32),
                   jax.ShapeDtypeStruct((E, n), F32)],
        scratch_shapes=[pltpu.VMEM((CH, CH), BF16)],
        compiler_params=_cp(None),
    )(aff)


def _compact_kernel(m_ref, idx_ref, *, cap, RB):
    m = m_ref[0]
    nch = m.shape[0]
    r = lax.broadcasted_iota(I32, (LANES, LANES), 0)
    c = lax.broadcasted_iota(I32, (LANES, LANES), 1)
    local = _dot(m, (r <= c).astype(F32))
    tot = _dot(m, (r >= 0).astype(F32))
    rr = lax.broadcasted_iota(I32, (nch, nch), 0)
    cc = lax.broadcasted_iota(I32, (nch, nch), 1)
    cend = _dot((cc <= rr).astype(F32), tot)
    cend_row = cend.T[0:1, :]
    cstart_row = cend_row - tot.T[0:1, :]
    chunk_id = lax.broadcasted_iota(I32, (RB, nch), 1).astype(F32)

    def block(b, carry):
        j0 = pl.multiple_of(b * RB, RB)
        slot = (j0 + lax.broadcasted_iota(I32, (RB, 1), 0)).astype(F32)
        cstar = jnp.sum((slot >= cend_row).astype(F32), axis=1, keepdims=True)
        oh = chunk_id == cstar
        counts = _dot(jnp.where(oh, 1.0, 0.0), local)
        first = jnp.sum(jnp.where(oh, cstart_row, 0.0), axis=1, keepdims=True)
        within = jnp.sum((counts <= slot - first).astype(F32), axis=1, keepdims=True)
        tok = jnp.broadcast_to(LANES * cstar + within, (RB, LANES)).T
        idx_ref[0, :, pl.ds(j0, RB)] = tok[0:8].astype(I32)
        return carry

    lax.fori_loop(0, cap // RB, block, 0)


def _compact(mask, n, cap, RB=512):
    E = N_EXPERTS
    nch = n // LANES
    m3 = mask.reshape(E, nch, LANES)
    if nch < LANES:
        m3 = jnp.pad(m3, ((0, 0), (0, LANES - nch), (0, 0)))
        nch = LANES
    RB = min(RB, cap)
    idx8 = pl.pallas_call(
        functools.partial(_compact_kernel, cap=cap, RB=RB),
        grid=(E,),
        in_specs=[pl.BlockSpec((1, nch, LANES), lambda e: (e, 0, 0))],
        out_specs=pl.BlockSpec((1, 8, cap), lambda e: (e, 0, 0)),
        out_shape=jax.ShapeDtypeStruct((E, 8, cap), I32),
        compiler_params=_cp(("parallel",)),
    )(m3)
    return idx8[:, 0, :].reshape(-1)


def _expert_kernel(idx_ref, nidx_ref, h_ref, wg_ref, wu_ref, wd_ref, o_ref, xbuf, xb, acc, sems,
                   *, tm, nf, d, nrt):
    e = pl.program_id(0)
    r = pl.program_id(1)
    f = pl.program_id(2)
    q = e * nrt + r
    slot = lax.rem(q, 2)
    last = q == N_EXPERTS * nrt - 1
    per = tm // nf

    def row_copy(t, s, j):
        return pltpu.make_async_copy(h_ref.at[pl.ds(t, 1)], xbuf.at[s, pl.ds(j, 1)], sems.at[s])

    def tile_wait(s):
        pltpu.make_async_copy(h_ref.at[pl.ds(0, tm)], xbuf.at[s, :, :], sems.at[s]).wait()

    @pl.when((q == 0) & (f == 0))
    def _():
        def issue(j, carry):
            row_copy(idx_ref[j], 0, j).start()
            return carry

        lax.fori_loop(0, tm, issue, 0, unroll=8)

    @pl.when(f == 0)
    def _():
        tile_wait(slot)
        xb[...] = xbuf[slot, :, 0:d].astype(BF16)
        acc[...] = jnp.zeros_like(acc)

    for u in range(per):
        j = f * per + u
        row_copy(nidx_ref[r * tm + j], 1 - slot, j).start()

    x = xb[...]
    g = _dot(x, wg_ref[0, 0])
    u_ = _dot(x, wu_ref[0, 0])
    hid = (g * _sigmoid(g) * u_).astype(BF16)
    acc[...] += _dot(hid, wd_ref[0, 0])

    @pl.when(f == nf - 1)
    def _():
        aff = xbuf[slot, :, d:d + LANES]
        lane = lax.broadcasted_iota(I32, aff.shape, 1)
        gate = jnp.sum(jnp.where(lane == e, aff, 0.0), axis=1, keepdims=True)
        o_ref[0] = (acc[...] * gate).astype(BF16)

    @pl.when(last & (f == nf - 1))
    def _():
        tile_wait(1 - slot)


def _experts(idx_flat, h, wg, wu, wd, layer, cap, tm=1024, tf=512):
    n, dx = h.shape
    _, E, d, dff = wg.shape
    tm = min(tm, cap)
    nf = dff // tf
    nrt = cap // tm
    return pl.pallas_call(
        functools.partial(_expert_kernel, tm=tm, nf=nf, d=d, nrt=nrt),
        grid=(E, nrt, nf),
        in_specs=[pl.BlockSpec((cap,), lambda e, r, f: (e,), memory_space=pltpu.SMEM),
                  pl.BlockSpec((cap,), lambda e, r, f: (e,), memory_space=pltpu.SMEM),
                  pl.BlockSpec(memory_space=pl.ANY),
                  pl.BlockSpec((1, 1, d, tf), lambda e, r, f: (layer, e, 0, f)),
                  pl.BlockSpec((1, 1, d, tf), lambda e, r, f: (layer, e, 0, f)),
                  pl.BlockSpec((1, 1, tf, d), lambda e, r, f: (layer, e, f, 0))],
        out_specs=pl.BlockSpec((1, tm, d), lambda e, r, f: (e, r, 0)),
        out_shape=jax.ShapeDtypeStruct((E, cap, d), BF16),
        scratch_shapes=[pltpu.VMEM((2, tm, dx), F32), pltpu.VMEM((tm, d), BF16), pltpu.VMEM((tm, d), F32),
                        pltpu.SemaphoreType.DMA((2,))],
        compiler_params=_cp(("arbitrary", "arbitrary", "arbitrary")),
    )(idx_flat, jnp.roll(idx_flat, -tm), h, wg, wu, wd)


COMBINE_WIN = 128
ROW_ALIGN = 16


def _combine_kernel(st_ref, x_ref, pos_ref, ye_ref, o_ref, buf, sems, buf2, sem2, p_s, *, TT, cap, nt):
    i = pl.program_id(0)
    E = N_EXPERTS
    W = COMBINE_WIN

    def window(tile, e, w):
        s0 = st_ref[tile * E + e]
        lo = lax.shift_left(lax.shift_right_logical(s0, 4), 4) + W * w
        return jnp.minimum(lo, cap - W), lo

    def first_copy(tile, slot, e):
        a, _ = window(tile, e, 0)
        return pltpu.make_async_copy(ye_ref.at[e, pl.ds(pl.multiple_of(a, ROW_ALIGN), W)],
                                     buf.at[slot, pl.ds(e * W, W)], sems.at[slot, e])

    slot = lax.rem(i, 2)

    @pl.when(i == 0)
    def _():
        for e in range(E):
            first_copy(0, 0, e).start()

    @pl.when(i + 1 < nt)
    def _():
        for e in range(E):
            first_copy(i + 1, 1 - slot, e).start()

    lane = lax.broadcasted_iota(I32, (TT, W), 1)

    def onehot(pc, a, lo):
        hit = (pc - a == lane) & (pc >= lo) & (pc < lo + W)
        return jnp.where(hit, 1.0, 0.0).astype(BF16)

    for e in range(E):
        a, lo = window(i, e, 0)
        p_s[:, e * W:(e + 1) * W] = onehot(pos_ref[:, e:e + 1], a, lo)
    for e in range(E):
        first_copy(i, slot, e).wait()
    o_ref[...] = x_ref[...] + _dot(p_s[...], buf[slot])

    for e in range(E):
        _, lo = window(i, e, 0)
        s1 = st_ref[(i + 1) * E + e]
        nwin = lax.shift_right_logical(s1 - lo + (W - 1), 7)

        def extra(w, carry):
            a2, lo2 = window(i, e, w)
            cp = pltpu.make_async_copy(ye_ref.at[e, pl.ds(pl.multiple_of(a2, ROW_ALIGN), W)], buf2, sem2)
            cp.start()
            cp.wait()
            o_ref[...] += _dot(onehot(pos_ref[:, e:e + 1], a2, lo2), buf2[...])
            return carry

        lax.fori_loop(1, nwin, extra, 0)


def _combine(starts, x, pos_t, ye, cap, TT=256):
    n, d = x.shape
    E = N_EXPERTS
    TT = min(TT, n)
    nt = n // TT
    grid_spec = pltpu.PrefetchScalarGridSpec(
        num_scalar_prefetch=1,
        grid=(nt,),
        in_specs=[pl.BlockSpec((TT, d), lambda i, st: (i, 0)),
                  pl.BlockSpec((TT, E), lambda i, st: (i, 0)),
                  pl.BlockSpec(memory_space=pl.ANY)],
        out_specs=pl.BlockSpec((TT, d), lambda i, st: (i, 0)),
        scratch_shapes=[pltpu.VMEM((2, E * COMBINE_WIN, d), BF16), pltpu.SemaphoreType.DMA((2, E)),
                        pltpu.VMEM((COMBINE_WIN, d), BF16), pltpu.SemaphoreType.DMA(()),
                        pltpu.VMEM((TT, E * COMBINE_WIN), BF16)],
    )
    return pl.pallas_call(
        functools.partial(_combine_kernel, TT=TT, cap=cap, nt=nt),
        grid_spec=grid_spec,
        out_shape=jax.ShapeDtypeStruct((n, d), F32),
        compiler_params=_cp(("arbitrary",)),
    )(starts, x, pos_t, ye)


def _moe(x, g, router, wg, wu, wd, layer):
    n, d = x.shape
    E = N_EXPERTS
    cap = EC_CAPACITY * n // E
    TT = min(256, n)
    h, aff = _router(x, g, router.T)
    pos, cs, mask = _select(aff, cap)
    idx = _compact(mask, n, cap)
    ye = _experts(idx, h, wg, wu, wd, layer, cap)
    starts = jnp.concatenate([jnp.zeros((E, 1), I32), cs[:, TT - 1::TT]], axis=1).T.reshape(-1)
    return _combine(starts, x, pos.T, ye, cap, TT=TT)


def _rmsnorm_kernel(x_ref, g_ref, o_ref):
    xv = x_ref[...]
    o_ref[...] = xv * lax.rsqrt(jnp.mean(xv * xv, axis=-1, keepdims=True) + EPS) * g_ref[...]


def _rmsnorm(x, g, tm=512):
    n, d = x.shape
    return pl.pallas_call(
        _rmsnorm_kernel,
        grid=(n // tm,),
        in_specs=[pl.BlockSpec((tm, d), lambda i: (i, 0)), pl.BlockSpec((1, d), lambda i: (0, 0))],
        out_specs=pl.BlockSpec((tm, d), lambda i: (i, 0)),
        out_shape=jax.ShapeDtypeStruct((n, d), F32),
        compiler_params=_cp(("parallel",)),
    )(x, g.reshape(1, d))


def _even_mixer(x, B, L, p):
    proj = _rms_matmul(x, p["norm"], p["w_in"]).reshape(B, L, 5 * D_GROUP)
    hf = _lru(proj, None, p["conv_w"], p["conv_b"], p["wcat"][0], p["ba"][0], p["bx"][0], p["lam"][0],
              reverse=False)
    a_out = _lru(proj, hf, p["conv_w"], p["conv_b"], p["wcat"][1], p["ba"][1], p["bx"][1], p["lam"][1],
                 reverse=True)
    inv_freq = ROPE_THETA ** (-jnp.arange(0, DIFF_DH, 2, dtype=F32) / DIFF_DH)
    tabs = _rope_tables(L, DIFF_DH, inv_freq)
    half = DIFF_DH // 2
    qr = _rope(proj, 2, tabs, sh_a=LANES - half, sh_b=half, scale=DIFF_DH ** -0.5 * math.log2(math.e))
    kr = _rope(proj, 3, tabs, sh_a=LANES - half, sh_b=half, scale=1.0)
    vt = _v_transpose(proj, 4)
    b_out = _diff_attention(qr, kr, vt, p["diff_lam"], p["subln"], p["lam_init"])
    return _out_matmul(a_out.reshape(B * L, D_GROUP), b_out.reshape(B * L, D_GROUP), p["w_out"], x)


def _odd_mixer(x, B, L, p):
    proj = _rms_matmul(x, p["norm"], p["w_in"]).reshape(B, L, 7 * D_GROUP)
    c_out = _hyena_h(proj, p["conv_w"], p["conv_b"], p["w1"], p["b1"], p["w2"], p["b2"], p["w3"], p["freq"],
                   p["bias"])
    inv_freq = 1.0 / (10000.0 ** jnp.linspace(0.0, 1.0, RET_DH // 2, dtype=F32))
    tabs = _rope_tables(L, RET_DH, inv_freq)
    half = RET_DH // 2
    qr = _rope(proj, 3, tabs, sh_a=LANES - half, sh_b=half, scale=1.0)
    kr = _rope(proj, 4, tabs, sh_a=LANES - half, sh_b=half, scale=RET_DH ** -0.5)
    d_out = _retention(proj, qr, kr, p["rho"], 5 * N_HEAD_BLOCKS, 6 * N_HEAD_BLOCKS)
    return _out_matmul(c_out.reshape(B * L, D_GROUP), d_out.reshape(B * L, D_GROUP), p["w_out"], x)


def _trunk(x3, layers, final_norm):
    B, L, D = x3.shape
    x = x3.reshape(B * L, D)
    for kind, mp, ep in layers:
        x = (_even_mixer if kind == "even" else _odd_mixer)(x, B, L, mp)
        x = _moe(x, ep["norm"], ep["router"], ep["wg"], ep["wu"], ep["wd"], ep["layer"])
    return _rmsnorm(x, final_norm).reshape(B, L, D)


def kernel(x_prompt, x_sample, ev_norm, ev_w_in, ev_conv_w, ev_conv_b, ev_rg_wa, ev_rg_ba, ev_rg_wx, ev_rg_bx, ev_rg_lam, ev_diff_lam, ev_subln, ev_w_out, od_norm, od_w_in, od_conv_w, od_conv_b, od_flt_w1, od_flt_b1, od_flt_w2, od_flt_b2, od_flt_w3, od_flt_freq, od_flt_bias, od_ret_rho, od_w_out, moe_norm, moe_router, moe_w_gate, moe_w_up, moe_w_down, final_norm):
    depth = moe_norm.shape[0]
    wg_bf, wu_bf, wd_bf = moe_w_gate.astype(BF16), moe_w_up.astype(BF16), moe_w_down.astype(BF16)
    layers = []
    for layer in range(depth):
        j = layer // 2
        if layer % 2 == 0:
            mp = dict(norm=ev_norm[j], w_in=ev_w_in[j].astype(BF16), conv_w=ev_conv_w[j], conv_b=ev_conv_b[j],
                      wcat=jnp.concatenate([ev_rg_wa[j], ev_rg_wx[j]], axis=-1).astype(BF16),
                      ba=ev_rg_ba[j], bx=ev_rg_bx[j], lam=ev_rg_lam[j], diff_lam=ev_diff_lam[j],
                      subln=ev_subln[j], w_out=ev_w_out[j].astype(BF16),
                      lam_init=0.8 - 0.6 * math.exp(-0.3 * layer))
            kind = "even"
        else:
            mp = dict(norm=od_norm[j], w_in=od_w_in[j].astype(BF16), conv_w=od_conv_w[j], conv_b=od_conv_b[j],
                      w1=od_flt_w1[j], b1=od_flt_b1[j], w2=od_flt_w2[j], b2=od_flt_b2[j], w3=od_flt_w3[j],
                      freq=od_flt_freq[j], bias=od_flt_bias[j], rho=od_ret_rho[j],
                      w_out=od_w_out[j].astype(BF16))
            kind = "odd"
        ep = dict(norm=moe_norm[layer], router=moe_router[layer], wg=wg_bf, wu=wu_bf, wd=wd_bf, layer=layer)
        layers.append((kind, mp, ep))
    return (_trunk(x_prompt, layers, final_norm), _trunk(x_sample, layers, final_norm))
```

```python
import functools
import math

import jax
import jax.numpy as jnp
import numpy as np
from jax import lax
from jax.experimental import pallas as pl
from jax.experimental.pallas import tpu as pltpu

F32 = jnp.float32
BF16 = jnp.bfloat16
I32 = jnp.int32

D_MODEL = 2048
D_GROUP = 1024
LANES = 128
N_HEAD_BLOCKS = D_GROUP // LANES
LRU_C = 8.0
DIFF_DH = 64
ROPE_THETA = 10000.0
RET_DH = 128
HY_EMB = 33
HY_FFN = 64
HY_FAST_DECAY = 0.3
HY_SLOW_DECAY = 1.5
HY_TARGET = 1e-2
N_EXPERTS = 16
EC_CAPACITY = 2
EPS = 1e-6
FFT_N2 = 128
HALO = 8
VMEM_LIMIT_MB = 56


def _cp(sem, vmem_mb=VMEM_LIMIT_MB):
    return pltpu.CompilerParams(dimension_semantics=sem, vmem_limit_bytes=vmem_mb * 1024 * 1024)


def _sigmoid(x):
    return 1.0 / (1.0 + jnp.exp(-x))


def _softplus(x):
    return jnp.maximum(x, 0.0) + jnp.log(1.0 + jnp.exp(-jnp.abs(x)))


def _dot(a, b):
    return jnp.dot(a, b, preferred_element_type=F32)


def _dot_nt(a, b):
    return lax.dot_general(a, b, (((1,), (1,)), ((), ())), preferred_element_type=F32)


def _rms_matmul_kernel(x_ref, g_ref, w_ref, o_ref, h_ref):
    @pl.when(pl.program_id(1) == 0)
    def _():
        xv = x_ref[...]
        ms = jnp.mean(xv * xv, axis=-1, keepdims=True)
        h_ref[...] = (xv * lax.rsqrt(ms + EPS) * g_ref[...]).astype(BF16)

    o_ref[...] = _dot(h_ref[...], w_ref[...])


def _rms_matmul(x, g, w_bf, tm=1024, tn=1024):
    n, d = x.shape
    tm = min(tm, n)
    nout = w_bf.shape[1]
    return pl.pallas_call(
        _rms_matmul_kernel,
        grid=(n // tm, nout // tn),
        in_specs=[
            pl.BlockSpec((tm, d), lambda i, j: (i, 0)),
            pl.BlockSpec((1, d), lambda i, j: (0, 0)),
            pl.BlockSpec((d, tn), lambda i, j: (0, j)),
        ],
        out_specs=pl.BlockSpec((tm, tn), lambda i, j: (i, j)),
        out_shape=jax.ShapeDtypeStruct((n, nout), F32),
        scratch_shapes=[pltpu.VMEM((tm, d), BF16)],
        compiler_params=_cp(("parallel", "arbitrary")),
    )(x, g.reshape(1, d), w_bf)


def _out_matmul_kernel(a_ref, b_ref, wa_ref, wb_ref, x_ref, o_ref):
    o_ref[...] = (x_ref[...] + _dot(a_ref[...].astype(BF16), wa_ref[...])
                  + _dot(b_ref[...].astype(BF16), wb_ref[...]))


def _out_matmul(a, b, w_bf, x, tm=1024, tn=1024):
    n, d = x.shape
    tm = min(tm, n)
    dg = a.shape[1]
    return pl.pallas_call(
        _out_matmul_kernel,
        grid=(n // tm, d // tn),
        in_specs=[
            pl.BlockSpec((tm, dg), lambda i, j: (i, 0)),
            pl.BlockSpec((tm, dg), lambda i, j: (i, 0)),
            pl.BlockSpec((dg, tn), lambda i, j: (0, j)),
            pl.BlockSpec((dg, tn), lambda i, j: (1, j)),
            pl.BlockSpec((tm, tn), lambda i, j: (i, j)),
        ],
        out_specs=pl.BlockSpec((tm, tn), lambda i, j: (i, j)),
        out_shape=jax.ShapeDtypeStruct((n, d), F32),
        compiler_params=_cp(("parallel", "parallel")),
    )(a, b, w_bf, w_bf, x)


def _lru_kernel(*refs, reverse, T, nt):
    if reverse:
        (x_ref, prev_ref, next_ref, cw_ref, cb_ref, w_ref, ba_ref, bx_ref, lam_ref,
         gate_ref, hf_ref, o_ref, a_s, b_s, carry, h_s) = refs
    else:
        (x_ref, prev_ref, next_ref, cw_ref, cb_ref, w_ref, ba_ref, bx_ref, lam_ref,
         o_ref, a_s, b_s, carry) = refs
    i = pl.program_id(1)
    ti = (nt - 1 - i) if reverse else i

    @pl.when(i == 0)
    def _():
        carry[...] = jnp.zeros_like(carry)

    x = x_ref[0]
    prev = jnp.where(ti == 0, 0.0, prev_ref[0])
    nxt = jnp.where(ti == nt - 1, 0.0, next_ref[0])
    xe = jnp.concatenate([prev, x, nxt], axis=0)
    cw = cw_ref[...]
    xc = (cb_ref[...] + cw[0:1] * xe[HALO - 2:HALO - 2 + T] + cw[1:2] * xe[HALO - 1:HALO - 1 + T]
          + cw[2:3] * xe[HALO:HALO + T] + cw[3:4] * xe[HALO + 1:HALO + 1 + T])
    sp = _softplus(-lam_ref[...])
    for blk in range(N_HEAD_BLOCKS):
        sl = slice(blk * LANES, (blk + 1) * LANES)
        xb = xc[:, sl]
        ri = _dot(xb.astype(BF16), w_ref[blk])
        r = _sigmoid(ri[:, :LANES] + ba_ref[:, sl])
        ig = _sigmoid(ri[:, LANES:] + bx_ref[:, sl])
        a = jnp.exp(-LRU_C * r * sp[:, sl])
        a_s[:, sl] = a
        b_s[:, sl] = jnp.sqrt(1.0 - a * a) * (ig * xb)

    dst = h_s if reverse else o_ref.at[0]

    def body(k, h):
        t = (T - 1 - k) if reverse else k
        h = a_s[pl.ds(t, 1), :] * h + b_s[pl.ds(t, 1), :]
        dst[pl.ds(t, 1), :] = h
        return h

    carry[...] = lax.fori_loop(0, T, body, carry[...], unroll=8)

    if reverse:
        g = gate_ref[0]
        gelu = 0.5 * g * (1.0 + jnp.tanh(math.sqrt(2.0 / math.pi) * (g + 0.044715 * g * g * g)))
        o_ref[0] = ((hf_ref[0] + h_s[...]) * gelu).astype(BF16)


def _lru(proj, hf, conv_w, conv_b, wcat, ba, bx, lam, *, reverse, T=512):
    B, L, _ = proj.shape
    T = min(T, L)
    nt = L // T
    hb = T // HALO
    nh = L // HALO

    def tile(i):
        return (nt - 1 - i) if reverse else i

    x_spec = pl.BlockSpec((1, T, D_GROUP), lambda b, i: (b, tile(i), 0))
    prev_spec = pl.BlockSpec((1, HALO, D_GROUP), lambda b, i: (b, jnp.maximum(tile(i) * hb - 1, 0), 0))
    next_spec = pl.BlockSpec((1, HALO, D_GROUP), lambda b, i: (b, jnp.minimum((tile(i) + 1) * hb, nh - 1), 0))
    vec = pl.BlockSpec((1, D_GROUP), lambda b, i: (0, 0))
    in_specs = [x_spec, prev_spec, next_spec,
                pl.BlockSpec((4, D_GROUP), lambda b, i: (0, 0)), vec,
                pl.BlockSpec((N_HEAD_BLOCKS, LANES, 2 * LANES), lambda b, i: (0, 0, 0)), vec, vec, vec]
    args = [proj, proj, proj, conv_w, conv_b.reshape(1, -1), wcat, ba.reshape(1, -1), bx.reshape(1, -1),
            lam.reshape(1, -1)]
    scratch = [pltpu.VMEM((T, D_GROUP), F32), pltpu.VMEM((T, D_GROUP), F32), pltpu.VMEM((1, D_GROUP), F32)]
    if reverse:
        in_specs += [pl.BlockSpec((1, T, D_GROUP), lambda b, i: (b, tile(i), 1)),
                     pl.BlockSpec((1, T, D_GROUP), lambda b, i: (b, tile(i), 0))]
        args += [proj, hf]
        scratch += [pltpu.VMEM((T, D_GROUP), F32)]
        out_dtype = BF16
    else:
        out_dtype = F32
    return pl.pallas_call(
        functools.partial(_lru_kernel, reverse=reverse, T=T, nt=nt),
        grid=(B, nt),
        in_specs=in_specs,
        out_specs=pl.BlockSpec((1, T, D_GROUP), lambda b, i: (b, tile(i), 0)),
        out_shape=jax.ShapeDtypeStruct((B, L, D_GROUP), out_dtype),
        scratch_shapes=scratch,
        compiler_params=_cp(("parallel", "arbitrary")),
    )(*args)


def _rope_kernel(x_ref, c_ref, sa_ref, sb_ref, o_ref, *, sh_a, sh_b, scale):
    c = c_ref[...]
    sa = sa_ref[...]
    sb = sb_ref[...]
    for h in range(N_HEAD_BLOCKS):
        sl = slice(h * LANES, (h + 1) * LANES)
        x = x_ref[0, :, sl]
        y = x * c + pltpu.roll(x, sh_a, 1) * sa + pltpu.roll(x, sh_b, 1) * sb
        o_ref[0, :, sl] = (y * scale).astype(o_ref.dtype)


def _rope(proj, col_block, tabs, *, sh_a, sh_b, scale, T=512):
    B, L, _ = proj.shape
    T = min(T, L)
    tab_spec = pl.BlockSpec((T, LANES), lambda b, i: (i, 0))
    return pl.pallas_call(
        functools.partial(_rope_kernel, sh_a=sh_a, sh_b=sh_b, scale=scale),
        grid=(B, L // T),
        in_specs=[pl.BlockSpec((1, T, D_GROUP), lambda b, i: (b, i, col_block)), tab_spec, tab_spec, tab_spec],
        out_specs=pl.BlockSpec((1, T, D_GROUP), lambda b, i: (b, i, 0)),
        out_shape=jax.ShapeDtypeStruct((B, L, D_GROUP), BF16),
        compiler_params=_cp(("parallel", "parallel")),
    )(proj, *tabs)


def _rope_tables(L, dh, inv_freq):
    lane = np.arange(LANES)
    d = lane % dh
    half = dh // 2
    ang = jnp.arange(L, dtype=F32)[:, None] * inv_freq[d % half][None, :]
    cos = jnp.cos(ang)
    sin = jnp.sin(ang)
    lo = jnp.asarray(d < half)[None, :]
    return cos, jnp.where(lo, -sin, 0.0), jnp.where(lo, 0.0, sin)


VT_ROWS = LANES + 16
MAX_EAGER_JUMP = 60.0


def _vt_kernel(v_ref, o_ref):
    T = v_ref.shape[1]
    for h in range(N_HEAD_BLOCKS):
        sl = slice(h * LANES, (h + 1) * LANES)
        o_ref[0, h, 0:LANES, :] = v_ref[0, :, sl].T.astype(BF16)
        o_ref[0, h, LANES:VT_ROWS, :] = jnp.ones((VT_ROWS - LANES, T), BF16)


def _v_transpose(proj, col_block, T=512):
    B, L, _ = proj.shape
    T = min(T, L)
    return pl.pallas_call(
        _vt_kernel,
        grid=(B, L // T),
        in_specs=[pl.BlockSpec((1, T, D_GROUP), lambda b, i: (b, i, col_block))],
        out_specs=pl.BlockSpec((1, N_HEAD_BLOCKS, VT_ROWS, T), lambda b, i: (b, 0, 0, i)),
        out_shape=jax.ShapeDtypeStruct((B, N_HEAD_BLOCKS, VT_ROWS, L), BF16),
        compiler_params=_cp(("parallel", "parallel")),
    )(proj)


def _attn_kernel(q_ref, k_ref, vt_ref, lv_ref, sg_ref, o_ref, q2_s, m_s, acc_s, *, tq, tks, nsub, nk, lam_init):
    ki = pl.program_id(3)

    @pl.when(ki == 0)
    def _():
        q = q_ref[0]
        lane = lax.broadcasted_iota(I32, q.shape, 1)
        zero = jnp.zeros_like(q)
        q2_s[0:tq, :] = jnp.where(lane < DIFF_DH, q, zero)
        q2_s[tq:2 * tq, :] = jnp.where(lane >= DIFF_DH, q, zero)
        m_s[...] = jnp.full_like(m_s, -jnp.inf)
        acc_s[...] = jnp.zeros_like(acc_s)

    def scores(j):
        return _dot_nt(k_ref[0, j * tks:(j + 1) * tks, :], q2_s[...])

    def softmax_pass(eager):
        m_in = m_s[...]
        acc_in = acc_s[...]
        jump = jnp.zeros_like(m_in)
        s_next = scores(0)
        for j in range(nsub):
            ks = slice(j * tks, (j + 1) * tks)
            s = s_next
            if j + 1 < nsub:
                s_next = scores(j + 1)
            m_prev = m_in if j == 0 else m_s[...]
            acc_prev = acc_in if j == 0 else acc_s[...]
            m_new = jnp.maximum(m_prev, jnp.max(s, axis=0, keepdims=True))
            if eager and j > 0:
                p = jnp.exp2(s - m_prev).astype(BF16)
                acc_s[...] = jnp.exp2(m_prev - m_new) * (acc_prev + _dot(vt_ref[0, 0, :, ks], p))
                jump = jnp.maximum(jump, m_new - m_prev)
            else:
                p = jnp.exp2(s - m_new).astype(BF16)
                acc_s[...] = jnp.exp2(m_prev - m_new) * acc_prev + _dot(vt_ref[0, 0, :, ks], p)
            m_s[...] = m_new
        return m_in, acc_in, jnp.max(jump)

    m_in, acc_in, jump = softmax_pass(True)

    @pl.when(jump > MAX_EAGER_JUMP)
    def _():
        m_s[...] = m_in
        acc_s[...] = acc_in
        softmax_pass(False)

    @pl.when(ki == nk - 1)
    def _():
        o = acc_s[0:LANES, :] / acc_s[LANES:LANES + 1, :]
        lv = lv_ref[...]
        lam = (jnp.exp(jnp.sum(lv[0:1] * lv[1:2], axis=1, keepdims=True))
               - jnp.exp(jnp.sum(lv[2:3] * lv[3:4], axis=1, keepdims=True)) + lam_init)
        od = o[:, 0:tq] - lam * o[:, tq:2 * tq]
        ms = jnp.mean(od * od, axis=0, keepdims=True)
        y = od * lax.rsqrt(ms + EPS) * sg_ref[...] * (1.0 - lam_init)
        o_ref[0] = y.T.astype(BF16)


def _diff_attention(qr, kr, vt, diff_lam, subln, lam_init, tq=512, tk=16384, tks=1024):
    B, L, _ = qr.shape
    tq = min(tq, L)
    tk = min(tk, L)
    tks = min(tks, tk)
    nq, nk = L // tq, L // tk
    return pl.pallas_call(
        functools.partial(_attn_kernel, tq=tq, tks=tks, nsub=tk // tks, nk=nk, lam_init=lam_init),
        grid=(B, N_HEAD_BLOCKS, nq, nk),
        in_specs=[
            pl.BlockSpec((1, tq, LANES), lambda b, h, qi, ki: (b, qi, h)),
            pl.BlockSpec((1, tk, LANES), lambda b, h, qi, ki: (b, ki, h)),
            pl.BlockSpec((1, 1, VT_ROWS, tk), lambda b, h, qi, ki: (b, h, 0, ki)),
            pl.BlockSpec((4, DIFF_DH), lambda b, h, qi, ki: (0, 0)),
            pl.BlockSpec((LANES, 1), lambda b, h, qi, ki: (0, 0)),
        ],
        out_specs=pl.BlockSpec((1, tq, LANES), lambda b, h, qi, ki: (b, qi, h)),
        out_shape=jax.ShapeDtypeStruct((B, L, D_GROUP), BF16),
        scratch_shapes=[pltpu.VMEM((2 * tq, LANES), BF16), pltpu.VMEM((1, 2 * tq), F32),
                        pltpu.VMEM((VT_ROWS, 2 * tq), F32)],
        compiler_params=_cp(("parallel", "parallel", "parallel", "arbitrary")),
    )(qr, kr, vt, diff_lam, subln.reshape(LANES, 1))


def _log_gamma(rho_ref):
    return -_softplus(-rho_ref[0, 0])


def _ret_state_kernel(kf_ref, vf_ref, kb_ref, vb_ref, rf_ref, rb_ref, sf_ref, sb_ref, sf_s, sb_s, *, C):
    n = pl.program_id(2)

    @pl.when(n == 0)
    def _():
        sf_s[...] = jnp.zeros_like(sf_s)
        sb_s[...] = jnp.zeros_like(sb_s)

    sf_ref[0, 0, 0] = sf_s[...].astype(BF16)
    sb_ref[0, 0, 0] = sb_s[...].astype(BF16)
    lgf = _log_gamma(rf_ref)
    lgb = _log_gamma(rb_ref)
    row = lax.broadcasted_iota(I32, (C, LANES), 0).astype(F32)
    kdf = kf_ref[0].astype(F32) * jnp.exp(lgf * (C - 1.0 - row))
    kdb = kb_ref[0].astype(F32) * jnp.exp(lgb * row)
    sf_s[...] = sf_s[...] * jnp.exp(lgf * C) + _dot(kdf.T.astype(BF16), vf_ref[0].astype(BF16))
    sb_s[...] = sb_s[...] * jnp.exp(lgb * C) + _dot(kdb.T.astype(BF16), vb_ref[0].astype(BF16))


def _ret_main_kernel(q_ref, k_ref, v_ref, g_ref, sf_ref, sb_ref, rf_ref, rb_ref, o_ref, *, C):
    lgf = _log_gamma(rf_ref)
    lgb = _log_gamma(rb_ref)
    q = q_ref[0]
    s = _dot_nt(q, k_ref[0])
    r = lax.broadcasted_iota(I32, (C, C), 0)
    c = lax.broadcasted_iota(I32, (C, C), 1)
    dist = (r - c).astype(F32)
    dmat = jnp.exp(jnp.where(r >= c, lgf * dist, -lgb * dist))
    inner = _dot((s * dmat).astype(BF16), v_ref[0].astype(BF16))
    row = lax.broadcasted_iota(I32, (C, LANES), 0).astype(F32)
    qf = q.astype(F32)
    qcat = jnp.concatenate([qf * jnp.exp(lgf * (row + 1.0)), qf * jnp.exp(lgb * (C - row))], axis=1)
    scat = jnp.concatenate([sf_ref[0, 0, 0], sb_ref[0, 0, 0]], axis=0)
    y = inner + _dot(qcat.astype(BF16), scat)
    y = y * lax.rsqrt(jnp.mean(y * y, axis=-1, keepdims=True) + EPS)
    g = g_ref[0]
    o_ref[0] = (y * (g * _sigmoid(g))).astype(BF16)


def _retention(proj, qr, kr, rho, v_blk0, g_blk0, C=512):
    B, L, _ = proj.shape
    H = N_HEAD_BLOCKS
    C = min(C, L)
    N = L // C
    rho4 = rho.reshape(2, H, 1, 1)
    rf_spec = pl.BlockSpec((1, 1, 1, 1), lambda b, h, n: (0, h, 0, 0))
    rb_spec = pl.BlockSpec((1, 1, 1, 1), lambda b, h, n: (1, h, 0, 0))
    st_shape = jax.ShapeDtypeStruct((B, H, N, LANES, LANES), BF16)
    sf, sb = pl.pallas_call(
        functools.partial(_ret_state_kernel, C=C),
        grid=(B, H, N),
        in_specs=[
            pl.BlockSpec((1, C, LANES), lambda b, h, n: (b, n, h)),
            pl.BlockSpec((1, C, LANES), lambda b, h, n: (b, n, v_blk0 + h)),
            pl.BlockSpec((1, C, LANES), lambda b, h, n: (b, N - 1 - n, h)),
            pl.BlockSpec((1, C, LANES), lambda b, h, n: (b, N - 1 - n, v_blk0 + h)),
            rf_spec, rb_spec,
        ],
        out_specs=[pl.BlockSpec((1, 1, 1, LANES, LANES), lambda b, h, n: (b, h, n, 0, 0)),
                   pl.BlockSpec((1, 1, 1, LANES, LANES), lambda b, h, n: (b, h, N - 1 - n, 0, 0))],
        out_shape=[st_shape, st_shape],
        scratch_shapes=[pltpu.VMEM((LANES, LANES), F32), pltpu.VMEM((LANES, LANES), F32)],
        compiler_params=_cp(("parallel", "parallel", "arbitrary")),
    )(kr, proj, kr, proj, rho4, rho4)
    st_spec = pl.BlockSpec((1, 1, 1, LANES, LANES), lambda b, h, n: (b, h, n, 0, 0))
    return pl.pallas_call(
        functools.partial(_ret_main_kernel, C=C),
        grid=(B, H, N),
        in_specs=[
            pl.BlockSpec((1, C, LANES), lambda b, h, n: (b, n, h)),
            pl.BlockSpec((1, C, LANES), lambda b, h, n: (b, n, h)),
            pl.BlockSpec((1, C, LANES), lambda b, h, n: (b, n, v_blk0 + h)),
            pl.BlockSpec((1, C, LANES), lambda b, h, n: (b, n, g_blk0 + h)),
            st_spec, st_spec, rf_spec, rb_spec,
        ],
        out_specs=pl.BlockSpec((1, C, LANES), lambda b, h, n: (b, n, h)),
        out_shape=jax.ShapeDtypeStruct((B, L, D_GROUP), BF16),
        compiler_params=_cp(("parallel", "parallel", "parallel")),
    )(qr, kr, proj, proj, sf, sb, rho4, rho4)


def _dwconv3_kernel(x_ref, prev_ref, next_ref, w_ref, b_ref, o_ref, *, T, nt):
    i = pl.program_id(1)
    x = x_ref[0]
    prev = jnp.where(i == 0, 0.0, prev_ref[0])
    nxt = jnp.where(i == nt - 1, 0.0, next_ref[0])
    xe = jnp.concatenate([prev, x, nxt], axis=0)
    w = w_ref[...]
    o_ref[0] = (b_ref[...] + w[0:1] * xe[HALO - 1:HALO - 1 + T] + w[1:2] * xe[HALO:HALO + T]
                + w[2:3] * xe[HALO + 1:HALO + 1 + T])


def _dwconv3(proj, conv_w, conv_b, T=512):
    B, L, _ = proj.shape
    T = min(T, L)
    nt = L // T
    hb = T // HALO
    nh = L // HALO
    W = 3 * D_GROUP
    return pl.pallas_call(
        functools.partial(_dwconv3_kernel, T=T, nt=nt),
        grid=(B, nt),
        in_specs=[
            pl.BlockSpec((1, T, W), lambda b, i: (b, i, 0)),
            pl.BlockSpec((1, HALO, W), lambda b, i: (b, jnp.maximum(i * hb - 1, 0), 0)),
            pl.BlockSpec((1, HALO, W), lambda b, i: (b, jnp.minimum((i + 1) * hb, nh - 1), 0)),
            pl.BlockSpec((3, W), lambda b, i: (0, 0)),
            pl.BlockSpec((1, W), lambda b, i: (0, 0)),
        ],
        out_specs=pl.BlockSpec((1, T, W), lambda b, i: (b, i, 0)),
        out_shape=jax.ShapeDtypeStruct((B, L, W), F32),
        compiler_params=_cp(("parallel", "parallel")),
    )(proj, proj, proj, conv_w, conv_b.reshape(1, W))


def _filter_mlp_kernel(z_ref, w1_ref, b1_ref, w2_ref, b2_ref, w3_ref, fr_ref, hf_ref, sum_ref, *, TL, L):
    i = pl.program_id(0)
    fr = fr_ref[...]
    h = jnp.sin(fr * (_dot(z_ref[...].astype(BF16), w1_ref[...]) + b1_ref[...]))
    for j in range(2):
        h = jnp.sin(fr * (_dot(h.astype(BF16), w2_ref[j]) + b2_ref[j]))
    hf = _dot(h.astype(BF16), w3_ref[...])
    row = i * TL + lax.broadcasted_iota(I32, (TL, D_GROUP), 0)
    t = row.astype(F32) / (L - 1.0)
    ch = lax.broadcasted_iota(I32, (TL, D_GROUP), 1).astype(F32)
    max_decay = math.log(HY_TARGET) / HY_FAST_DECAY
    min_decay = math.log(HY_TARGET) / HY_SLOW_DECAY
    delta = jnp.abs(min_decay + (max_decay - min_decay) * ch / (D_GROUP - 1.0))
    dec = jnp.exp(-t * delta)
    keep = row < L - 1

    @pl.when(i == 0)
    def _():
        sum_ref[...] = jnp.zeros_like(sum_ref)

    for q in range(4):
        sl = slice(q * D_GROUP, (q + 1) * D_GROUP)
        v = hf[:, sl] * dec
        if q % 2 == 1:
            v = jnp.where(keep, v, 0.0)
        hf_ref[:, sl] = v
        sum_ref[:, sl] += jnp.sum(jnp.abs(v), axis=0, keepdims=True)


def _filter_mlp(L, w1, b1, w2, b2, w3, freq, TL=512):
    TL = min(TL, L)
    t = jnp.linspace(0.0, 1.0, L, dtype=F32)[:, None]
    bands = (HY_EMB - 1) // 2
    wpos = 2.0 * math.pi * jnp.arange(L, dtype=F32)[:, None] / L
    f = jnp.linspace(1e-4, bands - 1, bands, dtype=F32)[None, :]
    z = jnp.concatenate([t, jnp.cos(f * wpos), jnp.sin(f * wpos)], axis=-1)
    z = jnp.pad(z, ((0, 0), (0, LANES - HY_EMB)))
    pf = LANES - HY_FFN
    w1p = jnp.pad(w1, ((0, LANES - HY_EMB), (0, pf))).astype(BF16)
    b1p = jnp.pad(b1, (0, pf)).reshape(1, LANES)
    w2p = jnp.pad(w2, ((0, 0), (0, pf), (0, pf))).astype(BF16)
    b2p = jnp.pad(b2, ((0, 0), (0, pf))).reshape(2, 1, LANES)
    w3p = jnp.pad(w3, ((0, pf), (0, 0))).astype(BF16)
    frp = jnp.pad(freq, (0, pf)).reshape(1, LANES)
    W = 4 * D_GROUP
    full = lambda *shape: pl.BlockSpec(shape, lambda i: (0,) * len(shape))
    return pl.pallas_call(
        functools.partial(_filter_mlp_kernel, TL=TL, L=L),
        grid=(L // TL,),
        in_specs=[pl.BlockSpec((TL, LANES), lambda i: (i, 0)), full(LANES, LANES), full(1, LANES),
                  full(2, LANES, LANES), full(2, 1, LANES), full(LANES, W), full(1, LANES)],
        out_specs=[pl.BlockSpec((TL, W), lambda i: (i, 0)), full(1, W)],
        out_shape=[jax.ShapeDtypeStruct((L, W), F32), jax.ShapeDtypeStruct((1, W), F32)],
        compiler_params=_cp(("arbitrary",)),
    )(z, w1p, b1p, w2p, b2p, w3p, frp)


T2_GROUP = 8
SLAB_PAD = 8


def _dft_tables_h(N):
    N2 = FFT_N2
    N1 = N // N2
    R = N1 // 2
    K1 = R + SLAB_PAD
    k1 = np.arange(K1)
    live = (k1 <= R).astype(np.float64)
    t1 = np.arange(R)
    a1 = 2.0 * math.pi * ((k1[:, None] * t1[None, :]) % N1) / N1
    f1 = np.concatenate([np.cos(a1) * live[:, None], -np.sin(a1) * live[:, None]], axis=0)
    wgt = np.where((k1 == 0) | (k1 == R), 1.0, 2.0) * live
    f1inv = np.concatenate([np.cos(a1) * wgt[:, None], -np.sin(a1) * wgt[:, None]], axis=0).T / N
    k2 = np.arange(N2)
    a2 = 2.0 * math.pi * ((k2[:, None] * k2[None, :]) % N2) / N2
    fr, fi = np.cos(a2), -np.sin(a2)
    at = 2.0 * math.pi * (k2[:, None] * k1[None, :]) / N
    ak = 2.0 * math.pi * (k1[:, None] + N1 * k2[None, :]) / N
    tw = lambda a: jnp.broadcast_to(jnp.asarray(a, F32)[:, :, None], a.shape + (LANES,))
    bf = lambda a: jnp.asarray(a, F32).astype(BF16)
    return dict(N1=N1, N2=N2, R=R, K1=K1, f1=bf(f1), f1inv=bf(f1inv),
                f2=bf(np.block([[fr, -fi], [fi, fr]])), f2inv=bf(np.block([[fr, fi], [-fi, fr]])),
                twa_c=tw(np.cos(at)), twa_s=tw(-np.sin(at)), twb_c=tw(np.cos(at.T)), twb_s=tw(np.sin(at.T)),
                wk_c=tw(np.cos(ak)), wk_s=tw(np.sin(ak)))


def _fft1_kernel(x_ref, f_ref, c_ref, s_ref, o_ref, *, K1, C):
    for j in range(T2_GROUP):
        a = _dot(f_ref[...], x_ref[0, :, j, :].astype(BF16))
        c = c_ref[j]
        s = s_ref[j]
        for cb in range(C // LANES):
            sl = slice(cb * LANES, (cb + 1) * LANES)
            ar = a[0:K1, sl]
            ai = a[K1:2 * K1, sl]
            o_ref[0, 0, j, 0:K1, sl] = ar * c - ai * s
            o_ref[0, 0, j, K1:2 * K1, sl] = ar * s + ai * c


def _fft1(x4, tab, col0, ncol):
    Bx, R, N2, _ = x4.shape
    K1 = tab["K1"]
    C = D_GROUP
    G = T2_GROUP
    return pl.pallas_call(
        functools.partial(_fft1_kernel, K1=K1, C=C),
        grid=(Bx, ncol, N2 // G),
        in_specs=[pl.BlockSpec((1, R, G, C), lambda b, ci, g: (b, 0, g, col0 + ci)),
                  pl.BlockSpec((2 * K1, R), lambda b, ci, g: (0, 0)),
                  pl.BlockSpec((G, K1, LANES), lambda b, ci, g: (g, 0, 0)),
                  pl.BlockSpec((G, K1, LANES), lambda b, ci, g: (g, 0, 0))],
        out_specs=pl.BlockSpec((1, 1, G, 2 * K1, C), lambda b, ci, g: (b, ci, g, 0, 0)),
        out_shape=jax.ShapeDtypeStruct((Bx, ncol, N2, 2 * K1, C), F32),
        compiler_params=_cp(("parallel", "parallel", "parallel")),
    )(x4, tab["f1"], tab["twa_c"], tab["twa_s"])


def _strided_cat(re_ref, im_ref, kk):
    return jnp.concatenate([re_ref[0, 0, :, kk, :], im_ref[0, 0, :, kk, :]], axis=0).astype(BF16)


def _filter_fft2_kernel(pr_ref, pi_ref, mr_ref, mi_ref, f_ref, c_ref, s_ref, inv_ref, o_ref, *, N2):
    for kk in range(T2_GROUP):
        zp = _dot(f_ref[...], _strided_cat(pr_ref, pi_ref, kk))
        zm = _dot(f_ref[...], _strided_cat(mr_ref, mi_ref, kk))
        c = c_ref[kk]
        s = s_ref[kk]
        inv = inv_ref[...]
        for cb in range(zp.shape[1] // LANES):
            sl = slice(cb * LANES, (cb + 1) * LANES)
            zmr, zmi = zm[0:N2, sl], zm[N2:2 * N2, sl]
            o_ref[kk, 0, :, sl] = ((zp[0:N2, sl] + c * zmr + s * zmi) * inv[:, sl]).astype(BF16)
            o_ref[kk, 1, :, sl] = ((zp[N2:2 * N2, sl] + s * zmr - c * zmi) * inv[:, sl]).astype(BF16)


def _filter_fft2(af, tab, inv_sum, Ct=512):
    _, _, N2, _, C = af.shape
    K1 = tab["K1"]
    G = T2_GROUP
    nkg = K1 // G
    nc = C // Ct

    def a_spec(side, im):
        return pl.BlockSpec((1, 1, N2, G, Ct), lambda kg, o, cj: (0, 2 * o + side, 0, im * nkg + kg, cj))

    tw_spec = pl.BlockSpec((G, N2, LANES), lambda kg, o, cj: (kg, 0, 0))
    return pl.pallas_call(
        functools.partial(_filter_fft2_kernel, N2=N2),
        grid=(nkg, 2, nc),
        in_specs=[a_spec(0, 0), a_spec(0, 1), a_spec(1, 0), a_spec(1, 1),
                  pl.BlockSpec((2 * N2, 2 * N2), lambda kg, o, cj: (0, 0)), tw_spec, tw_spec,
                  pl.BlockSpec((1, Ct), lambda kg, o, cj: (0, o * nc + cj))],
        out_specs=pl.BlockSpec((G, 2, N2, Ct), lambda kg, o, cj: (kg, 0, 0, o * nc + cj)),
        out_shape=jax.ShapeDtypeStruct((K1, 2, N2, 2 * C), BF16),
        compiler_params=_cp(("parallel", "parallel", "parallel")),
    )(af, af, af, af, tab["f2"], tab["wk_c"], tab["wk_s"], inv_sum)


def _conv_mid_h_kernel(ar_ref, ai_ref, kf_ref, f_ref, fi_ref, c_ref, s_ref, o_ref, *, N2):
    for kk in range(T2_GROUP):
        z = _dot(f_ref[...], _strided_cat(ar_ref, ai_ref, kk))
        zr, zi = z[0:N2], z[N2:2 * N2]
        kr = kf_ref[kk, 0].astype(F32)
        ki = kf_ref[kk, 1].astype(F32)
        y = jnp.concatenate([zr * kr - zi * ki, zr * ki + zi * kr], axis=0).astype(BF16)
        b = _dot(fi_ref[...], y)
        c = c_ref[kk]
        s = s_ref[kk]
        for cb in range(b.shape[1] // LANES):
            sl = slice(cb * LANES, (cb + 1) * LANES)
            br = b[0:N2, sl]
            bi = b[N2:2 * N2, sl]
            o_ref[0, 0, kk, :, sl] = br * c - bi * s
            o_ref[0, 1, kk, :, sl] = br * s + bi * c


def _conv_mid_h(a5, kf, order, tab, Ct=512):
    B, _, N2, _, C = a5.shape
    K1 = tab["K1"]
    G = T2_GROUP
    nkg = K1 // G
    nc = C // Ct
    sq = pl.BlockSpec((2 * N2, 2 * N2), lambda b, kg, cj: (0, 0))
    tw_spec = pl.BlockSpec((G, N2, LANES), lambda b, kg, cj: (kg, 0, 0))
    return pl.pallas_call(
        functools.partial(_conv_mid_h_kernel, N2=N2),
        grid=(B, nkg, nc),
        in_specs=[pl.BlockSpec((1, 1, N2, G, Ct), lambda b, kg, cj: (b, 0, 0, kg, cj)),
                  pl.BlockSpec((1, 1, N2, G, Ct), lambda b, kg, cj: (b, 0, 0, nkg + kg, cj)),
                  pl.BlockSpec((G, 2, N2, Ct), lambda b, kg, cj: (kg, 0, 0, order * nc + cj)),
                  sq, sq, tw_spec, tw_spec],
        out_specs=pl.BlockSpec((1, 2, G, N2, Ct), lambda b, kg, cj: (b, 0, kg, 0, cj)),
        out_shape=jax.ShapeDtypeStruct((B, 2, K1, N2, C), F32),
        compiler_params=_cp(("parallel", "parallel", "parallel")),
    )(a5, a5, kf, tab["f2"], tab["f2inv"], tab["twb_c"], tab["twb_s"])


def _conv_out_h_kernel(b_ref, f_ref, g_ref, z_ref, bias_ref, o_ref, *, K1):
    for j in range(T2_GROUP):
        bj = jnp.concatenate([b_ref[0, 0, :, j, :], b_ref[0, 1, :, j, :]], axis=0).astype(BF16)
        y = _dot(f_ref[...], bj)
        o_ref[0, :, j, :] = g_ref[0, :, j, :] * (y + z_ref[0, :, j, :] * bias_ref[...])


def _conv_out_h(b5, tab, uc4, g_blk, z4, z_blk, bias):
    B, _, K1, N2, C = b5.shape
    R = tab["R"]
    G = T2_GROUP
    return pl.pallas_call(
        functools.partial(_conv_out_h_kernel, K1=K1),
        grid=(B, N2 // G),
        in_specs=[pl.BlockSpec((1, 2, K1, G, C), lambda b, g: (b, 0, 0, g, 0)),
                  pl.BlockSpec((R, 2 * K1), lambda b, g: (0, 0)),
                  pl.BlockSpec((1, R, G, C), lambda b, g: (b, 0, g, g_blk)),
                  pl.BlockSpec((1, R, G, C), lambda b, g: (b, 0, g, z_blk)),
                  pl.BlockSpec((1, C), lambda b, g: (0, 0))],
        out_specs=pl.BlockSpec((1, R, G, C), lambda b, g: (b, 0, g, 0)),
        out_shape=jax.ShapeDtypeStruct((B, R, N2, C), F32),
        compiler_params=_cp(("parallel", "parallel")),
    )(b5, tab["f1inv"], uc4, z4, bias.reshape(1, C))


def _hyena_h(proj, conv_w, conv_b, w1, b1, w2, b2, w3, freq, bias):
    B, L, _ = proj.shape
    C = D_GROUP
    tab = _dft_tables_h(2 * L)
    N2, R = tab["N2"], tab["R"]
    uc4 = _dwconv3(proj, conv_w, conv_b).reshape(B, R, N2, 3 * C)
    hf, sums = _filter_mlp(L, w1, b1, w2, b2, w3, freq)
    s4 = sums.reshape(2, 2, C)
    inv_sum = (1.0 / (s4[:, 0] + s4[:, 1])).reshape(1, 2 * C)
    af = _fft1(hf.reshape(1, R, N2, 4 * C), tab, 0, 4)
    kf = _filter_fft2(af, tab, inv_sum)
    z4, z_blk = uc4, 0
    for o in range(2):
        a5 = _fft1(z4, tab, z_blk, 1)
        b5 = _conv_mid_h(a5, kf, o, tab)
        z4 = _conv_out_h(b5, tab, uc4, 1 + o, z4, z_blk, bias[o])
        z_blk = 0
    return z4.reshape(B, L, C)


def _router_kernel(x_ref, g_ref, rt_ref, h_ref, aff_ref):
    d = x_ref.shape[1]
    xv = x_ref[...]
    ms = jnp.mean(xv * xv, axis=-1, keepdims=True)
    h = xv * lax.rsqrt(ms + EPS) * g_ref[...]
    h_ref[:, 0:d] = h
    logits = lax.dot_general(rt_ref[...], h, (((1,), (1,)), ((), ())), precision=lax.Precision.HIGHEST,
                             preferred_element_type=F32)
    row = lax.broadcasted_iota(I32, logits.shape, 0)
    logits = jnp.where(row < N_EXPERTS, logits, -jnp.inf)
    e = jnp.exp(logits - jnp.max(logits, axis=0, keepdims=True))
    aff = e / jnp.sum(e, axis=0, keepdims=True)
    aff_ref[...] = aff[0:N_EXPERTS]
    h_ref[:, d:d + LANES] = aff.T


def _router(x, g, router_t, tm=512):
    n, d = x.shape
    rt = jnp.pad(router_t, ((0, LANES - N_EXPERTS), (0, 0)))
    return pl.pallas_call(
        _router_kernel,
        grid=(n // tm,),
        in_specs=[pl.BlockSpec((tm, d), lambda i: (i, 0)), pl.BlockSpec((1, d), lambda i: (0, 0)),
                  pl.BlockSpec((LANES, d), lambda i: (0, 0))],
        out_specs=[pl.BlockSpec((tm, d + LANES), lambda i: (i, 0)), pl.BlockSpec((N_EXPERTS, tm), lambda i: (0, i))],
        out_shape=[jax.ShapeDtypeStruct((n, d + LANES), F32), jax.ShapeDtypeStruct((N_EXPERTS, n), F32)],
        compiler_params=_cp(("parallel",)),
    )(x, g.reshape(1, d), rt)


def _select_kernel(aff_ref, pos_ref, cs_ref, m_s, u_s, *, n, cap, CH):
    E = N_EXPERTS
    bits = pltpu.bitcast(aff_ref[...], I32)

    def search(it, thr):
        cand = thr | jnp.left_shift(jnp.int32(1), 30 - it)
        cnt = jnp.sum((bits >= cand).astype(F32), axis=1, keepdims=True)
        return jnp.where(cnt >= cap, cand, thr)

    thr = lax.fori_loop(0, 31, search, jnp.zeros((E, 1), I32))
    need = cap - jnp.sum((bits > thr).astype(F32), axis=1, keepdims=True)
    r = lax.broadcasted_iota(I32, (CH, CH), 0)
    c = lax.broadcasted_iota(I32, (CH, CH), 1)
    u_s[...] = (r <= c).astype(BF16)

    def chunk_bits(j):
        off = pl.multiple_of(j * CH, CH)
        return off, pltpu.bitcast(aff_ref[:, pl.ds(off, CH)], I32)

    def ties(j, carry):
        off, b = chunk_bits(j)
        eq = (b == thr).astype(F32)
        incl = _dot(eq.astype(BF16), u_s[...]) + carry
        sel = (b > thr) | ((b == thr) & (incl - eq < need))
        m_s[:, pl.ds(off, CH)] = sel.astype(F32)
        return incl[:, CH - 1:CH]

    lax.fori_loop(0, n // CH, ties, jnp.zeros((E, 1), F32))

    def slots(j, carry):
        off, _ = chunk_bits(j)
        m = m_s[:, pl.ds(off, CH)]
        incl = _dot(m.astype(BF16), u_s[...]) + carry
        cs_ref[:, pl.ds(off, CH)] = incl.astype(I32)
        pos_ref[:, pl.ds(off, CH)] = jnp.where(m > 0.0, incl - 1.0, -1.0).astype(I32)
        return incl[:, CH - 1:CH]

    lax.fori_loop(0, n // CH, slots, jnp.zeros((E, 1), F32))


def _select(aff, cap, CH=512):
    E, n = aff.shape
    CH = min(CH, n)
    full = pl.BlockSpec((E, n), lambda: (0, 0))
    return pl.pallas_call(
        functools.partial(_select_kernel, n=n, cap=cap, CH=CH),
        in_specs=[full],
        out_specs=[full, full, full],
        out_shape=[jax.ShapeDtypeStruct((E, n), I32), jax.ShapeDtypeStruct((E, n), I32),
                   jax.ShapeDtypeStruct((E, n), F32)],
        scratch_shapes=[pltpu.VMEM((CH, CH), BF16)],
        compiler_params=_cp(None),
    )(aff)


def _compact_kernel(m_ref, idx_ref, *, cap, RB):
    m = m_ref[0]
    nch = m.shape[0]
    r = lax.broadcasted_iota(I32, (LANES, LANES), 0)
    c = lax.broadcasted_iota(I32, (LANES, LANES), 1)
    local = _dot(m, (r <= c).astype(F32))
    tot = _dot(m, (r >= 0).astype(F32))
    rr = lax.broadcasted_iota(I32, (nch, nch), 0)
    cc = lax.broadcasted_iota(I32, (nch, nch), 1)
    cend = _dot((cc <= rr).astype(F32), tot)
    cend_row = cend.T[0:1, :]
    cstart_row = cend_row - tot.T[0:1, :]
    chunk_id = lax.broadcasted_iota(I32, (RB, nch), 1).astype(F32)

    def block(b, carry):
        j0 = pl.multiple_of(b * RB, RB)
        slot = (j0 + lax.broadcasted_iota(I32, (RB, 1), 0)).astype(F32)
        cstar = jnp.sum((slot >= cend_row).astype(F32), axis=1, keepdims=True)
        oh = chunk_id == cstar
        counts = _dot(jnp.where(oh, 1.0, 0.0), local)
        first = jnp.sum(jnp.where(oh, cstart_row, 0.0), axis=1, keepdims=True)
        within = jnp.sum((counts <= slot - first).astype(F32), axis=1, keepdims=True)
        tok = jnp.broadcast_to(LANES * cstar + within, (RB, LANES)).T
        idx_ref[0, :, pl.ds(j0, RB)] = tok[0:8].astype(I32)
        return carry

    lax.fori_loop(0, cap // RB, block, 0)


def _compact(mask, n, cap, RB=512):
    E = N_EXPERTS
    nch = n // LANES
    m3 = mask.reshape(E, nch, LANES)
    if nch < LANES:
        m3 = jnp.pad(m3, ((0, 0), (0, LANES - nch), (0, 0)))
        nch = LANES
    RB = min(RB, cap)
    idx8 = pl.pallas_call(
        functools.partial(_compact_kernel, cap=cap, RB=RB),
        grid=(E,),
        in_specs=[pl.BlockSpec((1, nch, LANES), lambda e: (e, 0, 0))],
        out_specs=pl.BlockSpec((1, 8, cap), lambda e: (e, 0, 0)),
        out_shape=jax.ShapeDtypeStruct((E, 8, cap), I32),
        compiler_params=_cp(("parallel",)),
    )(m3)
    return idx8[:, 0, :].reshape(-1)


def _expert_kernel(idx_ref, nidx_ref, h_ref, wg_ref, wu_ref, wd_ref, o_ref, xbuf, xb, acc, sems,
                   *, tm, nf, d, nrt):
    e = pl.program_id(0)
    r = pl.program_id(1)
    f = pl.program_id(2)
    q = e * nrt + r
    slot = lax.rem(q, 2)
    last = q == N_EXPERTS * nrt - 1
    per = tm // nf

    def row_copy(t, s, j):
        return pltpu.make_async_copy(h_ref.at[pl.ds(t, 1)], xbuf.at[s, pl.ds(j, 1)], sems.at[s])

    def tile_wait(s):
        pltpu.make_async_copy(h_ref.at[pl.ds(0, tm)], xbuf.at[s, :, :], sems.at[s]).wait()

    @pl.when((q == 0) & (f == 0))
    def _():
        def issue(j, carry):
            row_copy(idx_ref[j], 0, j).start()
            return carry

        lax.fori_loop(0, tm, issue, 0, unroll=8)

    @pl.when(f == 0)
    def _():
        tile_wait(slot)
        xb[...] = xbuf[slot, :, 0:d].astype(BF16)
        acc[...] = jnp.zeros_like(acc)

    for u in range(per):
        j = f * per + u
        row_copy(nidx_ref[r * tm + j], 1 - slot, j).start()

    x = xb[...]
    g = _dot(x, wg_ref[0, 0])
    u_ = _dot(x, wu_ref[0, 0])
    hid = (g * _sigmoid(g) * u_).astype(BF16)
    acc[...] += _dot(hid, wd_ref[0, 0])

    @pl.when(f == nf - 1)
    def _():
        aff = xbuf[slot, :, d:d + LANES]
        lane = lax.broadcasted_iota(I32, aff.shape, 1)
        gate = jnp.sum(jnp.where(lane == e, aff, 0.0), axis=1, keepdims=True)
        o_ref[0] = (acc[...] * gate).astype(BF16)

    @pl.when(last & (f == nf - 1))
    def _():
        tile_wait(1 - slot)


def _experts(idx_flat, h, wg, wu, wd, layer, cap, tm=1024, tf=512):
    n, dx = h.shape
    _, E, d, dff = wg.shape
    tm = min(tm, cap)
    nf = dff // tf
    nrt = cap // tm
    return pl.pallas_call(
        functools.partial(_expert_kernel, tm=tm, nf=nf, d=d, nrt=nrt),
        grid=(E, nrt, nf),
        in_specs=[pl.BlockSpec((cap,), lambda e, r, f: (e,), memory_space=pltpu.SMEM),
                  pl.BlockSpec((cap,), lambda e, r, f: (e,), memory_space=pltpu.SMEM),
                  pl.BlockSpec(memory_space=pl.ANY),
                  pl.BlockSpec((1, 1, d, tf), lambda e, r, f: (layer, e, 0, f)),
                  pl.BlockSpec((1, 1, d, tf), lambda e, r, f: (layer, e, 0, f)),
                  pl.BlockSpec((1, 1, tf, d), lambda e, r, f: (layer, e, f, 0))],
        out_specs=pl.BlockSpec((1, tm, d), lambda e, r, f: (e, r, 0)),
        out_shape=jax.ShapeDtypeStruct((E, cap, d), BF16),
        scratch_shapes=[pltpu.VMEM((2, tm, dx), F32), pltpu.VMEM((tm, d), BF16), pltpu.VMEM((tm, d), F32),
                        pltpu.SemaphoreType.DMA((2,))],
        compiler_params=_cp(("arbitrary", "arbitrary", "arbitrary")),
    )(idx_flat, jnp.roll(idx_flat, -tm), h, wg, wu, wd)


COMBINE_WIN_LOG2 = 6
COMBINE_WIN = 1 << COMBINE_WIN_LOG2
ROW_ALIGN_LOG2 = 4
ROW_ALIGN = 1 << ROW_ALIGN_LOG2


def _combine_kernel(st_ref, x_ref, pos_ref, ye_ref, o_ref, buf, sems, buf2, sem2, p_s, *, TT, cap, nt):
    i = pl.program_id(0)
    E = N_EXPERTS
    W = COMBINE_WIN

    def window(tile, e, w):
        s0 = st_ref[tile * E + e]
        lo = lax.shift_left(lax.shift_right_logical(s0, ROW_ALIGN_LOG2), ROW_ALIGN_LOG2) + W * w
        return jnp.minimum(lo, cap - W), lo

    def first_copy(tile, slot, e):
        a, _ = window(tile, e, 0)
        return pltpu.make_async_copy(ye_ref.at[e, pl.ds(pl.multiple_of(a, ROW_ALIGN), W)],
                                     buf.at[slot, pl.ds(e * W, W)], sems.at[slot, e])

    slot = lax.rem(i, 2)

    @pl.when(i == 0)
    def _():
        for e in range(E):
            first_copy(0, 0, e).start()

    @pl.when(i + 1 < nt)
    def _():
        for e in range(E):
            first_copy(i + 1, 1 - slot, e).start()

    lane = lax.broadcasted_iota(I32, (TT, W), 1)

    def onehot(pc, a, lo):
        hit = (pc - a == lane) & (pc >= lo) & (pc < lo + W)
        return jnp.where(hit, 1.0, 0.0).astype(BF16)

    lane2 = lax.broadcasted_iota(I32, (TT, 2 * W), 1)
    first = lane2 < W
    for e in range(0, E, 2):
        a0, lo0 = window(i, e, 0)
        a1, lo1 = window(i, e + 1, 0)
        pc = jnp.where(first, pos_ref[:, e:e + 1], pos_ref[:, e + 1:e + 2])
        a = jnp.where(first, a0, a1 - W)
        lo = jnp.where(first, lo0, lo1)
        hit = (pc - a == lane2) & (pc >= lo) & (pc < lo + W)
        p_s[:, e * W:(e + 2) * W] = jnp.where(hit, 1.0, 0.0).astype(BF16)
    for e in range(E):
        first_copy(i, slot, e).wait()
    o_ref[...] = x_ref[...] + _dot(p_s[...], buf[slot])

    for e in range(E):
        _, lo = window(i, e, 0)
        s1 = st_ref[(i + 1) * E + e]
        nwin = lax.shift_right_logical(s1 - lo + (W - 1), COMBINE_WIN_LOG2)

        def extra(w, carry):
            a2, lo2 = window(i, e, w)
            cp = pltpu.make_async_copy(ye_ref.at[e, pl.ds(pl.multiple_of(a2, ROW_ALIGN), W)], buf2, sem2)
            cp.start()
            cp.wait()
            o_ref[...] += _dot(onehot(pos_ref[:, e:e + 1], a2, lo2), buf2[...])
            return carry

        lax.fori_loop(1, nwin, extra, 0)


def _combine(starts, x, pos_t, ye, cap, TT=256):
    n, d = x.shape
    E = N_EXPERTS
    TT = min(TT, n)
    nt = n // TT
    grid_spec = pltpu.PrefetchScalarGridSpec(
        num_scalar_prefetch=1,
        grid=(nt,),
        in_specs=[pl.BlockSpec((TT, d), lambda i, st: (i, 0)),
                  pl.BlockSpec((TT, E), lambda i, st: (i, 0)),
                  pl.BlockSpec(memory_space=pl.ANY)],
        out_specs=pl.BlockSpec((TT, d), lambda i, st: (i, 0)),
        scratch_shapes=[pltpu.VMEM((2, E * COMBINE_WIN, d), BF16), pltpu.SemaphoreType.DMA((2, E)),
                        pltpu.VMEM((COMBINE_WIN, d), BF16), pltpu.SemaphoreType.DMA(()),
                        pltpu.VMEM((TT, E * COMBINE_WIN), BF16)],
    )
    return pl.pallas_call(
        functools.partial(_combine_kernel, TT=TT, cap=cap, nt=nt),
        grid_spec=grid_spec,
        out_shape=jax.ShapeDtypeStruct((n, d), F32),
        compiler_params=_cp(("arbitrary",)),
    )(starts, x, pos_t, ye)


def _moe(x, g, router, wg, wu, wd, layer):
    n, d = x.shape
    E = N_EXPERTS
    cap = EC_CAPACITY * n // E
    TT = min(256, n)
    h, aff = _router(x, g, router.T)
    pos, cs, mask = _select(aff, cap)
    idx = _compact(mask, n, cap)
    ye = _experts(idx, h, wg, wu, wd, layer, cap)
    starts = jnp.concatenate([jnp.zeros((E, 1), I32), cs[:, TT - 1::TT]], axis=1).T.reshape(-1)
    return _combine(starts, x, pos.T, ye, cap, TT=TT)


def _rmsnorm_kernel(x_ref, g_ref, o_ref):
    xv = x_ref[...]
    o_ref[...] = xv * lax.rsqrt(jnp.mean(xv * xv, axis=-1, keepdims=True) + EPS) * g_ref[...]


def _rmsnorm(x, g, tm=512):
    n, d = x.shape
    return pl.pallas_call(
        _rmsnorm_kernel,
        grid=(n // tm,),
        in_specs=[pl.BlockSpec((tm, d), lambda i: (i, 0)), pl.BlockSpec((1, d), lambda i: (0, 0))],
        out_specs=pl.BlockSpec((tm, d), lambda i: (i, 0)),
        out_shape=jax.ShapeDtypeStruct((n, d), F32),
        compiler_params=_cp(("parallel",)),
    )(x, g.reshape(1, d))


def _even_mixer(x, B, L, p):
    proj = _rms_matmul(x, p["norm"], p["w_in"]).reshape(B, L, 5 * D_GROUP)
    hf = _lru(proj, None, p["conv_w"], p["conv_b"], p["wcat"][0], p["ba"][0], p["bx"][0], p["lam"][0],
              reverse=False)
    a_out = _lru(proj, hf, p["conv_w"], p["conv_b"], p["wcat"][1], p["ba"][1], p["bx"][1], p["lam"][1],
                 reverse=True)
    inv_freq = ROPE_THETA ** (-jnp.arange(0, DIFF_DH, 2, dtype=F32) / DIFF_DH)
    tabs = _rope_tables(L, DIFF_DH, inv_freq)
    half = DIFF_DH // 2
    qr = _rope(proj, 2, tabs, sh_a=LANES - half, sh_b=half, scale=DIFF_DH ** -0.5 * math.log2(math.e))
    kr = _rope(proj, 3, tabs, sh_a=LANES - half, sh_b=half, scale=1.0)
    vt = _v_transpose(proj, 4)
    b_out = _diff_attention(qr, kr, vt, p["diff_lam"], p["subln"], p["lam_init"])
    return _out_matmul(a_out.reshape(B * L, D_GROUP), b_out.reshape(B * L, D_GROUP), p["w_out"], x)


def _odd_mixer(x, B, L, p):
    proj = _rms_matmul(x, p["norm"], p["w_in"]).reshape(B, L, 7 * D_GROUP)
    c_out = _hyena_h(proj, p["conv_w"], p["conv_b"], p["w1"], p["b1"], p["w2"], p["b2"], p["w3"], p["freq"],
                   p["bias"])
    inv_freq = 1.0 / (10000.0 ** jnp.linspace(0.0, 1.0, RET_DH // 2, dtype=F32))
    tabs = _rope_tables(L, RET_DH, inv_freq)
    half = RET_DH // 2
    qr = _rope(proj, 3, tabs, sh_a=LANES - half, sh_b=half, scale=1.0)
    kr = _rope(proj, 4, tabs, sh_a=LANES - half, sh_b=half, scale=RET_DH ** -0.5)
    d_out = _retention(proj, qr, kr, p["rho"], 5 * N_HEAD_BLOCKS, 6 * N_HEAD_BLOCKS)
    return _out_matmul(c_out.reshape(B * L, D_GROUP), d_out.reshape(B * L, D_GROUP), p["w_out"], x)


def _trunk(x3, layers, final_norm):
    B, L, D = x3.shape
    x = x3.reshape(B * L, D)
    for kind, mp, ep in layers:
        x = (_even_mixer if kind == "even" else _odd_mixer)(x, B, L, mp)
        x = _moe(x, ep["norm"], ep["router"], ep["wg"], ep["wu"], ep["wd"], ep["layer"])
    return _rmsnorm(x, final_norm).reshape(B, L, D)


def kernel(x_prompt, x_sample, ev_norm, ev_w_in, ev_conv_w, ev_conv_b, ev_rg_wa, ev_rg_ba, ev_rg_wx, ev_rg_bx, ev_rg_lam, ev_diff_lam, ev_subln, ev_w_out, od_norm, od_w_in, od_conv_w, od_conv_b, od_flt_w1, od_flt_b1, od_flt_w2, od_flt_b2, od_flt_w3, od_flt_freq, od_flt_bias, od_ret_rho, od_w_out, moe_norm, moe_router, moe_w_gate, moe_w_up, moe_w_down, final_norm):
    depth = moe_norm.shape[0]
    wg_bf, wu_bf, wd_bf = moe_w_gate.astype(BF16), moe_w_up.astype(BF16), moe_w_down.astype(BF16)
    layers = []
    for layer in range(depth):
        j = layer // 2
        if layer % 2 == 0:
            mp = dict(norm=ev_norm[j], w_in=ev_w_in[j].astype(BF16), conv_w=ev_conv_w[j], conv_b=ev_conv_b[j],
                      wcat=jnp.concatenate([ev_rg_wa[j], ev_rg_wx[j]], axis=-1).astype(BF16),
                      ba=ev_rg_ba[j], bx=ev_rg_bx[j], lam=ev_rg_lam[j], diff_lam=ev_diff_lam[j],
                      subln=ev_subln[j], w_out=ev_w_out[j].astype(BF16),
                      lam_init=0.8 - 0.6 * math.exp(-0.3 * layer))
            kind = "even"
        else:
            mp = dict(norm=od_norm[j], w_in=od_w_in[j].astype(BF16), conv_w=od_conv_w[j], conv_b=od_conv_b[j],
                      w1=od_flt_w1[j], b1=od_flt_b1[j], w2=od_flt_w2[j], b2=od_flt_b2[j], w3=od_flt_w3[j],
                      freq=od_flt_freq[j], bias=od_flt_bias[j], rho=od_ret_rho[j],
                      w_out=od_w_out[j].astype(BF16))
            kind = "odd"
        ep = dict(norm=moe_norm[layer], router=moe_router[layer], wg=wg_bf, wu=wu_bf, wd=wd_bf, layer=layer)
        layers.append((kind, mp, ep))
    return (_trunk(x_prompt, layers, final_norm), _trunk(x_sample, layers, final_norm))
```

```python
import functools
import math

import jax
import jax.numpy as jnp
import numpy as np
from jax import lax
from jax.experimental import pallas as pl
from jax.experimental.pallas import tpu as pltpu

F32 = jnp.float32
BF16 = jnp.bfloat16
I32 = jnp.int32

D_MODEL = 2048
D_GROUP = 1024
LANES = 128
N_HEAD_BLOCKS = D_GROUP // LANES
LRU_C = 8.0
DIFF_DH = 64
ROPE_THETA = 10000.0
RET_DH = 128
HY_EMB = 33
HY_FFN = 64
HY_FAST_DECAY = 0.3
HY_SLOW_DECAY = 1.5
HY_TARGET = 1e-2
N_EXPERTS = 16
EC_CAPACITY = 2
EPS = 1e-6
FFT_N2 = 128
HALO = 8
VMEM_LIMIT_MB = 56


def _cp(sem, vmem_mb=VMEM_LIMIT_MB):
    return pltpu.CompilerParams(dimension_semantics=sem, vmem_limit_bytes=vmem_mb * 1024 * 1024)


def _sigmoid(x):
    return 1.0 / (1.0 + jnp.exp(-x))


def _softplus(x):
    return jnp.maximum(x, 0.0) + jnp.log(1.0 + jnp.exp(-jnp.abs(x)))


def _dot(a, b):
    return jnp.dot(a, b, preferred_element_type=F32)


def _dot_nt(a, b):
    return lax.dot_general(a, b, (((1,), (1,)), ((), ())), preferred_element_type=F32)


def _rms_matmul_kernel(x_ref, g_ref, w_ref, o_ref, h_ref):
    @pl.when(pl.program_id(1) == 0)
    def _():
        xv = x_ref[...]
        ms = jnp.mean(xv * xv, axis=-1, keepdims=True)
        h_ref[...] = (xv * lax.rsqrt(ms + EPS) * g_ref[...]).astype(BF16)

    o_ref[...] = _dot(h_ref[...], w_ref[...])


def _rms_matmul(x, g, w_bf, tm=1024, tn=1024):
    n, d = x.shape
    tm = min(tm, n)
    nout = w_bf.shape[1]
    return pl.pallas_call(
        _rms_matmul_kernel,
        grid=(n // tm, nout // tn),
        in_specs=[
            pl.BlockSpec((tm, d), lambda i, j: (i, 0)),
            pl.BlockSpec((1, d), lambda i, j: (0, 0)),
            pl.BlockSpec((d, tn), lambda i, j: (0, j)),
        ],
        out_specs=pl.BlockSpec((tm, tn), lambda i, j: (i, j)),
        out_shape=jax.ShapeDtypeStruct((n, nout), F32),
        scratch_shapes=[pltpu.VMEM((tm, d), BF16)],
        compiler_params=_cp(("parallel", "arbitrary")),
    )(x, g.reshape(1, d), w_bf)


def _out_matmul_kernel(a_ref, b_ref, wa_ref, wb_ref, x_ref, o_ref):
    o_ref[...] = (x_ref[...] + _dot(a_ref[...].astype(BF16), wa_ref[...])
                  + _dot(b_ref[...].astype(BF16), wb_ref[...]))


def _out_matmul(a, b, w_bf, x, tm=1024, tn=1024):
    n, d = x.shape
    tm = min(tm, n)
    dg = a.shape[1]
    return pl.pallas_call(
        _out_matmul_kernel,
        grid=(n // tm, d // tn),
        in_specs=[
            pl.BlockSpec((tm, dg), lambda i, j: (i, 0)),
            pl.BlockSpec((tm, dg), lambda i, j: (i, 0)),
            pl.BlockSpec((dg, tn), lambda i, j: (0, j)),
            pl.BlockSpec((dg, tn), lambda i, j: (1, j)),
            pl.BlockSpec((tm, tn), lambda i, j: (i, j)),
        ],
        out_specs=pl.BlockSpec((tm, tn), lambda i, j: (i, j)),
        out_shape=jax.ShapeDtypeStruct((n, d), F32),
        compiler_params=_cp(("parallel", "parallel")),
    )(a, b, w_bf, w_bf, x)


def _lru_kernel(*refs, reverse, T, nt):
    if reverse:
        (x_ref, prev_ref, next_ref, cw_ref, cb_ref, w_ref, ba_ref, bx_ref, lam_ref,
         gate_ref, hf_ref, o_ref, a_s, b_s, carry, h_s) = refs
    else:
        (x_ref, prev_ref, next_ref, cw_ref, cb_ref, w_ref, ba_ref, bx_ref, lam_ref,
         o_ref, a_s, b_s, carry) = refs
    i = pl.program_id(1)
    ti = (nt - 1 - i) if reverse else i

    @pl.when(i == 0)
    def _():
        carry[...] = jnp.zeros_like(carry)

    x = x_ref[0]
    prev = jnp.where(ti == 0, 0.0, prev_ref[0])
    nxt = jnp.where(ti == nt - 1, 0.0, next_ref[0])
    xe = jnp.concatenate([prev, x, nxt], axis=0)
    cw = cw_ref[...]
    xc = (cb_ref[...] + cw[0:1] * xe[HALO - 2:HALO - 2 + T] + cw[1:2] * xe[HALO - 1:HALO - 1 + T]
          + cw[2:3] * xe[HALO:HALO + T] + cw[3:4] * xe[HALO + 1:HALO + 1 + T])
    sp = _softplus(-lam_ref[...])
    for blk in range(N_HEAD_BLOCKS):
        sl = slice(blk * LANES, (blk + 1) * LANES)
        xb = xc[:, sl]
        ri = _dot(xb.astype(BF16), w_ref[blk])
        r = _sigmoid(ri[:, :LANES] + ba_ref[:, sl])
        ig = _sigmoid(ri[:, LANES:] + bx_ref[:, sl])
        a = jnp.exp(-LRU_C * r * sp[:, sl])
        a_s[:, sl] = a
        b_s[:, sl] = jnp.sqrt(1.0 - a * a) * (ig * xb)

    dst = h_s if reverse else o_ref.at[0]

    def body(k, h):
        t = (T - 1 - k) if reverse else k
        h = a_s[pl.ds(t, 1), :] * h + b_s[pl.ds(t, 1), :]
        dst[pl.ds(t, 1), :] = h
        return h

    carry[...] = lax.fori_loop(0, T, body, carry[...], unroll=8)

    if reverse:
        g = gate_ref[0]
        gelu = 0.5 * g * (1.0 + jnp.tanh(math.sqrt(2.0 / math.pi) * (g + 0.044715 * g * g * g)))
        o_ref[0] = ((hf_ref[0] + h_s[...]) * gelu).astype(BF16)


def _lru(proj, hf, conv_w, conv_b, wcat, ba, bx, lam, *, reverse, T=512):
    B, L, _ = proj.shape
    T = min(T, L)
    nt = L // T
    hb = T // HALO
    nh = L // HALO

    def tile(i):
        return (nt - 1 - i) if reverse else i

    x_spec = pl.BlockSpec((1, T, D_GROUP), lambda b, i: (b, tile(i), 0))
    prev_spec = pl.BlockSpec((1, HALO, D_GROUP), lambda b, i: (b, jnp.maximum(tile(i) * hb - 1, 0), 0))
    next_spec = pl.BlockSpec((1, HALO, D_GROUP), lambda b, i: (b, jnp.minimum((tile(i) + 1) * hb, nh - 1), 0))
    vec = pl.BlockSpec((1, D_GROUP), lambda b, i: (0, 0))
    in_specs = [x_spec, prev_spec, next_spec,
                pl.BlockSpec((4, D_GROUP), lambda b, i: (0, 0)), vec,
                pl.BlockSpec((N_HEAD_BLOCKS, LANES, 2 * LANES), lambda b, i: (0, 0, 0)), vec, vec, vec]
    args = [proj, proj, proj, conv_w, conv_b.reshape(1, -1), wcat, ba.reshape(1, -1), bx.reshape(1, -1),
            lam.reshape(1, -1)]
    scratch = [pltpu.VMEM((T, D_GROUP), F32), pltpu.VMEM((T, D_GROUP), F32), pltpu.VMEM((1, D_GROUP), F32)]
    if reverse:
        in_specs += [pl.BlockSpec((1, T, D_GROUP), lambda b, i: (b, tile(i), 1)),
                     pl.BlockSpec((1, T, D_GROUP), lambda b, i: (b, tile(i), 0))]
        args += [proj, hf]
        scratch += [pltpu.VMEM((T, D_GROUP), F32)]
        out_dtype = BF16
    else:
        out_dtype = F32
    return pl.pallas_call(
        functools.partial(_lru_kernel, reverse=reverse, T=T, nt=nt),
        grid=(B, nt),
        in_specs=in_specs,
        out_specs=pl.BlockSpec((1, T, D_GROUP), lambda b, i: (b, tile(i), 0)),
        out_shape=jax.ShapeDtypeStruct((B, L, D_GROUP), out_dtype),
        scratch_shapes=scratch,
        compiler_params=_cp(("parallel", "arbitrary")),
    )(*args)


def _rope_kernel(x_ref, c_ref, sa_ref, sb_ref, o_ref, *, sh_a, sh_b, scale):
    c = c_ref[...]
    sa = sa_ref[...]
    sb = sb_ref[...]
    for h in range(N_HEAD_BLOCKS):
        sl = slice(h * LANES, (h + 1) * LANES)
        x = x_ref[0, :, sl]
        y = x * c + pltpu.roll(x, sh_a, 1) * sa + pltpu.roll(x, sh_b, 1) * sb
        o_ref[0, :, sl] = (y * scale).astype(o_ref.dtype)


def _rope(proj, col_block, tabs, *, sh_a, sh_b, scale, T=512):
    B, L, _ = proj.shape
    T = min(T, L)
    tab_spec = pl.BlockSpec((T, LANES), lambda b, i: (i, 0))
    return pl.pallas_call(
        functools.partial(_rope_kernel, sh_a=sh_a, sh_b=sh_b, scale=scale),
        grid=(B, L // T),
        in_specs=[pl.BlockSpec((1, T, D_GROUP), lambda b, i: (b, i, col_block)), tab_spec, tab_spec, tab_spec],
        out_specs=pl.BlockSpec((1, T, D_GROUP), lambda b, i: (b, i, 0)),
        out_shape=jax.ShapeDtypeStruct((B, L, D_GROUP), BF16),
        compiler_params=_cp(("parallel", "parallel")),
    )(proj, *tabs)


def _rope_tables(L, dh, inv_freq):
    lane = np.arange(LANES)
    d = lane % dh
    half = dh // 2
    ang = jnp.arange(L, dtype=F32)[:, None] * inv_freq[d % half][None, :]
    cos = jnp.cos(ang)
    sin = jnp.sin(ang)
    lo = jnp.asarray(d < half)[None, :]
    return cos, jnp.where(lo, -sin, 0.0), jnp.where(lo, 0.0, sin)


VT_ROWS = LANES + 16
MAX_EAGER_JUMP = 60.0


def _vt_kernel(v_ref, o_ref):
    T = v_ref.shape[1]
    for h in range(N_HEAD_BLOCKS):
        sl = slice(h * LANES, (h + 1) * LANES)
        o_ref[0, h, 0:LANES, :] = v_ref[0, :, sl].T.astype(BF16)
        o_ref[0, h, LANES:VT_ROWS, :] = jnp.ones((VT_ROWS - LANES, T), BF16)


def _v_transpose(proj, col_block, T=512):
    B, L, _ = proj.shape
    T = min(T, L)
    return pl.pallas_call(
        _vt_kernel,
        grid=(B, L // T),
        in_specs=[pl.BlockSpec((1, T, D_GROUP), lambda b, i: (b, i, col_block))],
        out_specs=pl.BlockSpec((1, N_HEAD_BLOCKS, VT_ROWS, T), lambda b, i: (b, 0, 0, i)),
        out_shape=jax.ShapeDtypeStruct((B, N_HEAD_BLOCKS, VT_ROWS, L), BF16),
        compiler_params=_cp(("parallel", "parallel")),
    )(proj)


def _attn_kernel(q_ref, k_ref, vt_ref, lv_ref, sg_ref, o_ref, q2_s, m_s, acc_s, *, tq, tks, nsub, nk, lam_init):
    ki = pl.program_id(3)

    @pl.when(ki == 0)
    def _():
        q = q_ref[0]
        lane = lax.broadcasted_iota(I32, q.shape, 1)
        zero = jnp.zeros_like(q)
        q2_s[0:tq, :] = jnp.where(lane < DIFF_DH, q, zero)
        q2_s[tq:2 * tq, :] = jnp.where(lane >= DIFF_DH, q, zero)
        m_s[...] = jnp.full_like(m_s, -jnp.inf)
        acc_s[...] = jnp.zeros_like(acc_s)

    def scores(j):
        return _dot_nt(k_ref[0, j * tks:(j + 1) * tks, :], q2_s[...])

    def softmax_pass(eager):
        m_in = m_s[...]
        acc_in = acc_s[...]
        jump = jnp.zeros_like(m_in)
        s_next = scores(0)
        for j in range(nsub):
            ks = slice(j * tks, (j + 1) * tks)
            s = s_next
            if j + 1 < nsub:
                s_next = scores(j + 1)
            m_prev = m_in if j == 0 else m_s[...]
            acc_prev = acc_in if j == 0 else acc_s[...]
            m_new = jnp.maximum(m_prev, jnp.max(s, axis=0, keepdims=True))
            if eager and j > 0:
                p = jnp.exp2(s - m_prev).astype(BF16)
                acc_s[...] = jnp.exp2(m_prev - m_new) * (acc_prev + _dot(vt_ref[0, 0, :, ks], p))
                jump = jnp.maximum(jump, m_new - m_prev)
            else:
                p = jnp.exp2(s - m_new).astype(BF16)
                acc_s[...] = jnp.exp2(m_prev - m_new) * acc_prev + _dot(vt_ref[0, 0, :, ks], p)
            m_s[...] = m_new
        return m_in, acc_in, jnp.max(jump)

    m_in, acc_in, jump = softmax_pass(True)

    @pl.when(jump > MAX_EAGER_JUMP)
    def _():
        m_s[...] = m_in
        acc_s[...] = acc_in
        softmax_pass(False)

    @pl.when(ki == nk - 1)
    def _():
        o = acc_s[0:LANES, :] / acc_s[LANES:LANES + 1, :]
        lv = lv_ref[...]
        lam = (jnp.exp(jnp.sum(lv[0:1] * lv[1:2], axis=1, keepdims=True))
               - jnp.exp(jnp.sum(lv[2:3] * lv[3:4], axis=1, keepdims=True)) + lam_init)
        od = o[:, 0:tq] - lam * o[:, tq:2 * tq]
        ms = jnp.mean(od * od, axis=0, keepdims=True)
        y = od * lax.rsqrt(ms + EPS) * sg_ref[...] * (1.0 - lam_init)
        o_ref[0] = y.T.astype(BF16)


def _diff_attention(qr, kr, vt, diff_lam, subln, lam_init, tq=512, tk=16384, tks=2048):
    B, L, _ = qr.shape
    tq = min(tq, L)
    tk = min(tk, L)
    tks = min(tks, tk)
    nq, nk = L // tq, L // tk
    return pl.pallas_call(
        functools.partial(_attn_kernel, tq=tq, tks=tks, nsub=tk // tks, nk=nk, lam_init=lam_init),
        grid=(B, N_HEAD_BLOCKS, nq, nk),
        in_specs=[
            pl.BlockSpec((1, tq, LANES), lambda b, h, qi, ki: (b, qi, h)),
            pl.BlockSpec((1, tk, LANES), lambda b, h, qi, ki: (b, ki, h)),
            pl.BlockSpec((1, 1, VT_ROWS, tk), lambda b, h, qi, ki: (b, h, 0, ki)),
            pl.BlockSpec((4, DIFF_DH), lambda b, h, qi, ki: (0, 0)),
            pl.BlockSpec((LANES, 1), lambda b, h, qi, ki: (0, 0)),
        ],
        out_specs=pl.BlockSpec((1, tq, LANES), lambda b, h, qi, ki: (b, qi, h)),
        out_shape=jax.ShapeDtypeStruct((B, L, D_GROUP), BF16),
        scratch_shapes=[pltpu.VMEM((2 * tq, LANES), BF16), pltpu.VMEM((1, 2 * tq), F32),
                        pltpu.VMEM((VT_ROWS, 2 * tq), F32)],
        compiler_params=_cp(("parallel", "parallel", "parallel", "arbitrary")),
    )(qr, kr, vt, diff_lam, subln.reshape(LANES, 1))


def _log_gamma(rho_ref):
    return -_softplus(-rho_ref[0, 0])


def _ret_state_kernel(kf_ref, vf_ref, kb_ref, vb_ref, rf_ref, rb_ref, sf_ref, sb_ref, sf_s, sb_s, *, C):
    n = pl.program_id(2)

    @pl.when(n == 0)
    def _():
        sf_s[...] = jnp.zeros_like(sf_s)
        sb_s[...] = jnp.zeros_like(sb_s)

    sf_ref[0, 0, 0] = sf_s[...].astype(BF16)
    sb_ref[0, 0, 0] = sb_s[...].astype(BF16)
    lgf = _log_gamma(rf_ref)
    lgb = _log_gamma(rb_ref)
    row = lax.broadcasted_iota(I32, (C, LANES), 0).astype(F32)
    kdf = kf_ref[0].astype(F32) * jnp.exp(lgf * (C - 1.0 - row))
    kdb = kb_ref[0].astype(F32) * jnp.exp(lgb * row)
    sf_s[...] = sf_s[...] * jnp.exp(lgf * C) + _dot(kdf.T.astype(BF16), vf_ref[0].astype(BF16))
    sb_s[...] = sb_s[...] * jnp.exp(lgb * C) + _dot(kdb.T.astype(BF16), vb_ref[0].astype(BF16))


def _ret_main_kernel(q_ref, k_ref, v_ref, g_ref, sf_ref, sb_ref, rf_ref, rb_ref, o_ref, *, C):
    lgf = _log_gamma(rf_ref)
    lgb = _log_gamma(rb_ref)
    q = q_ref[0]
    s = _dot_nt(q, k_ref[0])
    r = lax.broadcasted_iota(I32, (C, C), 0)
    c = lax.broadcasted_iota(I32, (C, C), 1)
    dist = (r - c).astype(F32)
    dmat = jnp.exp(jnp.where(r >= c, lgf * dist, -lgb * dist))
    inner = _dot((s * dmat).astype(BF16), v_ref[0].astype(BF16))
    row = lax.broadcasted_iota(I32, (C, LANES), 0).astype(F32)
    qf = q.astype(F32)
    qcat = jnp.concatenate([qf * jnp.exp(lgf * (row + 1.0)), qf * jnp.exp(lgb * (C - row))], axis=1)
    scat = jnp.concatenate([sf_ref[0, 0, 0], sb_ref[0, 0, 0]], axis=0)
    y = inner + _dot(qcat.astype(BF16), scat)
    y = y * lax.rsqrt(jnp.mean(y * y, axis=-1, keepdims=True) + EPS)
    g = g_ref[0]
    o_ref[0] = (y * (g * _sigmoid(g))).astype(BF16)


def _retention(proj, qr, kr, rho, v_blk0, g_blk0, C=512):
    B, L, _ = proj.shape
    H = N_HEAD_BLOCKS
    C = min(C, L)
    N = L // C
    rho4 = rho.reshape(2, H, 1, 1)
    rf_spec = pl.BlockSpec((1, 1, 1, 1), lambda b, h, n: (0, h, 0, 0))
    rb_spec = pl.BlockSpec((1, 1, 1, 1), lambda b, h, n: (1, h, 0, 0))
    st_shape = jax.ShapeDtypeStruct((B, H, N, LANES, LANES), BF16)
    sf, sb = pl.pallas_call(
        functools.partial(_ret_state_kernel, C=C),
        grid=(B, H, N),
        in_specs=[
            pl.BlockSpec((1, C, LANES), lambda b, h, n: (b, n, h)),
            pl.BlockSpec((1, C, LANES), lambda b, h, n: (b, n, v_blk0 + h)),
            pl.BlockSpec((1, C, LANES), lambda b, h, n: (b, N - 1 - n, h)),
            pl.BlockSpec((1, C, LANES), lambda b, h, n: (b, N - 1 - n, v_blk0 + h)),
            rf_spec, rb_spec,
        ],
        out_specs=[pl.BlockSpec((1, 1, 1, LANES, LANES), lambda b, h, n: (b, h, n, 0, 0)),
                   pl.BlockSpec((1, 1, 1, LANES, LANES), lambda b, h, n: (b, h, N - 1 - n, 0, 0))],
        out_shape=[st_shape, st_shape],
        scratch_shapes=[pltpu.VMEM((LANES, LANES), F32), pltpu.VMEM((LANES, LANES), F32)],
        compiler_params=_cp(("parallel", "parallel", "arbitrary")),
    )(kr, proj, kr, proj, rho4, rho4)
    st_spec = pl.BlockSpec((1, 1, 1, LANES, LANES), lambda b, h, n: (b, h, n, 0, 0))
    return pl.pallas_call(
        functools.partial(_ret_main_kernel, C=C),
        grid=(B, H, N),
        in_specs=[
            pl.BlockSpec((1, C, LANES), lambda b, h, n: (b, n, h)),
            pl.BlockSpec((1, C, LANES), lambda b, h, n: (b, n, h)),
            pl.BlockSpec((1, C, LANES), lambda b, h, n: (b, n, v_blk0 + h)),
            pl.BlockSpec((1, C, LANES), lambda b, h, n: (b, n, g_blk0 + h)),
            st_spec, st_spec, rf_spec, rb_spec,
        ],
        out_specs=pl.BlockSpec((1, C, LANES), lambda b, h, n: (b, n, h)),
        out_shape=jax.ShapeDtypeStruct((B, L, D_GROUP), BF16),
        compiler_params=_cp(("parallel", "parallel", "parallel")),
    )(qr, kr, proj, proj, sf, sb, rho4, rho4)


def _dwconv3_kernel(x_ref, prev_ref, next_ref, w_ref, b_ref, o_ref, *, T, nt):
    i = pl.program_id(1)
    x = x_ref[0]
    prev = jnp.where(i == 0, 0.0, prev_ref[0])
    nxt = jnp.where(i == nt - 1, 0.0, next_ref[0])
    xe = jnp.concatenate([prev, x, nxt], axis=0)
    w = w_ref[...]
    o_ref[0] = (b_ref[...] + w[0:1] * xe[HALO - 1:HALO - 1 + T] + w[1:2] * xe[HALO:HALO + T]
                + w[2:3] * xe[HALO + 1:HALO + 1 + T])


def _dwconv3(proj, conv_w, conv_b, T=512):
    B, L, _ = proj.shape
    T = min(T, L)
    nt = L // T
    hb = T // HALO
    nh = L // HALO
    W = 3 * D_GROUP
    return pl.pallas_call(
        functools.partial(_dwconv3_kernel, T=T, nt=nt),
        grid=(B, nt),
        in_specs=[
            pl.BlockSpec((1, T, W), lambda b, i: (b, i, 0)),
            pl.BlockSpec((1, HALO, W), lambda b, i: (b, jnp.maximum(i * hb - 1, 0), 0)),
            pl.BlockSpec((1, HALO, W), lambda b, i: (b, jnp.minimum((i + 1) * hb, nh - 1), 0)),
            pl.BlockSpec((3, W), lambda b, i: (0, 0)),
            pl.BlockSpec((1, W), lambda b, i: (0, 0)),
        ],
        out_specs=pl.BlockSpec((1, T, W), lambda b, i: (b, i, 0)),
        out_shape=jax.ShapeDtypeStruct((B, L, W), F32),
        compiler_params=_cp(("parallel", "parallel")),
    )(proj, proj, proj, conv_w, conv_b.reshape(1, W))


def _filter_mlp_kernel(z_ref, w1_ref, b1_ref, w2_ref, b2_ref, w3_ref, fr_ref, hf_ref, sum_ref, *, TL, L):
    i = pl.program_id(0)
    fr = fr_ref[...]
    h = jnp.sin(fr * (_dot(z_ref[...].astype(BF16), w1_ref[...]) + b1_ref[...]))
    for j in range(2):
        h = jnp.sin(fr * (_dot(h.astype(BF16), w2_ref[j]) + b2_ref[j]))
    hf = _dot(h.astype(BF16), w3_ref[...])
    row = i * TL + lax.broadcasted_iota(I32, (TL, D_GROUP), 0)
    t = row.astype(F32) / (L - 1.0)
    ch = lax.broadcasted_iota(I32, (TL, D_GROUP), 1).astype(F32)
    max_decay = math.log(HY_TARGET) / HY_FAST_DECAY
    min_decay = math.log(HY_TARGET) / HY_SLOW_DECAY
    delta = jnp.abs(min_decay + (max_decay - min_decay) * ch / (D_GROUP - 1.0))
    dec = jnp.exp(-t * delta)
    keep = row < L - 1

    @pl.when(i == 0)
    def _():
        sum_ref[...] = jnp.zeros_like(sum_ref)

    for q in range(4):
        sl = slice(q * D_GROUP, (q + 1) * D_GROUP)
        v = hf[:, sl] * dec
        if q % 2 == 1:
            v = jnp.where(keep, v, 0.0)
        hf_ref[:, sl] = v
        sum_ref[:, sl] += jnp.sum(jnp.abs(v), axis=0, keepdims=True)


def _filter_mlp(L, w1, b1, w2, b2, w3, freq, TL=512):
    TL = min(TL, L)
    t = jnp.linspace(0.0, 1.0, L, dtype=F32)[:, None]
    bands = (HY_EMB - 1) // 2
    wpos = 2.0 * math.pi * jnp.arange(L, dtype=F32)[:, None] / L
    f = jnp.linspace(1e-4, bands - 1, bands, dtype=F32)[None, :]
    z = jnp.concatenate([t, jnp.cos(f * wpos), jnp.sin(f * wpos)], axis=-1)
    z = jnp.pad(z, ((0, 0), (0, LANES - HY_EMB)))
    pf = LANES - HY_FFN
    w1p = jnp.pad(w1, ((0, LANES - HY_EMB), (0, pf))).astype(BF16)
    b1p = jnp.pad(b1, (0, pf)).reshape(1, LANES)
    w2p = jnp.pad(w2, ((0, 0), (0, pf), (0, pf))).astype(BF16)
    b2p = jnp.pad(b2, ((0, 0), (0, pf))).reshape(2, 1, LANES)
    w3p = jnp.pad(w3, ((0, pf), (0, 0))).astype(BF16)
    frp = jnp.pad(freq, (0, pf)).reshape(1, LANES)
    W = 4 * D_GROUP
    full = lambda *shape: pl.BlockSpec(shape, lambda i: (0,) * len(shape))
    return pl.pallas_call(
        functools.partial(_filter_mlp_kernel, TL=TL, L=L),
        grid=(L // TL,),
        in_specs=[pl.BlockSpec((TL, LANES), lambda i: (i, 0)), full(LANES, LANES), full(1, LANES),
                  full(2, LANES, LANES), full(2, 1, LANES), full(LANES, W), full(1, LANES)],
        out_specs=[pl.BlockSpec((TL, W), lambda i: (i, 0)), full(1, W)],
        out_shape=[jax.ShapeDtypeStruct((L, W), F32), jax.ShapeDtypeStruct((1, W), F32)],
        compiler_params=_cp(("arbitrary",)),
    )(z, w1p, b1p, w2p, b2p, w3p, frp)


T2_GROUP = 8
SLAB_PAD = 8


def _dft_tables_h(N):
    N2 = FFT_N2
    N1 = N // N2
    R = N1 // 2
    K1 = R + SLAB_PAD
    k1 = np.arange(K1)
    live = (k1 <= R).astype(np.float64)
    t1 = np.arange(R)
    a1 = 2.0 * math.pi * ((k1[:, None] * t1[None, :]) % N1) / N1
    f1 = np.concatenate([np.cos(a1) * live[:, None], -np.sin(a1) * live[:, None]], axis=0)
    wgt = np.where((k1 == 0) | (k1 == R), 1.0, 2.0) * live
    f1inv = np.concatenate([np.cos(a1) * wgt[:, None], -np.sin(a1) * wgt[:, None]], axis=0).T / N
    k2 = np.arange(N2)
    a2 = 2.0 * math.pi * ((k2[:, None] * k2[None, :]) % N2) / N2
    fr, fi = np.cos(a2), -np.sin(a2)
    at = 2.0 * math.pi * (k2[:, None] * k1[None, :]) / N
    ak = 2.0 * math.pi * (k1[:, None] + N1 * k2[None, :]) / N
    tw = lambda a: jnp.broadcast_to(jnp.asarray(a, F32)[:, :, None], a.shape + (LANES,))
    bf = lambda a: jnp.asarray(a, F32).astype(BF16)
    return dict(N1=N1, N2=N2, R=R, K1=K1, f1=bf(f1), f1inv=bf(f1inv),
                f2=bf(np.block([[fr, -fi], [fi, fr]])), f2inv=bf(np.block([[fr, fi], [-fi, fr]])),
                twa_c=tw(np.cos(at)), twa_s=tw(-np.sin(at)), twb_c=tw(np.cos(at.T)), twb_s=tw(np.sin(at.T)),
                wk_c=tw(np.cos(ak)), wk_s=tw(np.sin(ak)))


def _fft1_kernel(x_ref, f_ref, c_ref, s_ref, o_ref, *, K1, C):
    for j in range(T2_GROUP):
        a = _dot(f_ref[...], x_ref[0, :, j, :].astype(BF16))
        c = c_ref[j]
        s = s_ref[j]
        for cb in range(C // LANES):
            sl = slice(cb * LANES, (cb + 1) * LANES)
            ar = a[0:K1, sl]
            ai = a[K1:2 * K1, sl]
            o_ref[0, 0, j, 0:K1, sl] = ar * c - ai * s
            o_ref[0, 0, j, K1:2 * K1, sl] = ar * s + ai * c


def _fft1(x4, tab, col0, ncol):
    Bx, R, N2, _ = x4.shape
    K1 = tab["K1"]
    C = D_GROUP
    G = T2_GROUP
    return pl.pallas_call(
        functools.partial(_fft1_kernel, K1=K1, C=C),
        grid=(Bx, ncol, N2 // G),
        in_specs=[pl.BlockSpec((1, R, G, C), lambda b, ci, g: (b, 0, g, col0 + ci)),
                  pl.BlockSpec((2 * K1, R), lambda b, ci, g: (0, 0)),
                  pl.BlockSpec((G, K1, LANES), lambda b, ci, g: (g, 0, 0)),
                  pl.BlockSpec((G, K1, LANES), lambda b, ci, g: (g, 0, 0))],
        out_specs=pl.BlockSpec((1, 1, G, 2 * K1, C), lambda b, ci, g: (b, ci, g, 0, 0)),
        out_shape=jax.ShapeDtypeStruct((Bx, ncol, N2, 2 * K1, C), F32),
        compiler_params=_cp(("parallel", "parallel", "parallel")),
    )(x4, tab["f1"], tab["twa_c"], tab["twa_s"])


def _strided_cat(re_ref, im_ref, kk):
    return jnp.concatenate([re_ref[0, 0, :, kk, :], im_ref[0, 0, :, kk, :]], axis=0).astype(BF16)


def _filter_fft2_kernel(pr_ref, pi_ref, mr_ref, mi_ref, f_ref, c_ref, s_ref, inv_ref, o_ref, *, N2):
    for kk in range(T2_GROUP):
        zp = _dot(f_ref[...], _strided_cat(pr_ref, pi_ref, kk))
        zm = _dot(f_ref[...], _strided_cat(mr_ref, mi_ref, kk))
        c = c_ref[kk]
        s = s_ref[kk]
        inv = inv_ref[...]
        for cb in range(zp.shape[1] // LANES):
            sl = slice(cb * LANES, (cb + 1) * LANES)
            zmr, zmi = zm[0:N2, sl], zm[N2:2 * N2, sl]
            o_ref[kk, 0, :, sl] = ((zp[0:N2, sl] + c * zmr + s * zmi) * inv[:, sl]).astype(BF16)
            o_ref[kk, 1, :, sl] = ((zp[N2:2 * N2, sl] + s * zmr - c * zmi) * inv[:, sl]).astype(BF16)


def _filter_fft2(af, tab, inv_sum, Ct=512):
    _, _, N2, _, C = af.shape
    K1 = tab["K1"]
    G = T2_GROUP
    nkg = K1 // G
    nc = C // Ct

    def a_spec(side, im):
        return pl.BlockSpec((1, 1, N2, G, Ct), lambda kg, o, cj: (0, 2 * o + side, 0, im * nkg + kg, cj))

    tw_spec = pl.BlockSpec((G, N2, LANES), lambda kg, o, cj: (kg, 0, 0))
    return pl.pallas_call(
        functools.partial(_filter_fft2_kernel, N2=N2),
        grid=(nkg, 2, nc),
        in_specs=[a_spec(0, 0), a_spec(0, 1), a_spec(1, 0), a_spec(1, 1),
                  pl.BlockSpec((2 * N2, 2 * N2), lambda kg, o, cj: (0, 0)), tw_spec, tw_spec,
                  pl.BlockSpec((1, Ct), lambda kg, o, cj: (0, o * nc + cj))],
        out_specs=pl.BlockSpec((G, 2, N2, Ct), lambda kg, o, cj: (kg, 0, 0, o * nc + cj)),
        out_shape=jax.ShapeDtypeStruct((K1, 2, N2, 2 * C), BF16),
        compiler_params=_cp(("parallel", "parallel", "parallel")),
    )(af, af, af, af, tab["f2"], tab["wk_c"], tab["wk_s"], inv_sum)


def _conv_mid_h_kernel(ar_ref, ai_ref, kf_ref, f_ref, fi_ref, c_ref, s_ref, o_ref, *, N2):
    for kk in range(T2_GROUP):
        z = _dot(f_ref[...], _strided_cat(ar_ref, ai_ref, kk))
        zr, zi = z[0:N2], z[N2:2 * N2]
        kr = kf_ref[kk, 0].astype(F32)
        ki = kf_ref[kk, 1].astype(F32)
        y = jnp.concatenate([zr * kr - zi * ki, zr * ki + zi * kr], axis=0).astype(BF16)
        b = _dot(fi_ref[...], y)
        c = c_ref[kk]
        s = s_ref[kk]
        for cb in range(b.shape[1] // LANES):
            sl = slice(cb * LANES, (cb + 1) * LANES)
            br = b[0:N2, sl]
            bi = b[N2:2 * N2, sl]
            o_ref[0, 0, kk, :, sl] = br * c - bi * s
            o_ref[0, 1, kk, :, sl] = br * s + bi * c


def _conv_mid_h(a5, kf, order, tab, Ct=512):
    B, _, N2, _, C = a5.shape
    K1 = tab["K1"]
    G = T2_GROUP
    nkg = K1 // G
    nc = C // Ct
    sq = pl.BlockSpec((2 * N2, 2 * N2), lambda b, kg, cj: (0, 0))
    tw_spec = pl.BlockSpec((G, N2, LANES), lambda b, kg, cj: (kg, 0, 0))
    return pl.pallas_call(
        functools.partial(_conv_mid_h_kernel, N2=N2),
        grid=(B, nkg, nc),
        in_specs=[pl.BlockSpec((1, 1, N2, G, Ct), lambda b, kg, cj: (b, 0, 0, kg, cj)),
                  pl.BlockSpec((1, 1, N2, G, Ct), lambda b, kg, cj: (b, 0, 0, nkg + kg, cj)),
                  pl.BlockSpec((G, 2, N2, Ct), lambda b, kg, cj: (kg, 0, 0, order * nc + cj)),
                  sq, sq, tw_spec, tw_spec],
        out_specs=pl.BlockSpec((1, 2, G, N2, Ct), lambda b, kg, cj: (b, 0, kg, 0, cj)),
        out_shape=jax.ShapeDtypeStruct((B, 2, K1, N2, C), F32),
        compiler_params=_cp(("parallel", "parallel", "parallel")),
    )(a5, a5, kf, tab["f2"], tab["f2inv"], tab["twb_c"], tab["twb_s"])


def _conv_out_h_kernel(b_ref, f_ref, g_ref, z_ref, bias_ref, o_ref, *, K1):
    for j in range(T2_GROUP):
        bj = jnp.concatenate([b_ref[0, 0, :, j, :], b_ref[0, 1, :, j, :]], axis=0).astype(BF16)
        y = _dot(f_ref[...], bj)
        o_ref[0, :, j, :] = g_ref[0, :, j, :] * (y + z_ref[0, :, j, :] * bias_ref[...])


def _conv_out_h(b5, tab, uc4, g_blk, z4, z_blk, bias):
    B, _, K1, N2, C = b5.shape
    R = tab["R"]
    G = T2_GROUP
    return pl.pallas_call(
        functools.partial(_conv_out_h_kernel, K1=K1),
        grid=(B, N2 // G),
        in_specs=[pl.BlockSpec((1, 2, K1, G, C), lambda b, g: (b, 0, 0, g, 0)),
                  pl.BlockSpec((R, 2 * K1), lambda b, g: (0, 0)),
                  pl.BlockSpec((1, R, G, C), lambda b, g: (b, 0, g, g_blk)),
                  pl.BlockSpec((1, R, G, C), lambda b, g: (b, 0, g, z_blk)),
                  pl.BlockSpec((1, C), lambda b, g: (0, 0))],
        out_specs=pl.BlockSpec((1, R, G, C), lambda b, g: (b, 0, g, 0)),
        out_shape=jax.ShapeDtypeStruct((B, R, N2, C), F32),
        compiler_params=_cp(("parallel", "parallel")),
    )(b5, tab["f1inv"], uc4, z4, bias.reshape(1, C))


def _hyena_h(proj, conv_w, conv_b, w1, b1, w2, b2, w3, freq, bias):
    B, L, _ = proj.shape
    C = D_GROUP
    tab = _dft_tables_h(2 * L)
    N2, R = tab["N2"], tab["R"]
    uc4 = _dwconv3(proj, conv_w, conv_b).reshape(B, R, N2, 3 * C)
    hf, sums = _filter_mlp(L, w1, b1, w2, b2, w3, freq)
    s4 = sums.reshape(2, 2, C)
    inv_sum = (1.0 / (s4[:, 0] + s4[:, 1])).reshape(1, 2 * C)
    af = _fft1(hf.reshape(1, R, N2, 4 * C), tab, 0, 4)
    kf = _filter_fft2(af, tab, inv_sum)
    z4, z_blk = uc4, 0
    for o in range(2):
        a5 = _fft1(z4, tab, z_blk, 1)
        b5 = _conv_mid_h(a5, kf, o, tab)
        z4 = _conv_out_h(b5, tab, uc4, 1 + o, z4, z_blk, bias[o])
        z_blk = 0
    return z4.reshape(B, L, C)


def _router_kernel(x_ref, g_ref, rt_ref, h_ref, aff_ref):
    d = x_ref.shape[1]
    xv = x_ref[...]
    ms = jnp.mean(xv * xv, axis=-1, keepdims=True)
    h = xv * lax.rsqrt(ms + EPS) * g_ref[...]
    h_ref[:, 0:d] = h
    logits = lax.dot_general(rt_ref[...], h, (((1,), (1,)), ((), ())), precision=lax.Precision.HIGHEST,
                             preferred_element_type=F32)
    row = lax.broadcasted_iota(I32, logits.shape, 0)
    logits = jnp.where(row < N_EXPERTS, logits, -jnp.inf)
    e = jnp.exp(logits - jnp.max(logits, axis=0, keepdims=True))
    aff = e / jnp.sum(e, axis=0, keepdims=True)
    aff_ref[...] = aff[0:N_EXPERTS]
    h_ref[:, d:d + LANES] = aff.T


def _router(x, g, router_t, tm=512):
    n, d = x.shape
    rt = jnp.pad(router_t, ((0, LANES - N_EXPERTS), (0, 0)))
    return pl.pallas_call(
        _router_kernel,
        grid=(n // tm,),
        in_specs=[pl.BlockSpec((tm, d), lambda i: (i, 0)), pl.BlockSpec((1, d), lambda i: (0, 0)),
                  pl.BlockSpec((LANES, d), lambda i: (0, 0))],
        out_specs=[pl.BlockSpec((tm, d + LANES), lambda i: (i, 0)), pl.BlockSpec((N_EXPERTS, tm), lambda i: (0, i))],
        out_shape=[jax.ShapeDtypeStruct((n, d + LANES), F32), jax.ShapeDtypeStruct((N_EXPERTS, n), F32)],
        compiler_params=_cp(("parallel",)),
    )(x, g.reshape(1, d), rt)


def _select_kernel(aff_ref, pos_ref, cs_ref, m_s, u_s, *, n, cap, CH):
    E = N_EXPERTS
    bits = pltpu.bitcast(aff_ref[...], I32)

    def search(it, thr):
        cand = thr | jnp.left_shift(jnp.int32(1), 30 - it)
        cnt = jnp.sum((bits >= cand).astype(F32), axis=1, keepdims=True)
        return jnp.where(cnt >= cap, cand, thr)

    thr = lax.fori_loop(0, 31, search, jnp.zeros((E, 1), I32))
    need = cap - jnp.sum((bits > thr).astype(F32), axis=1, keepdims=True)
    r = lax.broadcasted_iota(I32, (CH, CH), 0)
    c = lax.broadcasted_iota(I32, (CH, CH), 1)
    u_s[...] = (r <= c).astype(BF16)

    def chunk_bits(j):
        off = pl.multiple_of(j * CH, CH)
        return off, pltpu.bitcast(aff_ref[:, pl.ds(off, CH)], I32)

    def ties(j, carry):
        off, b = chunk_bits(j)
        eq = (b == thr).astype(F32)
        incl = _dot(eq.astype(BF16), u_s[...]) + carry
        sel = (b > thr) | ((b == thr) & (incl - eq < need))
        m_s[:, pl.ds(off, CH)] = sel.astype(F32)
        return incl[:, CH - 1:CH]

    lax.fori_loop(0, n // CH, ties, jnp.zeros((E, 1), F32))

    def slots(j, carry):
        off, _ = chunk_bits(j)
        m = m_s[:, pl.ds(off, CH)]
        incl = _dot(m.astype(BF16), u_s[...]) + carry
        cs_ref[:, pl.ds(off, CH)] = incl.astype(I32)
        pos_ref[:, pl.ds(off, CH)] = jnp.where(m > 0.0, incl - 1.0, -1.0).astype(I32)
        return incl[:, CH - 1:CH]

    lax.fori_loop(0, n // CH, slots, jnp.zeros((E, 1), F32))


def _select(aff, cap, CH=512):
    E, n = aff.shape
    CH = min(CH, n)
    full = pl.BlockSpec((E, n), lambda: (0, 0))
    return pl.pallas_call(
        functools.partial(_select_kernel, n=n, cap=cap, CH=CH),
        in_specs=[full],
        out_specs=[full, full, full],
        out_shape=[jax.ShapeDtypeStruct((E, n), I32), jax.ShapeDtypeStruct((E, n), I32),
                   jax.ShapeDtypeStruct((E, n), F32)],
        scratch_shapes=[pltpu.VMEM((CH, CH), BF16)],
        compiler_params=_cp(None),
    )(aff)


def _compact_kernel(m_ref, idx_ref, *, cap, RB):
    m = m_ref[0]
    nch = m.shape[0]
    r = lax.broadcasted_iota(I32, (LANES, LANES), 0)
    c = lax.broadcasted_iota(I32, (LANES, LANES), 1)
    local = _dot(m, (r <= c).astype(F32))
    tot = _dot(m, (r >= 0).astype(F32))
    rr = lax.broadcasted_iota(I32, (nch, nch), 0)
    cc = lax.broadcasted_iota(I32, (nch, nch), 1)
    cend = _dot((cc <= rr).astype(F32), tot)
    cend_row = cend.T[0:1, :]
    cstart_row = cend_row - tot.T[0:1, :]
    chunk_id = lax.broadcasted_iota(I32, (RB, nch), 1).astype(F32)

    def block(b, carry):
        j0 = pl.multiple_of(b * RB, RB)
        slot = (j0 + lax.broadcasted_iota(I32, (RB, 1), 0)).astype(F32)
        cstar = jnp.sum((slot >= cend_row).astype(F32), axis=1, keepdims=True)
        oh = chunk_id == cstar
        counts = _dot(jnp.where(oh, 1.0, 0.0), local)
        first = jnp.sum(jnp.where(oh, cstart_row, 0.0), axis=1, keepdims=True)
        within = jnp.sum((counts <= slot - first).astype(F32), axis=1, keepdims=True)
        tok = jnp.broadcast_to(LANES * cstar + within, (RB, LANES)).T
        idx_ref[0, :, pl.ds(j0, RB)] = tok[0:8].astype(I32)
        return carry

    lax.fori_loop(0, cap // RB, block, 0)


def _compact(mask, n, cap, RB=512):
    E = N_EXPERTS
    nch = n // LANES
    m3 = mask.reshape(E, nch, LANES)
    if nch < LANES:
        m3 = jnp.pad(m3, ((0, 0), (0, LANES - nch), (0, 0)))
        nch = LANES
    RB = min(RB, cap)
    idx8 = pl.pallas_call(
        functools.partial(_compact_kernel, cap=cap, RB=RB),
        grid=(E,),
        in_specs=[pl.BlockSpec((1, nch, LANES), lambda e: (e, 0, 0))],
        out_specs=pl.BlockSpec((1, 8, cap), lambda e: (e, 0, 0)),
        out_shape=jax.ShapeDtypeStruct((E, 8, cap), I32),
        compiler_params=_cp(("parallel",)),
    )(m3)
    return idx8[:, 0, :].reshape(-1)


def _expert_kernel(idx_ref, nidx_ref, h_ref, wg_ref, wu_ref, wd_ref, o_ref, xbuf, xb, acc, sems,
                   *, tm, nf, d, nrt):
    e = pl.program_id(0)
    r = pl.program_id(1)
    f = pl.program_id(2)
    q = e * nrt + r
    slot = lax.rem(q, 2)
    last = q == N_EXPERTS * nrt - 1
    per = tm // nf

    def row_copy(t, s, j):
        return pltpu.make_async_copy(h_ref.at[pl.ds(t, 1)], xbuf.at[s, pl.ds(j, 1)], sems.at[s])

    def tile_wait(s):
        pltpu.make_async_copy(h_ref.at[pl.ds(0, tm)], xbuf.at[s, :, :], sems.at[s]).wait()

    @pl.when((q == 0) & (f == 0))
    def _():
        def issue(j, carry):
            row_copy(idx_ref[j], 0, j).start()
            return carry

        lax.fori_loop(0, tm, issue, 0, unroll=8)

    @pl.when(f == 0)
    def _():
        tile_wait(slot)
        xb[...] = xbuf[slot, :, 0:d].astype(BF16)
        acc[...] = jnp.zeros_like(acc)

    for u in range(per):
        j = f * per + u
        row_copy(nidx_ref[r * tm + j], 1 - slot, j).start()

    x = xb[...]
    g = _dot(x, wg_ref[0, 0])
    u_ = _dot(x, wu_ref[0, 0])
    hid = (g * _sigmoid(g) * u_).astype(BF16)
    acc[...] += _dot(hid, wd_ref[0, 0])

    @pl.when(f == nf - 1)
    def _():
        aff = xbuf[slot, :, d:d + LANES]
        lane = lax.broadcasted_iota(I32, aff.shape, 1)
        gate = jnp.sum(jnp.where(lane == e, aff, 0.0), axis=1, keepdims=True)
        o_ref[0] = (acc[...] * gate).astype(BF16)

    @pl.when(last & (f == nf - 1))
    def _():
        tile_wait(1 - slot)


def _experts(idx_flat, h, wg, wu, wd, layer, cap, tm=1024, tf=512):
    n, dx = h.shape
    _, E, d, dff = wg.shape
    tm = min(tm, cap)
    nf = dff // tf
    nrt = cap // tm
    return pl.pallas_call(
        functools.partial(_expert_kernel, tm=tm, nf=nf, d=d, nrt=nrt),
        grid=(E, nrt, nf),
        in_specs=[pl.BlockSpec((cap,), lambda e, r, f: (e,), memory_space=pltpu.SMEM),
                  pl.BlockSpec((cap,), lambda e, r, f: (e,), memory_space=pltpu.SMEM),
                  pl.BlockSpec(memory_space=pl.ANY),
                  pl.BlockSpec((1, 1, d, tf), lambda e, r, f: (layer, e, 0, f)),
                  pl.BlockSpec((1, 1, d, tf), lambda e, r, f: (layer, e, 0, f)),
                  pl.BlockSpec((1, 1, tf, d), lambda e, r, f: (layer, e, f, 0))],
        out_specs=pl.BlockSpec((1, tm, d), lambda e, r, f: (e, r, 0)),
        out_shape=jax.ShapeDtypeStruct((E, cap, d), BF16),
        scratch_shapes=[pltpu.VMEM((2, tm, dx), F32), pltpu.VMEM((tm, d), BF16), pltpu.VMEM((tm, d), F32),
                        pltpu.SemaphoreType.DMA((2,))],
        compiler_params=_cp(("arbitrary", "arbitrary", "arbitrary")),
    )(idx_flat, jnp.roll(idx_flat, -tm), h, wg, wu, wd)


COMBINE_WIN_LOG2 = 6
COMBINE_WIN = 1 << COMBINE_WIN_LOG2
ROW_ALIGN_LOG2 = 4
ROW_ALIGN = 1 << ROW_ALIGN_LOG2


def _combine_kernel(st_ref, x_ref, pos_ref, ye_ref, o_ref, buf, sems, buf2, sem2, p_s, *, TT, cap, nt):
    i = pl.program_id(0)
    E = N_EXPERTS
    W = COMBINE_WIN

    def window(tile, e, w):
        s0 = st_ref[tile * E + e]
        lo = lax.shift_left(lax.shift_right_logical(s0, ROW_ALIGN_LOG2), ROW_ALIGN_LOG2) + W * w
        return jnp.minimum(lo, cap - W), lo

    def first_copy(tile, slot, e):
        a, _ = window(tile, e, 0)
        return pltpu.make_async_copy(ye_ref.at[e, pl.ds(pl.multiple_of(a, ROW_ALIGN), W)],
                                     buf.at[slot, pl.ds(e * W, W)], sems.at[slot, e])

    slot = lax.rem(i, 2)

    @pl.when(i == 0)
    def _():
        for e in range(E):
            first_copy(0, 0, e).start()

    @pl.when(i + 1 < nt)
    def _():
        for e in range(E):
            first_copy(i + 1, 1 - slot, e).start()

    lane = lax.broadcasted_iota(I32, (TT, W), 1)

    def onehot(pc, a, lo):
        hit = (pc - a == lane) & (pc >= lo) & (pc < lo + W)
        return jnp.where(hit, 1.0, 0.0).astype(BF16)

    lane2 = lax.broadcasted_iota(I32, (TT, 2 * W), 1)
    first = lane2 < W
    for e in range(0, E, 2):
        a0, lo0 = window(i, e, 0)
        a1, lo1 = window(i, e + 1, 0)
        pc = jnp.where(first, pos_ref[:, e:e + 1], pos_ref[:, e + 1:e + 2])
        a = jnp.where(first, a0, a1 - W)
        lo = jnp.where(first, lo0, lo1)
        hit = (pc - a == lane2) & (pc >= lo) & (pc < lo + W)
        p_s[:, e * W:(e + 2) * W] = jnp.where(hit, 1.0, 0.0).astype(BF16)
    for e in range(E):
        first_copy(i, slot, e).wait()
    o_ref[...] = x_ref[...] + _dot(p_s[...], buf[slot])

    for e in range(E):
        _, lo = window(i, e, 0)
        s1 = st_ref[(i + 1) * E + e]
        nwin = lax.shift_right_logical(s1 - lo + (W - 1), COMBINE_WIN_LOG2)

        def extra(w, carry):
            a2, lo2 = window(i, e, w)
            cp = pltpu.make_async_copy(ye_ref.at[e, pl.ds(pl.multiple_of(a2, ROW_ALIGN), W)], buf2, sem2)
            cp.start()
            cp.wait()
            o_ref[...] += _dot(onehot(pos_ref[:, e:e + 1], a2, lo2), buf2[...])
            return carry

        lax.fori_loop(1, nwin, extra, 0)


def _combine(starts, x, pos_t, ye, cap, TT=256):
    n, d = x.shape
    E = N_EXPERTS
    TT = min(TT, n)
    nt = n // TT
    grid_spec = pltpu.PrefetchScalarGridSpec(
        num_scalar_prefetch=1,
        grid=(nt,),
        in_specs=[pl.BlockSpec((TT, d), lambda i, st: (i, 0)),
                  pl.BlockSpec((TT, E), lambda i, st: (i, 0)),
                  pl.BlockSpec(memory_space=pl.ANY)],
        out_specs=pl.BlockSpec((TT, d), lambda i, st: (i, 0)),
        scratch_shapes=[pltpu.VMEM((2, E * COMBINE_WIN, d), BF16), pltpu.SemaphoreType.DMA((2, E)),
                        pltpu.VMEM((COMBINE_WIN, d), BF16), pltpu.SemaphoreType.DMA(()),
                        pltpu.VMEM((TT, E * COMBINE_WIN), BF16)],
    )
    return pl.pallas_call(
        functools.partial(_combine_kernel, TT=TT, cap=cap, nt=nt),
        grid_spec=grid_spec,
        out_shape=jax.ShapeDtypeStruct((n, d), F32),
        compiler_params=_cp(("arbitrary",)),
    )(starts, x, pos_t, ye)


def _moe(x, g, router, wg, wu, wd, layer):
    n, d = x.shape
    E = N_EXPERTS
    cap = EC_CAPACITY * n // E
    TT = min(256, n)
    h, aff = _router(x, g, router.T)
    pos, cs, mask = _select(aff, cap)
    idx = _compact(mask, n, cap)
    ye = _experts(idx, h, wg, wu, wd, layer, cap)
    starts = jnp.concatenate([jnp.zeros((E, 1), I32), cs[:, TT - 1::TT]], axis=1).T.reshape(-1)
    return _combine(starts, x, pos.T, ye, cap, TT=TT)


def _rmsnorm_kernel(x_ref, g_ref, o_ref):
    xv = x_ref[...]
    o_ref[...] = xv * lax.rsqrt(jnp.mean(xv * xv, axis=-1, keepdims=True) + EPS) * g_ref[...]


def _rmsnorm(x, g, tm=512):
    n, d = x.shape
    return pl.pallas_call(
        _rmsnorm_kernel,
        grid=(n // tm,),
        in_specs=[pl.BlockSpec((tm, d), lambda i: (i, 0)), pl.BlockSpec((1, d), lambda i: (0, 0))],
        out_specs=pl.BlockSpec((tm, d), lambda i: (i, 0)),
        out_shape=jax.ShapeDtypeStruct((n, d), F32),
        compiler_params=_cp(("parallel",)),
    )(x, g.reshape(1, d))


def _even_mixer(x, B, L, p):
    proj = _rms_matmul(x, p["norm"], p["w_in"]).reshape(B, L, 5 * D_GROUP)
    hf = _lru(proj, None, p["conv_w"], p["conv_b"], p["wcat"][0], p["ba"][0], p["bx"][0], p["lam"][0],
              reverse=False)
    a_out = _lru(proj, hf, p["conv_w"], p["conv_b"], p["wcat"][1], p["ba"][1], p["bx"][1], p["lam"][1],
                 reverse=True)
    inv_freq = ROPE_THETA ** (-jnp.arange(0, DIFF_DH, 2, dtype=F32) / DIFF_DH)
    tabs = _rope_tables(L, DIFF_DH, inv_freq)
    half = DIFF_DH // 2
    qr = _rope(proj, 2, tabs, sh_a=LANES - half, sh_b=half, scale=DIFF_DH ** -0.5 * math.log2(math.e))
    kr = _rope(proj, 3, tabs, sh_a=LANES - half, sh_b=half, scale=1.0)
    vt = _v_transpose(proj, 4)
    b_out = _diff_attention(qr, kr, vt, p["diff_lam"], p["subln"], p["lam_init"])
    return _out_matmul(a_out.reshape(B * L, D_GROUP), b_out.reshape(B * L, D_GROUP), p["w_out"], x)


def _odd_mixer(x, B, L, p):
    proj = _rms_matmul(x, p["norm"], p["w_in"]).reshape(B, L, 7 * D_GROUP)
    c_out = _hyena_h(proj, p["conv_w"], p["conv_b"], p["w1"], p["b1"], p["w2"], p["b2"], p["w3"], p["freq"],
                   p["bias"])
    inv_freq = 1.0 / (10000.0 ** jnp.linspace(0.0, 1.0, RET_DH // 2, dtype=F32))
    tabs = _rope_tables(L, RET_DH, inv_freq)
    half = RET_DH // 2
    qr = _rope(proj, 3, tabs, sh_a=LANES - half, sh_b=half, scale=1.0)
    kr = _rope(proj, 4, tabs, sh_a=LANES - half, sh_b=half, scale=RET_DH ** -0.5)
    d_out = _retention(proj, qr, kr, p["rho"], 5 * N_HEAD_BLOCKS, 6 * N_HEAD_BLOCKS)
    return _out_matmul(c_out.reshape(B * L, D_GROUP), d_out.reshape(B * L, D_GROUP), p["w_out"], x)


def _trunk(x3, layers, final_norm):
    B, L, D = x3.shape
    x = x3.reshape(B * L, D)
    for kind, mp, ep in layers:
        x = (_even_mixer if kind == "even" else _odd_mixer)(x, B, L, mp)
        x = _moe(x, ep["norm"], ep["router"], ep["wg"], ep["wu"], ep["wd"], ep["layer"])
    return _rmsnorm(x, final_norm).reshape(B, L, D)


def kernel(x_prompt, x_sample, ev_norm, ev_w_in, ev_conv_w, ev_conv_b, ev_rg_wa, ev_rg_ba, ev_rg_wx, ev_rg_bx, ev_rg_lam, ev_diff_lam, ev_subln, ev_w_out, od_norm, od_w_in, od_conv_w, od_conv_b, od_flt_w1, od_flt_b1, od_flt_w2, od_flt_b2, od_flt_w3, od_flt_freq, od_flt_bias, od_ret_rho, od_w_out, moe_norm, moe_router, moe_w_gate, moe_w_up, moe_w_down, final_norm):
    depth = moe_norm.shape[0]
    wg_bf, wu_bf, wd_bf = moe_w_gate.astype(BF16), moe_w_up.astype(BF16), moe_w_down.astype(BF16)
    layers = []
    for layer in range(depth):
        j = layer // 2
        if layer % 2 == 0:
            mp = dict(norm=ev_norm[j], w_in=ev_w_in[j].astype(BF16), conv_w=ev_conv_w[j], conv_b=ev_conv_b[j],
                      wcat=jnp.concatenate([ev_rg_wa[j], ev_rg_wx[j]], axis=-1).astype(BF16),
                      ba=ev_rg_ba[j], bx=ev_rg_bx[j], lam=ev_rg_lam[j], diff_lam=ev_diff_lam[j],
                      subln=ev_subln[j], w_out=ev_w_out[j].astype(BF16),
                      lam_init=0.8 - 0.6 * math.exp(-0.3 * layer))
            kind = "even"
        else:
            mp = dict(norm=od_norm[j], w_in=od_w_in[j].astype(BF16), conv_w=od_conv_w[j], conv_b=od_conv_b[j],
                      w1=od_flt_w1[j], b1=od_flt_b1[j], w2=od_flt_w2[j], b2=od_flt_b2[j], w3=od_flt_w3[j],
                      freq=od_flt_freq[j], bias=od_flt_bias[j], rho=od_ret_rho[j],
                      w_out=od_w_out[j].astype(BF16))
            kind = "odd"
        ep = dict(norm=moe_norm[layer], router=moe_router[layer], wg=wg_bf, wu=wu_bf, wd=wd_bf, layer=layer)
        layers.append((kind, mp, ep))
    return (_trunk(x_prompt, layers, final_norm), _trunk(x_sample, layers, final_norm))
```

```python
import functools
import math

import jax
import jax.numpy as jnp
import numpy as np
from jax import lax
from jax.experimental import pallas as pl
from jax.experimental.pallas import tpu as pltpu

F32 = jnp.float32
BF16 = jnp.bfloat16
I32 = jnp.int32

D_MODEL = 2048
D_GROUP = 1024
LANES = 128
N_HEAD_BLOCKS = D_GROUP // LANES
LRU_C = 8.0
DIFF_DH = 64
ROPE_THETA = 10000.0
RET_DH = 128
HY_EMB = 33
HY_FFN = 64
HY_FAST_DECAY = 0.3
HY_SLOW_DECAY = 1.5
HY_TARGET = 1e-2
N_EXPERTS = 16
EC_CAPACITY = 2
EPS = 1e-6
FFT_N2 = 128
HALO = 8
VMEM_LIMIT_MB = 56


def _cp(sem, vmem_mb=VMEM_LIMIT_MB):
    return pltpu.CompilerParams(dimension_semantics=sem, vmem_limit_bytes=vmem_mb * 1024 * 1024)


def _sigmoid(x):
    return 1.0 / (1.0 + jnp.exp(-x))


def _softplus(x):
    return jnp.maximum(x, 0.0) + jnp.log(1.0 + jnp.exp(-jnp.abs(x)))


def _dot(a, b):
    return jnp.dot(a, b, preferred_element_type=F32)


def _dot_nt(a, b):
    return lax.dot_general(a, b, (((1,), (1,)), ((), ())), preferred_element_type=F32)


def _rms_matmul_kernel(x_ref, g_ref, w_ref, o_ref, h_ref):
    @pl.when(pl.program_id(1) == 0)
    def _():
        xv = x_ref[...]
        ms = jnp.mean(xv * xv, axis=-1, keepdims=True)
        h_ref[...] = (xv * lax.rsqrt(ms + EPS) * g_ref[...]).astype(BF16)

    o_ref[...] = _dot(h_ref[...], w_ref[...])


def _rms_matmul(x, g, w_bf, tm=1024, tn=1024):
    n, d = x.shape
    tm = min(tm, n)
    nout = w_bf.shape[1]
    return pl.pallas_call(
        _rms_matmul_kernel,
        grid=(n // tm, nout // tn),
        in_specs=[
            pl.BlockSpec((tm, d), lambda i, j: (i, 0)),
            pl.BlockSpec((1, d), lambda i, j: (0, 0)),
            pl.BlockSpec((d, tn), lambda i, j: (0, j)),
        ],
        out_specs=pl.BlockSpec((tm, tn), lambda i, j: (i, j)),
        out_shape=jax.ShapeDtypeStruct((n, nout), F32),
        scratch_shapes=[pltpu.VMEM((tm, d), BF16)],
        compiler_params=_cp(("parallel", "arbitrary")),
    )(x, g.reshape(1, d), w_bf)


def _out_matmul_kernel(a_ref, b_ref, wa_ref, wb_ref, x_ref, o_ref):
    o_ref[...] = (x_ref[...] + _dot(a_ref[...].astype(BF16), wa_ref[...])
                  + _dot(b_ref[...].astype(BF16), wb_ref[...]))


def _out_matmul(a, b, w_bf, x, tm=1024, tn=1024):
    n, d = x.shape
    tm = min(tm, n)
    dg = a.shape[1]
    return pl.pallas_call(
        _out_matmul_kernel,
        grid=(n // tm, d // tn),
        in_specs=[
            pl.BlockSpec((tm, dg), lambda i, j: (i, 0)),
            pl.BlockSpec((tm, dg), lambda i, j: (i, 0)),
            pl.BlockSpec((dg, tn), lambda i, j: (0, j)),
            pl.BlockSpec((dg, tn), lambda i, j: (1, j)),
            pl.BlockSpec((tm, tn), lambda i, j: (i, j)),
        ],
        out_specs=pl.BlockSpec((tm, tn), lambda i, j: (i, j)),
        out_shape=jax.ShapeDtypeStruct((n, d), F32),
        compiler_params=_cp(("parallel", "parallel")),
    )(a, b, w_bf, w_bf, x)


def _lru_kernel(*refs, reverse, T, nt):
    if reverse:
        (x_ref, prev_ref, next_ref, cw_ref, cb_ref, w_ref, ba_ref, bx_ref, lam_ref,
         gate_ref, hf_ref, o_ref, a_s, b_s, carry, h_s) = refs
    else:
        (x_ref, prev_ref, next_ref, cw_ref, cb_ref, w_ref, ba_ref, bx_ref, lam_ref,
         o_ref, a_s, b_s, carry) = refs
    i = pl.program_id(0)
    ti = (nt - 1 - i) if reverse else i
    nb = x_ref.shape[0]

    @pl.when(i == 0)
    def _():
        carry[...] = jnp.zeros_like(carry)

    cw = cw_ref[...]
    sp = _softplus(-lam_ref[...])
    for b in range(nb):
        x = x_ref[b]
        prev = jnp.where(ti == 0, 0.0, prev_ref[b])
        nxt = jnp.where(ti == nt - 1, 0.0, next_ref[b])
        xe = jnp.concatenate([prev, x, nxt], axis=0)
        xc = (cb_ref[...] + cw[0:1] * xe[HALO - 2:HALO - 2 + T] + cw[1:2] * xe[HALO - 1:HALO - 1 + T]
              + cw[2:3] * xe[HALO:HALO + T] + cw[3:4] * xe[HALO + 1:HALO + 1 + T])
        for blk in range(N_HEAD_BLOCKS):
            sl = slice(blk * LANES, (blk + 1) * LANES)
            xb = xc[:, sl]
            ri = _dot(xb.astype(BF16), w_ref[blk])
            r = _sigmoid(ri[:, :LANES] + ba_ref[:, sl])
            ig = _sigmoid(ri[:, LANES:] + bx_ref[:, sl])
            a = jnp.exp(-LRU_C * r * sp[:, sl])
            a_s[b, :, sl] = a
            b_s[b, :, sl] = jnp.sqrt(1.0 - a * a) * (ig * xb)

    dst = h_s if reverse else o_ref

    def body(k, hs):
        t = (T - 1 - k) if reverse else k
        out = []
        for b in range(nb):
            h = a_s[b, pl.ds(t, 1), :] * hs[b] + b_s[b, pl.ds(t, 1), :]
            dst[b, pl.ds(t, 1), :] = h
            out.append(h)
        return tuple(out)

    hs = lax.fori_loop(0, T, body, tuple(carry[b] for b in range(nb)), unroll=8)
    for b in range(nb):
        carry[b] = hs[b]

    if reverse:
        for b in range(nb):
            g = gate_ref[b]
            gelu = 0.5 * g * (1.0 + jnp.tanh(math.sqrt(2.0 / math.pi) * (g + 0.044715 * g * g * g)))
            o_ref[b] = ((hf_ref[b] + h_s[b]) * gelu).astype(BF16)


def _lru(proj, hf, conv_w, conv_b, wcat, ba, bx, lam, *, reverse, T=256):
    B, L, _ = proj.shape
    T = min(T, L)
    nt = L // T
    hb = T // HALO
    nh = L // HALO

    def tile(i):
        return (nt - 1 - i) if reverse else i

    x_spec = pl.BlockSpec((B, T, D_GROUP), lambda i: (0, tile(i), 0))
    prev_spec = pl.BlockSpec((B, HALO, D_GROUP), lambda i: (0, jnp.maximum(tile(i) * hb - 1, 0), 0))
    next_spec = pl.BlockSpec((B, HALO, D_GROUP), lambda i: (0, jnp.minimum((tile(i) + 1) * hb, nh - 1), 0))
    vec = pl.BlockSpec((1, D_GROUP), lambda i: (0, 0))
    in_specs = [x_spec, prev_spec, next_spec,
                pl.BlockSpec((4, D_GROUP), lambda i: (0, 0)), vec,
                pl.BlockSpec((N_HEAD_BLOCKS, LANES, 2 * LANES), lambda i: (0, 0, 0)), vec, vec, vec]
    args = [proj, proj, proj, conv_w, conv_b.reshape(1, -1), wcat, ba.reshape(1, -1), bx.reshape(1, -1),
            lam.reshape(1, -1)]
    scratch = [pltpu.VMEM((B, T, D_GROUP), F32), pltpu.VMEM((B, T, D_GROUP), F32),
               pltpu.VMEM((B, 1, D_GROUP), F32)]
    if reverse:
        in_specs += [pl.BlockSpec((B, T, D_GROUP), lambda i: (0, tile(i), 1)),
                     pl.BlockSpec((B, T, D_GROUP), lambda i: (0, tile(i), 0))]
        args += [proj, hf]
        scratch += [pltpu.VMEM((B, T, D_GROUP), F32)]
        out_dtype = BF16
    else:
        out_dtype = F32
    return pl.pallas_call(
        functools.partial(_lru_kernel, reverse=reverse, T=T, nt=nt),
        grid=(nt,),
        in_specs=in_specs,
        out_specs=pl.BlockSpec((B, T, D_GROUP), lambda i: (0, tile(i), 0)),
        out_shape=jax.ShapeDtypeStruct((B, L, D_GROUP), out_dtype),
        scratch_shapes=scratch,
        compiler_params=_cp(("arbitrary",)),
    )(*args)


def _rope_kernel(x_ref, c_ref, sa_ref, sb_ref, o_ref, *, sh_a, sh_b, scale):
    c = c_ref[...]
    sa = sa_ref[...]
    sb = sb_ref[...]
    for h in range(N_HEAD_BLOCKS):
        sl = slice(h * LANES, (h + 1) * LANES)
        x = x_ref[0, :, sl]
        y = x * c + pltpu.roll(x, sh_a, 1) * sa + pltpu.roll(x, sh_b, 1) * sb
        o_ref[0, :, sl] = (y * scale).astype(o_ref.dtype)


def _rope(proj, col_block, tabs, *, sh_a, sh_b, scale, T=512):
    B, L, _ = proj.shape
    T = min(T, L)
    tab_spec = pl.BlockSpec((T, LANES), lambda b, i: (i, 0))
    return pl.pallas_call(
        functools.partial(_rope_kernel, sh_a=sh_a, sh_b=sh_b, scale=scale),
        grid=(B, L // T),
        in_specs=[pl.BlockSpec((1, T, D_GROUP), lambda b, i: (b, i, col_block)), tab_spec, tab_spec, tab_spec],
        out_specs=pl.BlockSpec((1, T, D_GROUP), lambda b, i: (b, i, 0)),
        out_shape=jax.ShapeDtypeStruct((B, L, D_GROUP), BF16),
        compiler_params=_cp(("parallel", "parallel")),
    )(proj, *tabs)


def _rope_tables(L, dh, inv_freq):
    lane = np.arange(LANES)
    d = lane % dh
    half = dh // 2
    ang = jnp.arange(L, dtype=F32)[:, None] * inv_freq[d % half][None, :]
    cos = jnp.cos(ang)
    sin = jnp.sin(ang)
    lo = jnp.asarray(d < half)[None, :]
    return cos, jnp.where(lo, -sin, 0.0), jnp.where(lo, 0.0, sin)


VT_ROWS = LANES + 16
MAX_EAGER_JUMP = 60.0


def _vt_kernel(v_ref, o_ref):
    T = v_ref.shape[1]
    for h in range(N_HEAD_BLOCKS):
        sl = slice(h * LANES, (h + 1) * LANES)
        o_ref[0, h, 0:LANES, :] = v_ref[0, :, sl].T.astype(BF16)
        o_ref[0, h, LANES:VT_ROWS, :] = jnp.ones((VT_ROWS - LANES, T), BF16)


def _v_transpose(proj, col_block, T=512):
    B, L, _ = proj.shape
    T = min(T, L)
    return pl.pallas_call(
        _vt_kernel,
        grid=(B, L // T),
        in_specs=[pl.BlockSpec((1, T, D_GROUP), lambda b, i: (b, i, col_block))],
        out_specs=pl.BlockSpec((1, N_HEAD_BLOCKS, VT_ROWS, T), lambda b, i: (b, 0, 0, i)),
        out_shape=jax.ShapeDtypeStruct((B, N_HEAD_BLOCKS, VT_ROWS, L), BF16),
        compiler_params=_cp(("parallel", "parallel")),
    )(proj)


def _attn_kernel(q_ref, k_ref, vt_ref, lv_ref, sg_ref, o_ref, q2_s, m_s, acc_s, *, tq, tks, nsub, nk, lam_init):
    ki = pl.program_id(3)

    @pl.when(ki == 0)
    def _():
        q = q_ref[0]
        lane = lax.broadcasted_iota(I32, q.shape, 1)
        zero = jnp.zeros_like(q)
        q2_s[0:tq, :] = jnp.where(lane < DIFF_DH, q, zero)
        q2_s[tq:2 * tq, :] = jnp.where(lane >= DIFF_DH, q, zero)
        m_s[...] = jnp.full_like(m_s, -jnp.inf)
        acc_s[...] = jnp.zeros_like(acc_s)

    def scores(j):
        return _dot_nt(k_ref[0, j * tks:(j + 1) * tks, :], q2_s[...])

    def softmax_pass(eager):
        m_in = m_s[...]
        acc_in = acc_s[...]
        jump = jnp.zeros_like(m_in)
        s_next = scores(0)
        for j in range(nsub):
            ks = slice(j * tks, (j + 1) * tks)
            s = s_next
            if j + 1 < nsub:
                s_next = scores(j + 1)
            m_prev = m_in if j == 0 else m_s[...]
            acc_prev = acc_in if j == 0 else acc_s[...]
            m_new = jnp.maximum(m_prev, jnp.max(s, axis=0, keepdims=True))
            if eager and j > 0:
                p = jnp.exp2(s - m_prev).astype(BF16)
                acc_s[...] = jnp.exp2(m_prev - m_new) * (acc_prev + _dot(vt_ref[0, 0, :, ks], p))
                jump = jnp.maximum(jump, m_new - m_prev)
            else:
                p = jnp.exp2(s - m_new).astype(BF16)
                acc_s[...] = jnp.exp2(m_prev - m_new) * acc_prev + _dot(vt_ref[0, 0, :, ks], p)
            m_s[...] = m_new
        return m_in, acc_in, jnp.max(jump)

    m_in, acc_in, jump = softmax_pass(True)

    @pl.when(jump > MAX_EAGER_JUMP)
    def _():
        m_s[...] = m_in
        acc_s[...] = acc_in
        softmax_pass(False)

    @pl.when(ki == nk - 1)
    def _():
        o = acc_s[0:LANES, :] / acc_s[LANES:LANES + 1, :]
        lv = lv_ref[...]
        lam = (jnp.exp(jnp.sum(lv[0:1] * lv[1:2], axis=1, keepdims=True))
               - jnp.exp(jnp.sum(lv[2:3] * lv[3:4], axis=1, keepdims=True)) + lam_init)
        od = o[:, 0:tq] - lam * o[:, tq:2 * tq]
        ms = jnp.mean(od * od, axis=0, keepdims=True)
        y = od * lax.rsqrt(ms + EPS) * sg_ref[...] * (1.0 - lam_init)
        o_ref[0] = y.T.astype(BF16)


def _diff_attention(qr, kr, vt, diff_lam, subln, lam_init, tq=512, tk=16384, tks=1024):
    B, L, _ = qr.shape
    tq = min(tq, L)
    tk = min(tk, L)
    tks = min(tks, tk)
    nq, nk = L // tq, L // tk
    return pl.pallas_call(
        functools.partial(_attn_kernel, tq=tq, tks=tks, nsub=tk // tks, nk=nk, lam_init=lam_init),
        grid=(B, N_HEAD_BLOCKS, nq, nk),
        in_specs=[
            pl.BlockSpec((1, tq, LANES), lambda b, h, qi, ki: (b, qi, h)),
            pl.BlockSpec((1, tk, LANES), lambda b, h, qi, ki: (b, ki, h)),
            pl.BlockSpec((1, 1, VT_ROWS, tk), lambda b, h, qi, ki: (b, h, 0, ki)),
            pl.BlockSpec((4, DIFF_DH), lambda b, h, qi, ki: (0, 0)),
            pl.BlockSpec((LANES, 1), lambda b, h, qi, ki: (0, 0)),
        ],
        out_specs=pl.BlockSpec((1, tq, LANES), lambda b, h, qi, ki: (b, qi, h)),
        out_shape=jax.ShapeDtypeStruct((B, L, D_GROUP), BF16),
        scratch_shapes=[pltpu.VMEM((2 * tq, LANES), BF16), pltpu.VMEM((1, 2 * tq), F32),
                        pltpu.VMEM((VT_ROWS, 2 * tq), F32)],
        compiler_params=_cp(("parallel", "parallel", "parallel", "arbitrary")),
    )(qr, kr, vt, diff_lam, subln.reshape(LANES, 1))


def _log_gamma(rho_ref):
    return -_softplus(-rho_ref[0, 0])


def _ret_state_kernel(kf_ref, vf_ref, kb_ref, vb_ref, rf_ref, rb_ref, sf_ref, sb_ref, sf_s, sb_s, *, C):
    n = pl.program_id(2)

    @pl.when(n == 0)
    def _():
        sf_s[...] = jnp.zeros_like(sf_s)
        sb_s[...] = jnp.zeros_like(sb_s)

    sf_ref[0, 0, 0] = sf_s[...].astype(BF16)
    sb_ref[0, 0, 0] = sb_s[...].astype(BF16)
    lgf = _log_gamma(rf_ref)
    lgb = _log_gamma(rb_ref)
    row = lax.broadcasted_iota(I32, (C, LANES), 0).astype(F32)
    kdf = kf_ref[0].astype(F32) * jnp.exp(lgf * (C - 1.0 - row))
    kdb = kb_ref[0].astype(F32) * jnp.exp(lgb * row)
    sf_s[...] = sf_s[...] * jnp.exp(lgf * C) + _dot(kdf.T.astype(BF16), vf_ref[0].astype(BF16))
    sb_s[...] = sb_s[...] * jnp.exp(lgb * C) + _dot(kdb.T.astype(BF16), vb_ref[0].astype(BF16))


def _ret_main_kernel(q_ref, k_ref, v_ref, g_ref, sf_ref, sb_ref, rf_ref, rb_ref, o_ref, *, C):
    lgf = _log_gamma(rf_ref)
    lgb = _log_gamma(rb_ref)
    q = q_ref[0]
    s = _dot_nt(q, k_ref[0])
    r = lax.broadcasted_iota(I32, (C, C), 0)
    c = lax.broadcasted_iota(I32, (C, C), 1)
    dist = (r - c).astype(F32)
    dmat = jnp.exp(jnp.where(r >= c, lgf * dist, -lgb * dist))
    inner = _dot((s * dmat).astype(BF16), v_ref[0].astype(BF16))
    row = lax.broadcasted_iota(I32, (C, LANES), 0).astype(F32)
    qf = q.astype(F32)
    qcat = jnp.concatenate([qf * jnp.exp(lgf * (row + 1.0)), qf * jnp.exp(lgb * (C - row))], axis=1)
    scat = jnp.concatenate([sf_ref[0, 0, 0], sb_ref[0, 0, 0]], axis=0)
    y = inner + _dot(qcat.astype(BF16), scat)
    y = y * lax.rsqrt(jnp.mean(y * y, axis=-1, keepdims=True) + EPS)
    g = g_ref[0]
    o_ref[0] = (y * (g * _sigmoid(g))).astype(BF16)


def _retention(proj, qr, kr, rho, v_blk0, g_blk0, C=512):
    B, L, _ = proj.shape
    H = N_HEAD_BLOCKS
    C = min(C, L)
    N = L // C
    rho4 = rho.reshape(2, H, 1, 1)
    rf_spec = pl.BlockSpec((1, 1, 1, 1), lambda b, h, n: (0, h, 0, 0))
    rb_spec = pl.BlockSpec((1, 1, 1, 1), lambda b, h, n: (1, h, 0, 0))
    st_shape = jax.ShapeDtypeStruct((B, H, N, LANES, LANES), BF16)
    sf, sb = pl.pallas_call(
        functools.partial(_ret_state_kernel, C=C),
        grid=(B, H, N),
        in_specs=[
            pl.BlockSpec((1, C, LANES), lambda b, h, n: (b, n, h)),
            pl.BlockSpec((1, C, LANES), lambda b, h, n: (b, n, v_blk0 + h)),
            pl.BlockSpec((1, C, LANES), lambda b, h, n: (b, N - 1 - n, h)),
            pl.BlockSpec((1, C, LANES), lambda b, h, n: (b, N - 1 - n, v_blk0 + h)),
            rf_spec, rb_spec,
        ],
        out_specs=[pl.BlockSpec((1, 1, 1, LANES, LANES), lambda b, h, n: (b, h, n, 0, 0)),
                   pl.BlockSpec((1, 1, 1, LANES, LANES), lambda b, h, n: (b, h, N - 1 - n, 0, 0))],
        out_shape=[st_shape, st_shape],
        scratch_shapes=[pltpu.VMEM((LANES, LANES), F32), pltpu.VMEM((LANES, LANES), F32)],
        compiler_params=_cp(("parallel", "parallel", "arbitrary")),
    )(kr, proj, kr, proj, rho4, rho4)
    st_spec = pl.BlockSpec((1, 1, 1, LANES, LANES), lambda b, h, n: (b, h, n, 0, 0))
    return pl.pallas_call(
        functools.partial(_ret_main_kernel, C=C),
        grid=(B, H, N),
        in_specs=[
            pl.BlockSpec((1, C, LANES), lambda b, h, n: (b, n, h)),
            pl.BlockSpec((1, C, LANES), lambda b, h, n: (b, n, h)),
            pl.BlockSpec((1, C, LANES), lambda b, h, n: (b, n, v_blk0 + h)),
            pl.BlockSpec((1, C, LANES), lambda b, h, n: (b, n, g_blk0 + h)),
            st_spec, st_spec, rf_spec, rb_spec,
        ],
        out_specs=pl.BlockSpec((1, C, LANES), lambda b, h, n: (b, n, h)),
        out_shape=jax.ShapeDtypeStruct((B, L, D_GROUP), BF16),
        compiler_params=_cp(("parallel", "parallel", "parallel")),
    )(qr, kr, proj, proj, sf, sb, rho4, rho4)


def _dwconv3_kernel(x_ref, prev_ref, next_ref, w_ref, b_ref, o_ref, *, T, nt):
    i = pl.program_id(1)
    x = x_ref[0]
    prev = jnp.where(i == 0, 0.0, prev_ref[0])
    nxt = jnp.where(i == nt - 1, 0.0, next_ref[0])
    xe = jnp.concatenate([prev, x, nxt], axis=0)
    w = w_ref[...]
    o_ref[0] = (b_ref[...] + w[0:1] * xe[HALO - 1:HALO - 1 + T] + w[1:2] * xe[HALO:HALO + T]
                + w[2:3] * xe[HALO + 1:HALO + 1 + T])


def _dwconv3(proj, conv_w, conv_b, T=512):
    B, L, _ = proj.shape
    T = min(T, L)
    nt = L // T
    hb = T // HALO
    nh = L // HALO
    W = 3 * D_GROUP
    return pl.pallas_call(
        functools.partial(_dwconv3_kernel, T=T, nt=nt),
        grid=(B, nt),
        in_specs=[
            pl.BlockSpec((1, T, W), lambda b, i: (b, i, 0)),
            pl.BlockSpec((1, HALO, W), lambda b, i: (b, jnp.maximum(i * hb - 1, 0), 0)),
            pl.BlockSpec((1, HALO, W), lambda b, i: (b, jnp.minimum((i + 1) * hb, nh - 1), 0)),
            pl.BlockSpec((3, W), lambda b, i: (0, 0)),
            pl.BlockSpec((1, W), lambda b, i: (0, 0)),
        ],
        out_specs=pl.BlockSpec((1, T, W), lambda b, i: (b, i, 0)),
        out_shape=jax.ShapeDtypeStruct((B, L, W), F32),
        compiler_params=_cp(("parallel", "parallel")),
    )(proj, proj, proj, conv_w, conv_b.reshape(1, W))


def _filter_mlp_kernel(z_ref, w1_ref, b1_ref, w2_ref, b2_ref, w3_ref, fr_ref, hf_ref, sum_ref, *, TL, L):
    i = pl.program_id(0)
    fr = fr_ref[...]
    h = jnp.sin(fr * (_dot(z_ref[...].astype(BF16), w1_ref[...]) + b1_ref[...]))
    for j in range(2):
        h = jnp.sin(fr * (_dot(h.astype(BF16), w2_ref[j]) + b2_ref[j]))
    hf = _dot(h.astype(BF16), w3_ref[...])
    row = i * TL + lax.broadcasted_iota(I32, (TL, D_GROUP), 0)
    t = row.astype(F32) / (L - 1.0)
    ch = lax.broadcasted_iota(I32, (TL, D_GROUP), 1).astype(F32)
    max_decay = math.log(HY_TARGET) / HY_FAST_DECAY
    min_decay = math.log(HY_TARGET) / HY_SLOW_DECAY
    delta = jnp.abs(min_decay + (max_decay - min_decay) * ch / (D_GROUP - 1.0))
    dec = jnp.exp(-t * delta)
    keep = row < L - 1

    @pl.when(i == 0)
    def _():
        sum_ref[...] = jnp.zeros_like(sum_ref)

    for q in range(4):
        sl = slice(q * D_GROUP, (q + 1) * D_GROUP)
        v = hf[:, sl] * dec
        if q % 2 == 1:
            v = jnp.where(keep, v, 0.0)
        hf_ref[:, sl] = v
        sum_ref[:, sl] += jnp.sum(jnp.abs(v), axis=0, keepdims=True)


def _filter_mlp(L, w1, b1, w2, b2, w3, freq, TL=512):
    TL = min(TL, L)
    t = jnp.linspace(0.0, 1.0, L, dtype=F32)[:, None]
    bands = (HY_EMB - 1) // 2
    wpos = 2.0 * math.pi * jnp.arange(L, dtype=F32)[:, None] / L
    f = jnp.linspace(1e-4, bands - 1, bands, dtype=F32)[None, :]
    z = jnp.concatenate([t, jnp.cos(f * wpos), jnp.sin(f * wpos)], axis=-1)
    z = jnp.pad(z, ((0, 0), (0, LANES - HY_EMB)))
    pf = LANES - HY_FFN
    w1p = jnp.pad(w1, ((0, LANES - HY_EMB), (0, pf))).astype(BF16)
    b1p = jnp.pad(b1, (0, pf)).reshape(1, LANES)
    w2p = jnp.pad(w2, ((0, 0), (0, pf), (0, pf))).astype(BF16)
    b2p = jnp.pad(b2, ((0, 0), (0, pf))).reshape(2, 1, LANES)
    w3p = jnp.pad(w3, ((0, pf), (0, 0))).astype(BF16)
    frp = jnp.pad(freq, (0, pf)).reshape(1, LANES)
    W = 4 * D_GROUP
    full = lambda *shape: pl.BlockSpec(shape, lambda i: (0,) * len(shape))
    return pl.pallas_call(
        functools.partial(_filter_mlp_kernel, TL=TL, L=L),
        grid=(L // TL,),
        in_specs=[pl.BlockSpec((TL, LANES), lambda i: (i, 0)), full(LANES, LANES), full(1, LANES),
                  full(2, LANES, LANES), full(2, 1, LANES), full(LANES, W), full(1, LANES)],
        out_specs=[pl.BlockSpec((TL, W), lambda i: (i, 0)), full(1, W)],
        out_shape=[jax.ShapeDtypeStruct((L, W), F32), jax.ShapeDtypeStruct((1, W), F32)],
        compiler_params=_cp(("arbitrary",)),
    )(z, w1p, b1p, w2p, b2p, w3p, frp)


T2_GROUP = 8
SLAB_PAD = 8


def _dft_tables_h(N):
    N2 = FFT_N2
    N1 = N // N2
    R = N1 // 2
    K1 = R + SLAB_PAD
    k1 = np.arange(K1)
    live = (k1 <= R).astype(np.float64)
    t1 = np.arange(R)
    a1 = 2.0 * math.pi * ((k1[:, None] * t1[None, :]) % N1) / N1
    f1 = np.concatenate([np.cos(a1) * live[:, None], -np.sin(a1) * live[:, None]], axis=0)
    wgt = np.where((k1 == 0) | (k1 == R), 1.0, 2.0) * live
    f1inv = np.concatenate([np.cos(a1) * wgt[:, None], -np.sin(a1) * wgt[:, None]], axis=0).T / N
    k2 = np.arange(N2)
    a2 = 2.0 * math.pi * ((k2[:, None] * k2[None, :]) % N2) / N2
    fr, fi = np.cos(a2), -np.sin(a2)
    at = 2.0 * math.pi * (k2[:, None] * k1[None, :]) / N
    ak = 2.0 * math.pi * (k1[:, None] + N1 * k2[None, :]) / N
    tw = lambda a: jnp.broadcast_to(jnp.asarray(a, F32)[:, :, None], a.shape + (LANES,))
    bf = lambda a: jnp.asarray(a, F32).astype(BF16)
    return dict(N1=N1, N2=N2, R=R, K1=K1, f1=bf(f1), f1inv=bf(f1inv),
                f2=bf(np.block([[fr, -fi], [fi, fr]])), f2inv=bf(np.block([[fr, fi], [-fi, fr]])),
                twa_c=tw(np.cos(at)), twa_s=tw(-np.sin(at)), twb_c=tw(np.cos(at.T)), twb_s=tw(np.sin(at.T)),
                wk_c=tw(np.cos(ak)), wk_s=tw(np.sin(ak)))


def _fft1_kernel(x_ref, f_ref, c_ref, s_ref, o_ref, *, K1, C):
    for j in range(T2_GROUP):
        a = _dot(f_ref[...], x_ref[0, :, j, :].astype(BF16))
        c = c_ref[j]
        s = s_ref[j]
        for cb in range(C // LANES):
            sl = slice(cb * LANES, (cb + 1) * LANES)
            ar = a[0:K1, sl]
            ai = a[K1:2 * K1, sl]
            o_ref[0, 0, j, 0:K1, sl] = ar * c - ai * s
            o_ref[0, 0, j, K1:2 * K1, sl] = ar * s + ai * c


def _fft1(x4, tab, col0, ncol):
    Bx, R, N2, _ = x4.shape
    K1 = tab["K1"]
    C = D_GROUP
    G = T2_GROUP
    return pl.pallas_call(
        functools.partial(_fft1_kernel, K1=K1, C=C),
        grid=(Bx, ncol, N2 // G),
        in_specs=[pl.BlockSpec((1, R, G, C), lambda b, ci, g: (b, 0, g, col0 + ci)),
                  pl.BlockSpec((2 * K1, R), lambda b, ci, g: (0, 0)),
                  pl.BlockSpec((G, K1, LANES), lambda b, ci, g: (g, 0, 0)),
                  pl.BlockSpec((G, K1, LANES), lambda b, ci, g: (g, 0, 0))],
        out_specs=pl.BlockSpec((1, 1, G, 2 * K1, C), lambda b, ci, g: (b, ci, g, 0, 0)),
        out_shape=jax.ShapeDtypeStruct((Bx, ncol, N2, 2 * K1, C), F32),
        compiler_params=_cp(("parallel", "parallel", "parallel")),
    )(x4, tab["f1"], tab["twa_c"], tab["twa_s"])


def _strided_cat(re_ref, im_ref, kk):
    return jnp.concatenate([re_ref[0, 0, :, kk, :], im_ref[0, 0, :, kk, :]], axis=0).astype(BF16)


def _filter_fft2_kernel(pr_ref, pi_ref, mr_ref, mi_ref, f_ref, c_ref, s_ref, inv_ref, o_ref, *, N2):
    for kk in range(T2_GROUP):
        zp = _dot(f_ref[...], _strided_cat(pr_ref, pi_ref, kk))
        zm = _dot(f_ref[...], _strided_cat(mr_ref, mi_ref, kk))
        c = c_ref[kk]
        s = s_ref[kk]
        inv = inv_ref[...]
        for cb in range(zp.shape[1] // LANES):
            sl = slice(cb * LANES, (cb + 1) * LANES)
            zmr, zmi = zm[0:N2, sl], zm[N2:2 * N2, sl]
            o_ref[kk, 0, :, sl] = ((zp[0:N2, sl] + c * zmr + s * zmi) * inv[:, sl]).astype(BF16)
            o_ref[kk, 1, :, sl] = ((zp[N2:2 * N2, sl] + s * zmr - c * zmi) * inv[:, sl]).astype(BF16)


def _filter_fft2(af, tab, inv_sum, Ct=512):
    _, _, N2, _, C = af.shape
    K1 = tab["K1"]
    G = T2_GROUP
    nkg = K1 // G
    nc = C // Ct

    def a_spec(side, im):
        return pl.BlockSpec((1, 1, N2, G, Ct), lambda kg, o, cj: (0, 2 * o + side, 0, im * nkg + kg, cj))

    tw_spec = pl.BlockSpec((G, N2, LANES), lambda kg, o, cj: (kg, 0, 0))
    return pl.pallas_call(
        functools.partial(_filter_fft2_kernel, N2=N2),
        grid=(nkg, 2, nc),
        in_specs=[a_spec(0, 0), a_spec(0, 1), a_spec(1, 0), a_spec(1, 1),
                  pl.BlockSpec((2 * N2, 2 * N2), lambda kg, o, cj: (0, 0)), tw_spec, tw_spec,
                  pl.BlockSpec((1, Ct), lambda kg, o, cj: (0, o * nc + cj))],
        out_specs=pl.BlockSpec((G, 2, N2, Ct), lambda kg, o, cj: (kg, 0, 0, o * nc + cj)),
        out_shape=jax.ShapeDtypeStruct((K1, 2, N2, 2 * C), BF16),
        compiler_params=_cp(("parallel", "parallel", "parallel")),
    )(af, af, af, af, tab["f2"], tab["wk_c"], tab["wk_s"], inv_sum)


def _conv_mid_h_kernel(ar_ref, ai_ref, kf_ref, f_ref, fi_ref, c_ref, s_ref, o_ref, *, N2):
    for kk in range(T2_GROUP):
        z = _dot(f_ref[...], _strided_cat(ar_ref, ai_ref, kk))
        zr, zi = z[0:N2], z[N2:2 * N2]
        kr = kf_ref[kk, 0].astype(F32)
        ki = kf_ref[kk, 1].astype(F32)
        y = jnp.concatenate([zr * kr - zi * ki, zr * ki + zi * kr], axis=0).astype(BF16)
        b = _dot(fi_ref[...], y)
        c = c_ref[kk]
        s = s_ref[kk]
        for cb in range(b.shape[1] // LANES):
            sl = slice(cb * LANES, (cb + 1) * LANES)
            br = b[0:N2, sl]
            bi = b[N2:2 * N2, sl]
            o_ref[0, 0, kk, :, sl] = br * c - bi * s
            o_ref[0, 1, kk, :, sl] = br * s + bi * c


def _conv_mid_h(a5, kf, order, tab, Ct=512):
    B, _, N2, _, C = a5.shape
    K1 = tab["K1"]
    G = T2_GROUP
    nkg = K1 // G
    nc = C // Ct
    sq = pl.BlockSpec((2 * N2, 2 * N2), lambda b, kg, cj: (0, 0))
    tw_spec = pl.BlockSpec((G, N2, LANES), lambda b, kg, cj: (kg, 0, 0))
    return pl.pallas_call(
        functools.partial(_conv_mid_h_kernel, N2=N2),
        grid=(B, nkg, nc),
        in_specs=[pl.BlockSpec((1, 1, N2, G, Ct), lambda b, kg, cj: (b, 0, 0, kg, cj)),
                  pl.BlockSpec((1, 1, N2, G, Ct), lambda b, kg, cj: (b, 0, 0, nkg + kg, cj)),
                  pl.BlockSpec((G, 2, N2, Ct), lambda b, kg, cj: (kg, 0, 0, order * nc + cj)),
                  sq, sq, tw_spec, tw_spec],
        out_specs=pl.BlockSpec((1, 2, G, N2, Ct), lambda b, kg, cj: (b, 0, kg, 0, cj)),
        out_shape=jax.ShapeDtypeStruct((B, 2, K1, N2, C), F32),
        compiler_params=_cp(("parallel", "parallel", "parallel")),
    )(a5, a5, kf, tab["f2"], tab["f2inv"], tab["twb_c"], tab["twb_s"])


def _conv_out_h_kernel(b_ref, f_ref, g_ref, z_ref, bias_ref, o_ref, *, K1):
    for j in range(T2_GROUP):
        bj = jnp.concatenate([b_ref[0, 0, :, j, :], b_ref[0, 1, :, j, :]], axis=0).astype(BF16)
        y = _dot(f_ref[...], bj)
        o_ref[0, :, j, :] = g_ref[0, :, j, :] * (y + z_ref[0, :, j, :] * bias_ref[...])


def _conv_out_h(b5, tab, uc4, g_blk, z4, z_blk, bias):
    B, _, K1, N2, C = b5.shape
    R = tab["R"]
    G = T2_GROUP
    return pl.pallas_call(
        functools.partial(_conv_out_h_kernel, K1=K1),
        grid=(B, N2 // G),
        in_specs=[pl.BlockSpec((1, 2, K1, G, C), lambda b, g: (b, 0, 0, g, 0)),
                  pl.BlockSpec((R, 2 * K1), lambda b, g: (0, 0)),
                  pl.BlockSpec((1, R, G, C), lambda b, g: (b, 0, g, g_blk)),
                  pl.BlockSpec((1, R, G, C), lambda b, g: (b, 0, g, z_blk)),
                  pl.BlockSpec((1, C), lambda b, g: (0, 0))],
        out_specs=pl.BlockSpec((1, R, G, C), lambda b, g: (b, 0, g, 0)),
        out_shape=jax.ShapeDtypeStruct((B, R, N2, C), F32),
        compiler_params=_cp(("parallel", "parallel")),
    )(b5, tab["f1inv"], uc4, z4, bias.reshape(1, C))


def _hyena_h(proj, conv_w, conv_b, w1, b1, w2, b2, w3, freq, bias):
    B, L, _ = proj.shape
    C = D_GROUP
    tab = _dft_tables_h(2 * L)
    N2, R = tab["N2"], tab["R"]
    uc4 = _dwconv3(proj, conv_w, conv_b).reshape(B, R, N2, 3 * C)
    hf, sums = _filter_mlp(L, w1, b1, w2, b2, w3, freq)
    s4 = sums.reshape(2, 2, C)
    inv_sum = (1.0 / (s4[:, 0] + s4[:, 1])).reshape(1, 2 * C)
    af = _fft1(hf.reshape(1, R, N2, 4 * C), tab, 0, 4)
    kf = _filter_fft2(af, tab, inv_sum)
    z4, z_blk = uc4, 0
    for o in range(2):
        a5 = _fft1(z4, tab, z_blk, 1)
        b5 = _conv_mid_h(a5, kf, o, tab)
        z4 = _conv_out_h(b5, tab, uc4, 1 + o, z4, z_blk, bias[o])
        z_blk = 0
    return z4.reshape(B, L, C)


def _router_kernel(x_ref, g_ref, rt_ref, h_ref, aff_ref):
    d = x_ref.shape[1]
    xv = x_ref[...]
    ms = jnp.mean(xv * xv, axis=-1, keepdims=True)
    h = xv * lax.rsqrt(ms + EPS) * g_ref[...]
    h_ref[:, 0:d] = h
    logits = lax.dot_general(rt_ref[...], h, (((1,), (1,)), ((), ())), precision=lax.Precision.HIGHEST,
                             preferred_element_type=F32)
    row = lax.broadcasted_iota(I32, logits.shape, 0)
    logits = jnp.where(row < N_EXPERTS, logits, -jnp.inf)
    e = jnp.exp(logits - jnp.max(logits, axis=0, keepdims=True))
    aff = e / jnp.sum(e, axis=0, keepdims=True)
    aff_ref[...] = aff[0:N_EXPERTS]
    h_ref[:, d:d + LANES] = aff.T


def _router(x, g, router_t, tm=512):
    n, d = x.shape
    rt = jnp.pad(router_t, ((0, LANES - N_EXPERTS), (0, 0)))
    return pl.pallas_call(
        _router_kernel,
        grid=(n // tm,),
        in_specs=[pl.BlockSpec((tm, d), lambda i: (i, 0)), pl.BlockSpec((1, d), lambda i: (0, 0)),
                  pl.BlockSpec((LANES, d), lambda i: (0, 0))],
        out_specs=[pl.BlockSpec((tm, d + LANES), lambda i: (i, 0)), pl.BlockSpec((N_EXPERTS, tm), lambda i: (0, i))],
        out_shape=[jax.ShapeDtypeStruct((n, d + LANES), F32), jax.ShapeDtypeStruct((N_EXPERTS, n), F32)],
        compiler_params=_cp(("parallel",)),
    )(x, g.reshape(1, d), rt)


def _select_kernel(aff_ref, pos_ref, cs_ref, m_s, u_s, *, n, cap, CH):
    E = N_EXPERTS
    bits = pltpu.bitcast(aff_ref[...], I32)

    def search(it, thr):
        cand = thr | jnp.left_shift(jnp.int32(1), 30 - it)
        cnt = jnp.sum((bits >= cand).astype(F32), axis=1, keepdims=True)
        return jnp.where(cnt >= cap, cand, thr)

    thr = lax.fori_loop(0, 31, search, jnp.zeros((E, 1), I32))
    need = cap - jnp.sum((bits > thr).astype(F32), axis=1, keepdims=True)
    r = lax.broadcasted_iota(I32, (CH, CH), 0)
    c = lax.broadcasted_iota(I32, (CH, CH), 1)
    u_s[...] = (r <= c).astype(BF16)

    def chunk_bits(j):
        off = pl.multiple_of(j * CH, CH)
        return off, pltpu.bitcast(aff_ref[:, pl.ds(off, CH)], I32)

    def ties(j, carry):
        off, b = chunk_bits(j)
        eq = (b == thr).astype(F32)
        incl = _dot(eq.astype(BF16), u_s[...]) + carry
        sel = (b > thr) | ((b == thr) & (incl - eq < need))
        m_s[:, pl.ds(off, CH)] = sel.astype(F32)
        return incl[:, CH - 1:CH]

    lax.fori_loop(0, n // CH, ties, jnp.zeros((E, 1), F32))

    def slots(j, carry):
        off, _ = chunk_bits(j)
        m = m_s[:, pl.ds(off, CH)]
        incl = _dot(m.astype(BF16), u_s[...]) + carry
        cs_ref[:, pl.ds(off, CH)] = incl.astype(I32)
        pos_ref[:, pl.ds(off, CH)] = jnp.where(m > 0.0, incl - 1.0, -1.0).astype(I32)
        return incl[:, CH - 1:CH]

    lax.fori_loop(0, n // CH, slots, jnp.zeros((E, 1), F32))


def _select(aff, cap, CH=512):
    E, n = aff.shape
    CH = min(CH, n)
    full = pl.BlockSpec((E, n), lambda: (0, 0))
    return pl.pallas_call(
        functools.partial(_select_kernel, n=n, cap=cap, CH=CH),
        in_specs=[full],
        out_specs=[full, full, full],
        out_shape=[jax.ShapeDtypeStruct((E, n), I32), jax.ShapeDtypeStruct((E, n), I32),
                   jax.ShapeDtypeStruct((E, n), F32)],
        scratch_shapes=[pltpu.VMEM((CH, CH), BF16)],
        compiler_params=_cp(None),
    )(aff)


def _compact_kernel(m_ref, idx_ref, *, cap, RB):
    m = m_ref[0]
    nch = m.shape[0]
    r = lax.broadcasted_iota(I32, (LANES, LANES), 0)
    c = lax.broadcasted_iota(I32, (LANES, LANES), 1)
    local = _dot(m, (r <= c).astype(F32))
    tot = _dot(m, (r >= 0).astype(F32))
    rr = lax.broadcasted_iota(I32, (nch, nch), 0)
    cc = lax.broadcasted_iota(I32, (nch, nch), 1)
    cend = _dot((cc <= rr).astype(F32), tot)
    cend_row = cend.T[0:1, :]
    cstart_row = cend_row - tot.T[0:1, :]
    chunk_id = lax.broadcasted_iota(I32, (RB, nch), 1).astype(F32)

    def block(b, carry):
        j0 = pl.multiple_of(b * RB, RB)
        slot = (j0 + lax.broadcasted_iota(I32, (RB, 1), 0)).astype(F32)
        cstar = jnp.sum((slot >= cend_row).astype(F32), axis=1, keepdims=True)
        oh = chunk_id == cstar
        counts = _dot(jnp.where(oh, 1.0, 0.0), local)
        first = jnp.sum(jnp.where(oh, cstart_row, 0.0), axis=1, keepdims=True)
        within = jnp.sum((counts <= slot - first).astype(F32), axis=1, keepdims=True)
        tok = jnp.broadcast_to(LANES * cstar + within, (RB, LANES)).T
        idx_ref[0, :, pl.ds(j0, RB)] = tok[0:8].astype(I32)
        return carry

    lax.fori_loop(0, cap // RB, block, 0)


def _compact(mask, n, cap, RB=512):
    E = N_EXPERTS
    nch = n // LANES
    m3 = mask.reshape(E, nch, LANES)
    if nch < LANES:
        m3 = jnp.pad(m3, ((0, 0), (0, LANES - nch), (0, 0)))
        nch = LANES
    RB = min(RB, cap)
    idx8 = pl.pallas_call(
        functools.partial(_compact_kernel, cap=cap, RB=RB),
        grid=(E,),
        in_specs=[pl.BlockSpec((1, nch, LANES), lambda e: (e, 0, 0))],
        out_specs=pl.BlockSpec((1, 8, cap), lambda e: (e, 0, 0)),
        out_shape=jax.ShapeDtypeStruct((E, 8, cap), I32),
        compiler_params=_cp(("parallel",)),
    )(m3)
    return idx8[:, 0, :].reshape(-1)


def _expert_kernel(idx_ref, nidx_ref, h_ref, wg_ref, wu_ref, wd_ref, o_ref, xbuf, xb, acc, sems,
                   *, tm, nf, d, nrt):
    e = pl.program_id(0)
    r = pl.program_id(1)
    f = pl.program_id(2)
    q = e * nrt + r
    slot = lax.rem(q, 2)
    last = q == N_EXPERTS * nrt - 1
    per = tm // nf

    def row_copy(t, s, j):
        return pltpu.make_async_copy(h_ref.at[pl.ds(t, 1)], xbuf.at[s, pl.ds(j, 1)], sems.at[s])

    def tile_wait(s):
        pltpu.make_async_copy(h_ref.at[pl.ds(0, tm)], xbuf.at[s, :, :], sems.at[s]).wait()

    @pl.when((q == 0) & (f == 0))
    def _():
        def issue(j, carry):
            row_copy(idx_ref[j], 0, j).start()
            return carry

        lax.fori_loop(0, tm, issue, 0, unroll=8)

    @pl.when(f == 0)
    def _():
        tile_wait(slot)
        xb[...] = xbuf[slot, :, 0:d].astype(BF16)
        acc[...] = jnp.zeros_like(acc)

    for u in range(per):
        j = f * per + u
        row_copy(nidx_ref[r * tm + j], 1 - slot, j).start()

    x = xb[...]
    g = _dot(x, wg_ref[0, 0])
    u_ = _dot(x, wu_ref[0, 0])
    hid = (g * _sigmoid(g) * u_).astype(BF16)
    acc[...] += _dot(hid, wd_ref[0, 0])

    @pl.when(f == nf - 1)
    def _():
        aff = xbuf[slot, :, d:d + LANES]
        lane = lax.broadcasted_iota(I32, aff.shape, 1)
        gate = jnp.sum(jnp.where(lane == e, aff, 0.0), axis=1, keepdims=True)
        o_ref[0] = (acc[...] * gate).astype(BF16)

    @pl.when(last & (f == nf - 1))
    def _():
        tile_wait(1 - slot)


def _experts(idx_flat, h, wg, wu, wd, layer, cap, tm=1024, tf=512):
    n, dx = h.shape
    _, E, d, dff = wg.shape
    tm = min(tm, cap)
    nf = dff // tf
    nrt = cap // tm
    return pl.pallas_call(
        functools.partial(_expert_kernel, tm=tm, nf=nf, d=d, nrt=nrt),
        grid=(E, nrt, nf),
        in_specs=[pl.BlockSpec((cap,), lambda e, r, f: (e,), memory_space=pltpu.SMEM),
                  pl.BlockSpec((cap,), lambda e, r, f: (e,), memory_space=pltpu.SMEM),
                  pl.BlockSpec(memory_space=pl.ANY),
                  pl.BlockSpec((1, 1, d, tf), lambda e, r, f: (layer, e, 0, f)),
                  pl.BlockSpec((1, 1, d, tf), lambda e, r, f: (layer, e, 0, f)),
                  pl.BlockSpec((1, 1, tf, d), lambda e, r, f: (layer, e, f, 0))],
        out_specs=pl.BlockSpec((1, tm, d), lambda e, r, f: (e, r, 0)),
        out_shape=jax.ShapeDtypeStruct((E, cap, d), BF16),
        scratch_shapes=[pltpu.VMEM((2, tm, dx), F32), pltpu.VMEM((tm, d), BF16), pltpu.VMEM((tm, d), F32),
                        pltpu.SemaphoreType.DMA((2,))],
        compiler_params=_cp(("arbitrary", "arbitrary", "arbitrary")),
    )(idx_flat, jnp.roll(idx_flat, -tm), h, wg, wu, wd)


COMBINE_WIN_LOG2 = 6
COMBINE_WIN = 1 << COMBINE_WIN_LOG2
ROW_ALIGN_LOG2 = 4
ROW_ALIGN = 1 << ROW_ALIGN_LOG2


def _combine_kernel(st_ref, x_ref, pos_ref, ye_ref, o_ref, buf, sems, buf2, sem2, p_s, *, TT, cap, nt):
    i = pl.program_id(0)
    E = N_EXPERTS
    W = COMBINE_WIN

    def window(tile, e, w):
        s0 = st_ref[tile * E + e]
        lo = lax.shift_left(lax.shift_right_logical(s0, ROW_ALIGN_LOG2), ROW_ALIGN_LOG2) + W * w
        return jnp.minimum(lo, cap - W), lo

    def first_copy(tile, slot, e):
        a, _ = window(tile, e, 0)
        return pltpu.make_async_copy(ye_ref.at[e, pl.ds(pl.multiple_of(a, ROW_ALIGN), W)],
                                     buf.at[slot, pl.ds(e * W, W)], sems.at[slot, e])

    slot = lax.rem(i, 2)

    @pl.when(i == 0)
    def _():
        for e in range(E):
            first_copy(0, 0, e).start()

    @pl.when(i + 1 < nt)
    def _():
        for e in range(E):
            first_copy(i + 1, 1 - slot, e).start()

    lane = lax.broadcasted_iota(I32, (TT, W), 1)

    def onehot(pc, a, lo):
        hit = (pc - a == lane) & (pc >= lo) & (pc < lo + W)
        return jnp.where(hit, 1.0, 0.0).astype(BF16)

    lane2 = lax.broadcasted_iota(I32, (TT, 2 * W), 1)
    first = lane2 < W
    for e in range(0, E, 2):
        a0, lo0 = window(i, e, 0)
        a1, lo1 = window(i, e + 1, 0)
        pc = jnp.where(first, pos_ref[:, e:e + 1], pos_ref[:, e + 1:e + 2])
        a = jnp.where(first, a0, a1 - W)
        lo = jnp.where(first, lo0, lo1)
        hit = (pc - a == lane2) & (pc >= lo) & (pc < lo + W)
        p_s[:, e * W:(e + 2) * W] = jnp.where(hit, 1.0, 0.0).astype(BF16)
    for e in range(E):
        first_copy(i, slot, e).wait()
    o_ref[...] = x_ref[...] + _dot(p_s[...], buf[slot])

    for e in range(E):
        _, lo = window(i, e, 0)
        s1 = st_ref[(i + 1) * E + e]
        nwin = lax.shift_right_logical(s1 - lo + (W - 1), COMBINE_WIN_LOG2)

        def extra(w, carry):
            a2, lo2 = window(i, e, w)
            cp = pltpu.make_async_copy(ye_ref.at[e, pl.ds(pl.multiple_of(a2, ROW_ALIGN), W)], buf2, sem2)
            cp.start()
            cp.wait()
            o_ref[...] += _dot(onehot(pos_ref[:, e:e + 1], a2, lo2), buf2[...])
            return carry

        lax.fori_loop(1, nwin, extra, 0)


def _combine(starts, x, pos_t, ye, cap, TT=256):
    n, d = x.shape
    E = N_EXPERTS
    TT = min(TT, n)
    nt = n // TT
    grid_spec = pltpu.PrefetchScalarGridSpec(
        num_scalar_prefetch=1,
        grid=(nt,),
        in_specs=[pl.BlockSpec((TT, d), lambda i, st: (i, 0)),
                  pl.BlockSpec((TT, E), lambda i, st: (i, 0)),
                  pl.BlockSpec(memory_space=pl.ANY)],
        out_specs=pl.BlockSpec((TT, d), lambda i, st: (i, 0)),
        scratch_shapes=[pltpu.VMEM((2, E * COMBINE_WIN, d), BF16), pltpu.SemaphoreType.DMA((2, E)),
                        pltpu.VMEM((COMBINE_WIN, d), BF16), pltpu.SemaphoreType.DMA(()),
                        pltpu.VMEM((TT, E * COMBINE_WIN), BF16)],
    )
    return pl.pallas_call(
        functools.partial(_combine_kernel, TT=TT, cap=cap, nt=nt),
        grid_spec=grid_spec,
        out_shape=jax.ShapeDtypeStruct((n, d), F32),
        compiler_params=_cp(("arbitrary",)),
    )(starts, x, pos_t, ye)


def _moe(x, g, router, wg, wu, wd, layer):
    n, d = x.shape
    E = N_EXPERTS
    cap = EC_CAPACITY * n // E
    TT = min(256, n)
    h, aff = _router(x, g, router.T)
    pos, cs, mask = _select(aff, cap)
    idx = _compact(mask, n, cap)
    ye = _experts(idx, h, wg, wu, wd, layer, cap)
    starts = jnp.concatenate([jnp.zeros((E, 1), I32), cs[:, TT - 1::TT]], axis=1).T.reshape(-1)
    return _combine(starts, x, pos.T, ye, cap, TT=TT)


def _rmsnorm_kernel(x_ref, g_ref, o_ref):
    xv = x_ref[...]
    o_ref[...] = xv * lax.rsqrt(jnp.mean(xv * xv, axis=-1, keepdims=True) + EPS) * g_ref[...]


def _rmsnorm(x, g, tm=512):
    n, d = x.shape
    return pl.pallas_call(
        _rmsnorm_kernel,
        grid=(n // tm,),
        in_specs=[pl.BlockSpec((tm, d), lambda i: (i, 0)), pl.BlockSpec((1, d), lambda i: (0, 0))],
        out_specs=pl.BlockSpec((tm, d), lambda i: (i, 0)),
        out_shape=jax.ShapeDtypeStruct((n, d), F32),
        compiler_params=_cp(("parallel",)),
    )(x, g.reshape(1, d))


def _even_mixer(x, B, L, p):
    proj = _rms_matmul(x, p["norm"], p["w_in"]).reshape(B, L, 5 * D_GROUP)
    hf = _lru(proj, None, p["conv_w"], p["conv_b"], p["wcat"][0], p["ba"][0], p["bx"][0], p["lam"][0],
              reverse=False)
    a_out = _lru(proj, hf, p["conv_w"], p["conv_b"], p["wcat"][1], p["ba"][1], p["bx"][1], p["lam"][1],
                 reverse=True)
    inv_freq = ROPE_THETA ** (-jnp.arange(0, DIFF_DH, 2, dtype=F32) / DIFF_DH)
    tabs = _rope_tables(L, DIFF_DH, inv_freq)
    half = DIFF_DH // 2
    qr = _rope(proj, 2, tabs, sh_a=LANES - half, sh_b=half, scale=DIFF_DH ** -0.5 * math.log2(math.e))
    kr = _rope(proj, 3, tabs, sh_a=LANES - half, sh_b=half, scale=1.0)
    vt = _v_transpose(proj, 4)
    b_out = _diff_attention(qr, kr, vt, p["diff_lam"], p["subln"], p["lam_init"])
    return _out_matmul(a_out.reshape(B * L, D_GROUP), b_out.reshape(B * L, D_GROUP), p["w_out"], x)


def _odd_mixer(x, B, L, p):
    proj = _rms_matmul(x, p["norm"], p["w_in"]).reshape(B, L, 7 * D_GROUP)
    c_out = _hyena_h(proj, p["conv_w"], p["conv_b"], p["w1"], p["b1"], p["w2"], p["b2"], p["w3"], p["freq"],
                   p["bias"])
    inv_freq = 1.0 / (10000.0 ** jnp.linspace(0.0, 1.0, RET_DH // 2, dtype=F32))
    tabs = _rope_tables(L, RET_DH, inv_freq)
    half = RET_DH // 2
    qr = _rope(proj, 3, tabs, sh_a=LANES - half, sh_b=half, scale=1.0)
    kr = _rope(proj, 4, tabs, sh_a=LANES - half, sh_b=half, scale=RET_DH ** -0.5)
    d_out = _retention(proj, qr, kr, p["rho"], 5 * N_HEAD_BLOCKS, 6 * N_HEAD_BLOCKS)
    return _out_matmul(c_out.reshape(B * L, D_GROUP), d_out.reshape(B * L, D_GROUP), p["w_out"], x)


def _trunk(x3, layers, final_norm):
    B, L, D = x3.shape
    x = x3.reshape(B * L, D)
    for kind, mp, ep in layers:
        x = (_even_mixer if kind == "even" else _odd_mixer)(x, B, L, mp)
        x = _moe(x, ep["norm"], ep["router"], ep["wg"], ep["wu"], ep["wd"], ep["layer"])
    return _rmsnorm(x, final_norm).reshape(B, L, D)


def kernel(x_prompt, x_sample, ev_norm, ev_w_in, ev_conv_w, ev_conv_b, ev_rg_wa, ev_rg_ba, ev_rg_wx, ev_rg_bx, ev_rg_lam, ev_diff_lam, ev_subln, ev_w_out, od_norm, od_w_in, od_conv_w, od_conv_b, od_flt_w1, od_flt_b1, od_flt_w2, od_flt_b2, od_flt_w3, od_flt_freq, od_flt_bias, od_ret_rho, od_w_out, moe_norm, moe_router, moe_w_gate, moe_w_up, moe_w_down, final_norm):
    depth = moe_norm.shape[0]
    wg_bf, wu_bf, wd_bf = moe_w_gate.astype(BF16), moe_w_up.astype(BF16), moe_w_down.astype(BF16)
    layers = []
    for layer in range(depth):
        j = layer // 2
        if layer % 2 == 0:
            mp = dict(norm=ev_norm[j], w_in=ev_w_in[j].astype(BF16), conv_w=ev_conv_w[j], conv_b=ev_conv_b[j],
                      wcat=jnp.concatenate([ev_rg_wa[j], ev_rg_wx[j]], axis=-1).astype(BF16),
                      ba=ev_rg_ba[j], bx=ev_rg_bx[j], lam=ev_rg_lam[j], diff_lam=ev_diff_lam[j],
                      subln=ev_subln[j], w_out=ev_w_out[j].astype(BF16),
                      lam_init=0.8 - 0.6 * math.exp(-0.3 * layer))
            kind = "even"
        else:
            mp = dict(norm=od_norm[j], w_in=od_w_in[j].astype(BF16), conv_w=od_conv_w[j], conv_b=od_conv_b[j],
                      w1=od_flt_w1[j], b1=od_flt_b1[j], w2=od_flt_w2[j], b2=od_flt_b2[j], w3=od_flt_w3[j],
                      freq=od_flt_freq[j], bias=od_flt_bias[j], rho=od_ret_rho[j],
                      w_out=od_w_out[j].astype(BF16))
            kind = "odd"
        ep = dict(norm=moe_norm[layer], router=moe_router[layer], wg=wg_bf, wu=wu_bf, wd=wd_bf, layer=layer)
        layers.append((kind, mp, ep))
    return (_trunk(x_prompt, layers, final_norm), _trunk(x_sample, layers, final_norm))
```
